```python
import math
import jax, jax.numpy as jnp
from jax import lax
import numpy as np

D_MODEL = 1024
BATCH = 2
SEQ = 8192
DEPTH = 4
DEC_BATCH = 32
DEC_SEQ = 4
PAST_LEN = 8192
PAGE_SIZE = 128

N_ATT = (DEPTH + 1) // 2
N_GLA = DEPTH // 2
A_HEADS = 4
A_DK = 64
A_DV = 2 * A_DK
B_HEADS = 4
B_DK = 128
B_DV = 128
CONV_W = 4
C_HEADS = 4
C_DK = 128
C_DV = 256
GLA_LOWRANK = 16
GLA_TAU = 16.0
D_FF = 2816
N_BUCKETS = 32
MAX_DIST = 128
Q_BLOCK = 128
DELTA_CHUNK = 64
GLA_CHUNK = 64
EPS = 1e-6
NEG = -1e30
OUT_SCALE = 0.5

A_QK = A_HEADS * 2 * A_DK
A_V = A_HEADS * A_DV
B_QK = B_HEADS * B_DK
B_V = B_HEADS * B_DV
B_CONV_DIM = 2 * B_QK + B_V
MIX_EVEN = A_V + B_V
IN_EVEN = 2 * A_QK + A_V + B_CONV_DIM + B_V + 2 * B_HEADS
C_QK = C_HEADS * C_DK
C_V = C_HEADS * C_DV
IN_ODD = 2 * C_QK + 2 * C_V + GLA_LOWRANK

kernel_name = 'hybrid_diffattn_gdn_gla_macaron_step'


def rmsnorm(x, g):
    xf = x.astype(jnp.float32)
    y = xf * lax.rsqrt(jnp.mean(xf * xf, axis=-1, keepdims=True) + EPS)
    return (y * g.astype(jnp.float32)).astype(x.dtype)


def l2norm(x):
    xf = x.astype(jnp.float32)
    return xf * lax.rsqrt(jnp.sum(xf * xf, axis=-1, keepdims=True) + EPS)


def split_cols(x, sizes):
    idx, acc = [], 0
    for s in sizes[:-1]:
        acc += s
        idx.append(acc)
    return jnp.split(x, idx, axis=-1)


def swiglu(x, w_in, w_out):
    gate, up = jnp.split(x @ w_in, 2, axis=-1)
    return (jax.nn.silu(gate) * up) @ w_out


def t5_bucket(rel):
    n = jnp.maximum(rel, 0)
    exact = N_BUCKETS // 2
    nf = jnp.maximum(n, 1).astype(jnp.float32)
    large = exact + (jnp.log(nf / exact) / math.log(MAX_DIST / exact) * (N_BUCKETS - exact)).astype(jnp.int32)
    return jnp.where(n < exact, n, jnp.minimum(large, N_BUCKETS - 1))


def diff_attn_core(q, k, v, q_pos, k_pos, rel_bias, lam):
    s = jnp.einsum('bqhcd,bkhcd->bchqk', q, k).astype(jnp.float32) * (A_DK ** -0.5)
    rel = q_pos[:, None] - k_pos[None, :]
    bias = jnp.transpose(rel_bias[t5_bucket(rel)], (2, 0, 1)).astype(jnp.float32)
    s = jnp.where(rel >= 0, s + bias, NEG)
    p = jax.nn.softmax(s, axis=-1)
    attn = p[:, 0] - lam * p[:, 1]
    return jnp.einsum('bhqk,bkhd->bqhd', attn.astype(v.dtype), v)


def gated_delta_chunked(q, k, v, g, beta, S0):
    Bsz, T, H, dk = q.shape
    dv = v.shape[-1]
    C = DELTA_CHUNK if T % DELTA_CHUNK == 0 else T
    n = T // C
    chunks = lambda a: a.reshape((Bsz, n, C) + a.shape[2:]).swapaxes(0, 1)
    incl = jnp.tril(jnp.ones((C, C), dtype=bool))
    strict = jnp.tril(jnp.ones((C, C), dtype=bool), -1)
    eye = jnp.eye(C, dtype=jnp.float32)

    def step(S, xs):
        qc, kc, vc, gc, bc = xs
        gcum = jnp.cumsum(gc, axis=1).swapaxes(1, 2)
        diff = gcum[..., :, None] - gcum[..., None, :]
        dec = jnp.where(incl, jnp.exp(jnp.where(incl, diff, 0.0)), 0.0)
        bh = bc.swapaxes(1, 2)
        kk = jnp.einsum('bihd,bjhd->bhij', kc, kc)
        m = eye + jnp.where(strict, kk * dec * bh[..., :, None], 0.0)
        rhs = jnp.concatenate([vc.swapaxes(1, 2) * bh[..., None],
                               kc.swapaxes(1, 2) * (bh * jnp.exp(gcum))[..., None]], axis=-1)
        uw = lax.linalg.triangular_solve(m, rhs, left_side=True, lower=True, unit_diagonal=True)
        u, wk = uw[..., :dv], uw[..., dv:]
        v_new = u - jnp.einsum('bhck,bhkv->bhcv', wk, S)
        qk = jnp.einsum('bihd,bjhd->bhij', qc, kc) * dec
        o = (jnp.einsum('bhck,bhkv->bchv', qc.swapaxes(1, 2) * jnp.exp(gcum)[..., None], S)
             + jnp.einsum('bhij,bhjv->bihv', qk, v_new))
        glast = gcum[..., -1]
        kdec = kc.swapaxes(1, 2) * jnp.exp(glast[..., None] - gcum)[..., None]
        S = S * jnp.exp(glast)[..., None, None] + jnp.einsum('bhck,bhcv->bhkv', kdec, v_new)
        return S, o

    S, o = lax.scan(step, S0, (chunks(q), chunks(k), chunks(v), chunks(g), chunks(beta)))
    return o.swapaxes(0, 1).reshape(Bsz, T, H, dv), S


def gla_chunked(q, k, v, gk, S0):
    Bsz, T, H, dk = q.shape
    dv = v.shape[-1]
    C = GLA_CHUNK if T % GLA_CHUNK == 0 else T
    n = T // C
    chunks = lambda a: a.reshape((Bsz, n, C) + a.shape[2:]).swapaxes(0, 1)
    incl = jnp.tril(jnp.ones((C, C), dtype=bool))[None, :, :, None, None]

    def step(S, xs):
        qc, kc, vc, gc = xs
        b = jnp.cumsum(gc, axis=1)
        diff = b[:, :, None] - b[:, None, :]
        dec = jnp.where(incl, jnp.exp(jnp.where(incl, diff, 0.0)), 0.0)
        a = jnp.einsum('bihd,bjhd,bijhd->bhij', qc, kc, dec)
        o = (jnp.einsum('bihk,bhkv->bihv', qc * jnp.exp(b), S)
             + jnp.einsum('bhij,bjhv->bihv', a, vc))
        blast = b[:, -1]
        S = S * jnp.exp(blast)[..., None] + jnp.einsum('bchk,bchv->bhkv', kc * jnp.exp(blast[:, None] - b), vc)
        return S, o

    S, o = lax.scan(step, S0, (chunks(q), chunks(k), chunks(v), chunks(gk)))
    return o.swapaxes(0, 1).reshape(Bsz, T, H, dv), S


def even_mixer(h, li, layer_idx, w, past):
    Bsz, T, _ = h.shape
    f32 = jnp.float32
    aq, ak, av, bqkv, bz, ba, bb = split_cols(h @ w['w_in_even'][li],
                                              [A_QK, A_QK, A_V, B_CONV_DIM, B_V, B_HEADS, B_HEADS])
    q = rmsnorm(aq.reshape(Bsz, T, A_HEADS, 2, A_DK), w['a_qk_norm'][li, 0])
    k = rmsnorm(ak.reshape(Bsz, T, A_HEADS, 2, A_DK), w['a_qk_norm'][li, 1])
    v = av.reshape(Bsz, T, A_HEADS, A_DV)
    lq1, lk1, lq2, lk2 = w['a_lambda'][li].astype(f32)
    lam_init = 0.8 - 0.6 * math.exp(-0.3 * layer_idx)
    lam = jnp.exp(jnp.sum(lq1 * lk1)) - jnp.exp(jnp.sum(lq2 * lk2)) + lam_init
    rel_bias = w['rel_bias']
    if past is None:
        pos = jnp.arange(T, dtype=jnp.int32)
        qb_len = Q_BLOCK if T % Q_BLOCK == 0 else T
        nblk = T // qb_len
        qb = q.reshape(Bsz, nblk, qb_len, A_HEADS, 2, A_DK).swapaxes(0, 1)
        ob = lax.map(lambda a: diff_attn_core(a[0], k, v, a[1], pos, rel_bias, lam),
                     (qb, pos.reshape(nblk, qb_len)))
        oa = ob.swapaxes(0, 1).reshape(Bsz, T, A_HEADS, A_DV)
        S0 = jnp.zeros((Bsz, B_HEADS, B_DK, B_DV), f32)
        conv0 = jnp.zeros((Bsz, CONV_W - 1, B_CONV_DIM), h.dtype)
    else:
        k_past, v_past, S0, conv0 = past
        P = k_past.shape[1]
        k_all = jnp.concatenate([k_past.astype(k.dtype), k], axis=1)
        v_all = jnp.concatenate([v_past.astype(v.dtype), v], axis=1)
        q_pos = P + jnp.arange(T, dtype=jnp.int32)
        k_pos = jnp.arange(P + T, dtype=jnp.int32)
        oa = diff_attn_core(q, k_all, v_all, q_pos, k_pos, rel_bias, lam)
    oa = rmsnorm(oa, w['a_subln'][li]) * (1.0 - lam_init)
    xc = jnp.concatenate([conv0.astype(bqkv.dtype), bqkv], axis=1)
    cw = w['b_conv'][li]
    conv = xc[:, 0:T] * cw[0]
    for j in range(1, CONV_W):
        conv = conv + xc[:, j:j + T] * cw[j]
    conv_new = xc[:, T:]
    bq, bk, bv = jnp.split(jax.nn.silu(conv), 3, axis=-1)
    bq = l2norm(bq.reshape(Bsz, T, B_HEADS, B_DK)) * (B_DK ** -0.5)
    bk = l2norm(bk.reshape(Bsz, T, B_HEADS, B_DK))
    bv = bv.reshape(Bsz, T, B_HEADS, B_DV).astype(f32)
    g = -jnp.exp(w['b_a_log'][li].astype(f32)) * jax.nn.softplus(ba.astype(f32) + w['b_dt_bias'][li].astype(f32))
    beta = jax.nn.sigmoid(bb.astype(f32))
    ob, S_new = gated_delta_chunked(bq, bk, bv, g, beta, S0.astype(f32))
    ob = rmsnorm(ob.astype(h.dtype), w['b_norm'][li]) * jax.nn.silu(bz.reshape(Bsz, T, B_HEADS, B_DV))
    out = jnp.concatenate([oa.reshape(Bsz, T, A_V), ob.reshape(Bsz, T, B_V)], axis=-1) @ w['w_out_even'][li]
    k_rows = k.reshape(Bsz, T, A_HEADS, 2 * A_DK)
    return out, k_rows, v, S_new.astype(h.dtype), conv_new


def odd_mixer(h, li, w, S0):
    Bsz, T, _ = h.shape
    f32 = jnp.float32
    q, k, v, r, glr = split_cols(h @ w['w_in_odd'][li], [C_QK, C_QK, C_V, C_V, GLA_LOWRANK])
    gk = jax.nn.log_sigmoid((glr @ w['c_gate_w2'][li] + w['c_gate_b'][li]).astype(f32)) / GLA_TAU
    q = q.reshape(Bsz, T, C_HEADS, C_DK).astype(f32) * (C_DK ** -0.5)
    k = k.reshape(Bsz, T, C_HEADS, C_DK).astype(f32)
    v = v.reshape(Bsz, T, C_HEADS, C_DV).astype(f32)
    o, S_new = gla_chunked(q, k, v, gk.reshape(Bsz, T, C_HEADS, C_DK), S0.astype(f32))
    o = rmsnorm(o.astype(h.dtype), w['c_norm'][li]) * jax.nn.silu(r.reshape(Bsz, T, C_HEADS, C_DV))
    return o.reshape(Bsz, T, C_V) @ w['w_out_odd'][li], S_new.astype(h.dtype)


def trunk(x, w, sample_state):
    Bsz = x.shape[0]
    k_rows, v_rows, deltas, convs, glas = [], [], [], [], []
    for i in range(DEPTH):
        li = i // 2
        x = x + 0.5 * swiglu(rmsnorm(x, w['norm_ffn1'][i]), w['ffn_in'][i, 0], w['ffn_out'][i, 0])
        h = rmsnorm(x, w['norm_mix'][i])
        if i % 2 == 0:
            if sample_state is None:
                past = None
            else:
                cache_k, cache_v, page_table, st_delta, st_conv, _ = sample_state
                kp = cache_k[li][page_table].reshape(Bsz, -1, A_HEADS, 2, A_DK)
                vp = cache_v[li][page_table].reshape(Bsz, -1, A_HEADS, A_DV)
                past = (kp, vp, st_delta[li], st_conv[li])
            mix, kr, vr, sd, sc = even_mixer(h, li, i, w, past)
            k_rows.append(kr)
            v_rows.append(vr)
            deltas.append(sd)
            convs.append(sc)
        else:
            if sample_state is None:
                s0 = jnp.zeros((Bsz, C_HEADS, C_DK, C_DV), x.dtype)
            else:
                s0 = sample_state[5][li]
            mix, sg = odd_mixer(h, li, w, s0)
            glas.append(sg)
        x = x + mix
        x = x + 0.5 * swiglu(rmsnorm(x, w['norm_ffn2'][i]), w['ffn_in'][i, 1], w['ffn_out'][i, 1])
    return x, jnp.stack(k_rows), jnp.stack(v_rows), jnp.stack(deltas), jnp.stack(convs), jnp.stack(glas)


def setup_inputs(seed: int = 0) -> dict:
    key = jax.random.key(seed)
    ks = jax.random.split(key, 32)
    f32 = jnp.float32
    n_pages = PAST_LEN // PAGE_SIZE
    n_used = DEC_BATCH * n_pages
    n_pool = n_used + n_used // 4

    def nrm(k, shape, scale):
        return jax.random.normal(k, shape, f32) * scale

    def gain(k, shape):
        return 1.0 + nrm(k, shape, 0.05)

    x_prompt = nrm(ks[0], (BATCH, SEQ, D_MODEL), 1.0)
    x_sample = nrm(ks[1], (DEC_BATCH, DEC_SEQ, D_MODEL), 1.0)
    cache_k = nrm(ks[2], (N_ATT, n_pool, PAGE_SIZE, A_HEADS, 2 * A_DK), 1.0)
    cache_v = nrm(ks[3], (N_ATT, n_pool, PAGE_SIZE, A_HEADS, A_DV), 1.0)
    state_delta = nrm(ks[4], (N_ATT, DEC_BATCH, B_HEADS, B_DK, B_DV), B_DK ** -0.5)
    state_conv = nrm(ks[5], (N_ATT, DEC_BATCH, CONV_W - 1, B_CONV_DIM), 1.0)
    state_gla = nrm(ks[6], (N_GLA, DEC_BATCH, C_HEADS, C_DK, C_DV), C_DK ** -0.5)
    page_table = jax.random.permutation(ks[7], n_pool)[:n_used].reshape(DEC_BATCH, n_pages).astype(jnp.int32)
    rel_bias = nrm(ks[8], (N_BUCKETS, A_HEADS), 0.5)
    norm_ffn1 = gain(ks[9], (DEPTH, D_MODEL))
    norm_mix = gain(ks[10], (DEPTH, D_MODEL))
    norm_ffn2 = gain(ks[11], (DEPTH, D_MODEL))
    ffn_in = nrm(ks[12], (DEPTH, 2, D_MODEL, 2 * D_FF), D_MODEL ** -0.5)
    ffn_out = nrm(ks[13], (DEPTH, 2, D_FF, D_MODEL), OUT_SCALE * D_FF ** -0.5)
    w_in_even = nrm(ks[14], (N_ATT, D_MODEL, IN_EVEN), D_MODEL ** -0.5)
    a_qk_norm = gain(ks[15], (N_ATT, 2, A_DK))
    a_lambda = nrm(ks[16], (N_ATT, 4, A_DK), 0.1)
    a_subln = gain(ks[17], (N_ATT, A_DV))
    b_conv = nrm(ks[18], (N_ATT, CONV_W, B_CONV_DIM), CONV_W ** -0.5)
    b_a_log = jnp.log(jax.random.uniform(ks[19], (N_ATT, B_HEADS), f32, 1.0, 16.0))
    dt = jnp.exp(jax.random.uniform(ks[20], (N_ATT, B_HEADS), f32, math.log(1e-3), math.log(1e-1)))
    b_dt_bias = dt + jnp.log(-jnp.expm1(-dt))
    b_norm = gain(ks[21], (N_ATT, B_DV))
    w_out_even = nrm(ks[22], (N_ATT, MIX_EVEN, D_MODEL), OUT_SCALE * MIX_EVEN ** -0.5)
    w_in_odd = nrm(ks[23], (N_GLA, D_MODEL, IN_ODD), D_MODEL ** -0.5)
    c_gate_w2 = nrm(ks[24], (N_GLA, GLA_LOWRANK, C_QK), GLA_LOWRANK ** -0.5)
    c_gate_b = nrm(ks[25], (N_GLA, C_QK), 0.1)
    c_norm = gain(ks[26], (N_GLA, C_DV))
    w_out_odd = nrm(ks[27], (N_GLA, C_V, D_MODEL), OUT_SCALE * C_V ** -0.5)
    return {'x_prompt': x_prompt, 'x_sample': x_sample, 'cache_k': cache_k, 'cache_v': cache_v,
            'state_delta': state_delta, 'state_conv': state_conv, 'state_gla': state_gla,
            'page_table': page_table, 'rel_bias': rel_bias, 'norm_ffn1': norm_ffn1, 'norm_mix': norm_mix,
            'norm_ffn2': norm_ffn2, 'ffn_in': ffn_in, 'ffn_out': ffn_out, 'w_in_even': w_in_even,
            'a_qk_norm': a_qk_norm, 'a_lambda': a_lambda, 'a_subln': a_subln, 'b_conv': b_conv,
            'b_a_log': b_a_log, 'b_dt_bias': b_dt_bias, 'b_norm': b_norm, 'w_out_even': w_out_even,
            'w_in_odd': w_in_odd, 'c_gate_w2': c_gate_w2, 'c_gate_b': c_gate_b, 'c_norm': c_norm,
            'w_out_odd': w_out_odd}


def reference(x_prompt, x_sample, cache_k, cache_v, state_delta, state_conv, state_gla, page_table,
              rel_bias, norm_ffn1, norm_mix, norm_ffn2, ffn_in, ffn_out, w_in_even, a_qk_norm, a_lambda,
              a_subln, b_conv, b_a_log, b_dt_bias, b_norm, w_out_even, w_in_odd, c_gate_w2, c_gate_b,
              c_norm, w_out_odd):
    w = dict(rel_bias=rel_bias, norm_ffn1=norm_ffn1, norm_mix=norm_mix, norm_ffn2=norm_ffn2,
             ffn_in=ffn_in, ffn_out=ffn_out, w_in_even=w_in_even, a_qk_norm=a_qk_norm,
             a_lambda=a_lambda, a_subln=a_subln, b_conv=b_conv, b_a_log=b_a_log, b_dt_bias=b_dt_bias,
             b_norm=b_norm, w_out_even=w_out_even, w_in_odd=w_in_odd, c_gate_w2=c_gate_w2,
             c_gate_b=c_gate_b, c_norm=c_norm, w_out_odd=w_out_odd)
    y_prompt, k_prompt, v_prompt, delta_prompt, conv_prompt, gla_prompt = trunk(x_prompt, w, None)
    y_sample, k_sample, v_sample, delta_sample, conv_sample, gla_sample = trunk(
        x_sample, w, (cache_k, cache_v, page_table, state_delta, state_conv, state_gla))
    return (y_prompt, y_sample, k_prompt, v_prompt, delta_prompt, conv_prompt, gla_prompt,
            k_sample, v_sample, delta_sample, conv_sample, gla_sample)
```

```python
import functools
import math

import jax
import jax.numpy as jnp
from jax import lax
from jax.experimental import pallas as pl
from jax.experimental.pallas import tpu as pltpu

F32 = jnp.float32
BF16 = jnp.bfloat16

A_HEADS, A_DK, A_DV = 4, 64, 128
B_HEADS, B_DK, B_DV = 4, 128, 128
CONV_W = 4
C_HEADS, C_DK, C_DV = 4, 128, 256
GLA_LOWRANK, GLA_TAU = 16, 16.0
N_BUCKETS, MAX_DIST = 32, 128
EPS = 1e-6
NEG = -1e30
LOG2E = 1.4426950408889634

A_QK = A_HEADS * 2 * A_DK
A_V = A_HEADS * A_DV
B_QK = B_HEADS * B_DK
B_V = B_HEADS * B_DV
B_CONV_DIM = 2 * B_QK + B_V
C_QK = C_HEADS * C_DK
C_V = C_HEADS * C_DV

LANES = 128
SUBLANES = 8
VMEM_LIMIT = 56 * 2 ** 20

ROWS_DENSE = 512
ROWS_INPROJ = 256
ATT_TILE = 256
SEQ_CHUNK = 64
GLA_SUB = 16
PAGES_PER_STEP = 8

NT = (((1,), (1,)), ((), ()))
TN = (((0,), (0,)), ((), ()))


def _cp(*sem):
    return pltpu.CompilerParams(dimension_semantics=sem, vmem_limit_bytes=VMEM_LIMIT)


def _mm(a, b, dn=None):
    if dn is None:
        return jnp.dot(a, b, preferred_element_type=F32)
    return lax.dot_general(a, b, dn, preferred_element_type=F32)


def _split2(x):
    hi = x.astype(BF16)
    lo = (x - hi.astype(F32)).astype(BF16)
    return hi, lo


def _split3(x):
    hi = x.astype(BF16)
    r = x - hi.astype(F32)
    mid = r.astype(BF16)
    lo = (r - mid.astype(F32)).astype(BF16)
    return hi, mid, lo


def _mm3(a, b):
    ah, al = _split2(a)
    bh, bl = _split2(b)
    return _mm(ah, bh) + (_mm(ah, bl) + _mm(al, bh))


def _mm_exact_lhs(a_bf, x):
    hi, mid, lo = _split3(x)
    return _mm(a_bf, hi) + (_mm(a_bf, mid) + _mm(a_bf, lo))


def _sigmoid(x):
    return 1.0 / (1.0 + jnp.exp(-x))


def _silu(x):
    return x * _sigmoid(x)


def _softplus(x):
    return jnp.maximum(x, 0.0) + jnp.log(1.0 + jnp.exp(-jnp.abs(x)))


def _rmsnorm(xf, g):
    return xf * lax.rsqrt(jnp.mean(xf * xf, axis=-1, keepdims=True) + EPS) * g


def _iota(shape, dim):
    return lax.broadcasted_iota(jnp.int32, shape, dim)


def _row_to_col(row, n):
    eye = _iota((n, n), 0) == _iota((n, n), 1)
    return jnp.sum(jnp.where(eye, jnp.broadcast_to(row, (n, n)), 0.0), axis=1, keepdims=True)


def _ffn_kernel(x_ref, g_ref, wg_ref, wu_ref, wo_ref, o_ref, h_ref):
    j = pl.program_id(1)

    @pl.when(j == 0)
    def _():
        xf = x_ref[...]
        h_ref[...] = _rmsnorm(xf, g_ref[...]).astype(BF16)
        o_ref[...] = xf

    h = h_ref[...]
    gate = _mm(h, wg_ref[...])
    up = _mm(h, wu_ref[...])
    a = (_silu(gate) * up).astype(BF16)
    o_ref[...] += 0.5 * _mm(a, wo_ref[...])


def _ffn(x, g, w_in, w_out, layer, which, tm):
    m, d = x.shape
    f = w_out.shape[2]
    nf = 2
    tf = f // nf
    assert f % nf == 0 and tf % LANES == 0 and m % tm == 0
    return pl.pallas_call(
        _ffn_kernel,
        grid=(m // tm, nf),
        in_specs=[
            pl.BlockSpec((tm, d), lambda i, j: (i, 0)),
            pl.BlockSpec((None, 1, d), lambda i, j: (layer, 0, 0)),
            pl.BlockSpec((None, None, d, tf), lambda i, j: (layer, which, 0, j)),
            pl.BlockSpec((None, None, d, tf), lambda i, j: (layer, which, 0, j + nf)),
            pl.BlockSpec((None, None, tf, d), lambda i, j: (layer, which, j, 0)),
        ],
        out_specs=pl.BlockSpec((tm, d), lambda i, j: (i, 0)),
        out_shape=jax.ShapeDtypeStruct((m, d), F32),
        scratch_shapes=[pltpu.VMEM((tm, d), BF16)],
        compiler_params=_cp("parallel", "arbitrary"),
        name="ffn",
    )(x, g, w_in, w_in, w_out)


def _group_rms(x, grp_bf, gain, width):
    hi, lo = _split2(x * x)
    ms = (_mm(hi, grp_bf) + _mm(lo, grp_bf)) * (1.0 / width)
    return x * lax.rsqrt(ms + EPS) * gain


def _inproj_even_kernel(x_ref, g_ref, w_ref, ws_ref, grp_ref, qg_ref, kg_ref, gp_ref,
                        q_ref, k32_ref, kbf_ref, v32_ref, vaug_ref, bqkv_ref, bz_ref, gb_ref):
    h = _rmsnorm(x_ref[...], g_ref[...]).astype(BF16)
    grp = grp_ref[...]

    aq = _mm(h, w_ref[:, 0:A_QK])
    q = _group_rms(aq, grp, qg_ref[...], A_DK)
    q_ref[...] = (q * (A_DK ** -0.5 * LOG2E)).astype(BF16)

    ak = _mm(h, w_ref[:, A_QK:2 * A_QK])
    k = _group_rms(ak, grp, kg_ref[...], A_DK)
    k32_ref[...] = k
    kbf_ref[...] = k.astype(BF16)

    v = _mm(h, w_ref[:, 2 * A_QK:2 * A_QK + A_V])
    v32_ref[...] = v
    ones = jnp.ones((v.shape[0], A_DV), BF16)
    for hd in range(A_HEADS):
        vaug_ref[:, 2 * A_DV * hd:2 * A_DV * hd + A_DV] = v[:, A_DV * hd:A_DV * (hd + 1)].astype(BF16)
        vaug_ref[:, 2 * A_DV * hd + A_DV:2 * A_DV * (hd + 1)] = ones

    o = 2 * A_QK + A_V
    bqkv_ref[...] = _mm(h, w_ref[:, o:o + B_CONV_DIM])
    o += B_CONV_DIM
    bz_ref[...] = _mm(h, w_ref[:, o:o + B_V])

    y = _mm(h, ws_ref[...])
    lane = _iota(y.shape, 1)
    g = gp_ref[0:1, :] * _softplus(y + gp_ref[1:2, :])
    gb_ref[...] = jnp.where(lane < B_HEADS, g, jnp.where(lane < 2 * B_HEADS, _sigmoid(y), 0.0))


def _inproj_even(x, g, w, ws, grp, qg, kg, gp, layer, li, tm):
    m, d = x.shape
    wmain = 2 * A_QK + A_V + B_CONV_DIM + B_V
    row = lambda n: pl.BlockSpec((tm, n), lambda i: (i, 0))
    out_shapes = [
        jax.ShapeDtypeStruct((m, A_QK), BF16),
        jax.ShapeDtypeStruct((m, A_QK), F32),
        jax.ShapeDtypeStruct((m, A_QK), BF16),
        jax.ShapeDtypeStruct((m, A_V), F32),
        jax.ShapeDtypeStruct((m, 2 * A_V), BF16),
        jax.ShapeDtypeStruct((m, B_CONV_DIM), F32),
        jax.ShapeDtypeStruct((m, B_V), F32),
        jax.ShapeDtypeStruct((m, LANES), F32),
    ]
    return pl.pallas_call(
        _inproj_even_kernel,
        grid=(m // tm,),
        in_specs=[
            row(d),
            pl.BlockSpec((None, 1, d), lambda i: (layer, 0, 0)),
            pl.BlockSpec((None, d, wmain), lambda i: (li, 0, 0)),
            pl.BlockSpec((None, d, LANES), lambda i: (li, 0, 0)),
            pl.BlockSpec((A_QK, A_QK), lambda i: (0, 0)),
            pl.BlockSpec((None, 1, A_QK), lambda i: (li, 0, 0)),
            pl.BlockSpec((None, 1, A_QK), lambda i: (li, 0, 0)),
            pl.BlockSpec((None, 2, LANES), lambda i: (li, 0, 0)),
        ],
        out_specs=[row(s.shape[1]) for s in out_shapes],
        out_shape=out_shapes,
        compiler_params=_cp("parallel"),
        name="inproj_even",
    )(x, g, w, ws, grp, qg, kg, gp)


def _bias_kernel(rb_ref, bp_ref, bs_ref, *, tq, past, tnew):
    h = pl.program_id(0)
    far = rb_ref[N_BUCKETS - 1, h]
    exact = N_BUCKETS // 2

    def table(rel):
        n = jnp.maximum(rel, 0)
        nf = jnp.maximum(n, 1).astype(F32)
        large = exact + (jnp.log(nf / exact) / math.log(MAX_DIST / exact) * (N_BUCKETS - exact)).astype(jnp.int32)
        bucket = jnp.where(n < exact, n, jnp.minimum(large, N_BUCKETS - 1))
        val = lax.fori_loop(0, N_BUCKETS, lambda b, acc: jnp.where(bucket == b, rb_ref[b, h], acc),
                            jnp.zeros(rel.shape, F32))
        return jnp.where(rel >= 0, (val - far) * LOG2E, NEG)

    a = _iota((tq, tq), 0)
    b = _iota((tq, tq), 1)
    for d in range(2):
        t = table(d * tq + a - b)
        bp_ref[d, 0:tq, :] = t
        bp_ref[d, tq:2 * tq, :] = t

    nk = past + LANES
    t_idx = _iota((2 * tnew, nk), 0) & (tnew - 1)
    key = _iota((2 * tnew, nk), 1)
    rel_new = jnp.where(key - past < tnew, t_idx - (key - past), -1)
    bs_ref[...] = table(jnp.where(key < past, past + t_idx - key, rel_new))


def _bias_tiles(rel_bias, tq, past, tnew):
    assert tnew & (tnew - 1) == 0 and 2 * tnew == SUBLANES
    return pl.pallas_call(
        functools.partial(_bias_kernel, tq=tq, past=past, tnew=tnew),
        grid=(A_HEADS,),
        in_specs=[pl.BlockSpec(memory_space=pltpu.SMEM)],
        out_specs=[
            pl.BlockSpec((None, 2, 2 * tq, tq), lambda h: (h, 0, 0, 0)),
            pl.BlockSpec((None, 2 * tnew, past + LANES), lambda h: (h, 0, 0)),
        ],
        out_shape=[
            jax.ShapeDtypeStruct((A_HEADS, 2, 2 * tq, tq), F32),
            jax.ShapeDtypeStruct((A_HEADS, 2 * tnew, past + LANES), F32),
        ],
        compiler_params=_cp("parallel"),
        name="t5_bias",
    )(rel_bias)


def _far_bucket_saturated(dist):
    exact = N_BUCKETS // 2
    v = math.log(dist / exact) / math.log(MAX_DIST / exact) * (N_BUCKETS - exact)
    return dist >= exact and exact + v >= N_BUCKETS - 1 + 1e-3


def _attn_prompt_kernel(sc_ref, q_ref, k_ref, v_ref, bias_ref, sg_ref, o_ref, m_ref, acc_ref, *, tq):
    i = pl.program_id(2)
    q = q_ref[...]
    lane = _iota(q.shape, 1)
    zero = jnp.zeros_like(q)
    qbd = jnp.concatenate([jnp.where(lane < A_DK, q, zero), jnp.where(lane >= A_DK, q, zero)], axis=0)
    m_ref[...] = jnp.full(m_ref.shape, NEG, F32)
    acc_ref[...] = jnp.zeros(acc_ref.shape, F32)

    def step(j, bias):
        start = pl.multiple_of(j * tq, tq)
        s = _mm(qbd, k_ref[pl.ds(start, tq), :], NT)
        if bias is not None:
            s = s + bias
        m_old = m_ref[...]
        m_new = jnp.maximum(m_old, jnp.max(s, axis=1, keepdims=True))
        alpha = jnp.exp2(m_old - m_new)
        p = jnp.exp2(s - m_new).astype(BF16)
        acc_ref[...] = alpha * acc_ref[...] + _mm(p, v_ref[pl.ds(start, tq), :])
        m_ref[...] = m_new

    def far(j, c):
        step(j, None)
        return c

    lax.fori_loop(0, jnp.maximum(i - 1, 0), far, 0)

    @pl.when(i >= 1)
    def _():
        step(i - 1, bias_ref[1])

    step(i, bias_ref[0])

    lam = sc_ref[0]
    acc = acc_ref[...]
    o = acc[:, 0:A_DV] / acc[:, A_DV:2 * A_DV]
    o = o[0:tq] - lam * o[tq:2 * tq]
    o_ref[...] = (_rmsnorm(o, sg_ref[...]) * sc_ref[1]).astype(BF16)


def _attn_prompt(sc, q, k, vaug, bias_p, sg, li, tq):
    bsz, t, _ = q.shape
    return pl.pallas_call(
        functools.partial(_attn_prompt_kernel, tq=tq),
        grid=(bsz, A_HEADS, t // tq),
        in_specs=[
            pl.BlockSpec(memory_space=pltpu.SMEM),
            pl.BlockSpec((None, tq, 2 * A_DK), lambda b, h, i: (b, i, h)),
            pl.BlockSpec((None, t, 2 * A_DK), lambda b, h, i: (b, 0, h)),
            pl.BlockSpec((None, t, 2 * A_DV), lambda b, h, i: (b, 0, h)),
            pl.BlockSpec((None, 2, 2 * tq, tq), lambda b, h, i: (h, 0, 0, 0)),
            pl.BlockSpec((None, 1, A_DV), lambda b, h, i: (li, 0, 0)),
        ],
        out_specs=pl.BlockSpec((None, tq, A_DV), lambda b, h, i: (b, i, h)),
        out_shape=jax.ShapeDtypeStruct((bsz, t, A_V), BF16),
        scratch_shapes=[pltpu.VMEM((2 * tq, 1), F32), pltpu.VMEM((2 * tq, 2 * A_DV), F32)],
        compiler_params=_cp("parallel", "parallel", "arbitrary"),
        name="attn_prompt",
    )(sc, q, k, vaug, bias_p, sg)


def _attn_sample_kernel(pt_ref, sc_ref, qbd_ref, bias_ref, biasn_ref, kn_ref, vn_ref, sg_ref, *rest, npg, tnew):
    k_refs = rest[:npg]
    v_refs = rest[npg:2 * npg]
    o_ref, m_ref, l_ref, acc_ref = rest[2 * npg:]
    pg = pl.program_id(1)
    qbd = qbd_ref[...]

    @pl.when(pg == 0)
    def _():
        m_ref[...] = jnp.full(m_ref.shape, NEG, F32)
        l_ref[...] = jnp.zeros(l_ref.shape, F32)
        acc_ref[...] = jnp.zeros(acc_ref.shape, F32)

    def update(s, vs):
        m_old = m_ref[...]
        m_new = jnp.maximum(m_old, jnp.max(s, axis=1, keepdims=True))
        alpha = jnp.exp2(m_old - m_new)
        p = jnp.exp2(s - m_new)
        l_ref[...] = alpha * l_ref[...] + jnp.sum(p, axis=1, keepdims=True)
        pb = p.astype(BF16)
        pv = _mm(pb[:, 0:LANES], vs[0])
        for g in range(1, len(vs)):
            pv = pv + _mm(pb[:, g * LANES:(g + 1) * LANES], vs[g])
        acc_ref[...] = alpha * acc_ref[...] + pv
        m_ref[...] = m_new

    s = jnp.concatenate([_mm(qbd, kr[...].astype(BF16), NT) for kr in k_refs], axis=1) + bias_ref[...]
    update(s, [vr[...].astype(BF16) for vr in v_refs])

    @pl.when(pg == pl.num_programs(1) - 1)
    def _():
        update(_mm(qbd, kn_ref[...], NT) + biasn_ref[...], [vn_ref[...]])
        lam = sc_ref[0]
        rows = 2 * tnew
        for h in range(A_HEADS):
            a = acc_ref[rows * h:rows * (h + 1), A_DV * h:A_DV * (h + 1)] / l_ref[rows * h:rows * (h + 1), :]
            o = a - lam * pltpu.roll(a, tnew, axis=0)
            o_ref[:, A_DV * h:A_DV * (h + 1)] = _rmsnorm(o, sg_ref[...]) * sc_ref[1]


def _attn_sample(page_table, sc, qbd, bias_s, knew, vnew, sg, cache_k, cache_v, li, tnew):
    bsz, nrow, _ = qbd.shape
    n_pages = page_table.shape[1]
    npg = PAGES_PER_STEP
    assert n_pages % npg == 0
    page = cache_k.shape[2]
    past = n_pages * page

    def kvspec(g):
        return pl.BlockSpec((None, None, page, A_QK), lambda b, pg, pt: (li, pt[b, pg * npg + g], 0, 0))

    grid_spec = pltpu.PrefetchScalarGridSpec(
        num_scalar_prefetch=1,
        grid=(bsz, n_pages // npg),
        in_specs=[
            pl.BlockSpec(memory_space=pltpu.SMEM),
            pl.BlockSpec((None, nrow, A_QK), lambda b, pg, pt: (b, 0, 0)),
            pl.BlockSpec((nrow, npg * page), lambda b, pg, pt: (0, pg)),
            pl.BlockSpec((nrow, LANES), lambda b, pg, pt: (0, past // LANES)),
            pl.BlockSpec((None, LANES, A_QK), lambda b, pg, pt: (b, 0, 0)),
            pl.BlockSpec((None, LANES, A_V), lambda b, pg, pt: (b, 0, 0)),
            pl.BlockSpec((None, 1, A_DV), lambda b, pg, pt: (li, 0, 0)),
        ] + [kvspec(g) for g in range(npg)] + [kvspec(g) for g in range(npg)],
        out_specs=pl.BlockSpec((None, 2 * tnew, A_V), lambda b, pg, pt: (b, 0, 0)),
        scratch_shapes=[pltpu.VMEM((nrow, 1), F32), pltpu.VMEM((nrow, 1), F32), pltpu.VMEM((nrow, A_V), F32)],
    )
    return pl.pallas_call(
        functools.partial(_attn_sample_kernel, npg=npg, tnew=tnew),
        grid_spec=grid_spec,
        out_shape=jax.ShapeDtypeStruct((bsz, 2 * tnew, A_V), F32),
        compiler_params=_cp("parallel", "arbitrary"),
        name="attn_sample",
    )(page_table, sc, qbd, bias_s, bias_s, knew, vnew, sg, *([cache_k] * npg), *([cache_v] * npg))


def _l2norm(x):
    return x * lax.rsqrt(jnp.sum(x * x, axis=-1, keepdims=True) + EPS)


def _unit_lower_inverse(a, c):
    eye = (_iota((c, c), 0) == _iota((c, c), 1)).astype(F32)
    p = eye - a
    q = a
    span = 2
    while span < c:
        q = _mm3(q, q)
        p = p + _mm3(p, q)
        span *= 2
    return p


def _delta_kernel(x_ref, z_ref, gb_ref, c0_ref, s0_ref, cw_ref, bn_ref, ob_ref, sout_ref, xbuf, s_scr, *, c):
    ci = pl.program_id(1)

    @pl.when(ci == 0)
    def _():
        xbuf[0:SUBLANES, :] = c0_ref[...]
        s_scr[...] = s0_ref[...]

    xbuf[SUBLANES:SUBLANES + c, :] = x_ref[...]
    first = SUBLANES - (CONV_W - 1)
    conv = xbuf[pl.ds(first, c), :] * cw_ref[0:1, :]
    for j in range(1, CONV_W):
        conv = conv + xbuf[pl.ds(first + j, c), :] * cw_ref[j:j + 1, :]
    xbuf[0:SUBLANES, :] = xbuf[c:c + SUBLANES, :]
    conv = _silu(conv)

    row = _iota((c, c), 0)
    col = _iota((c, c), 1)
    incl = row >= col
    strict = row > col
    lower_bf = incl.astype(BF16)
    ones_bf = jnp.ones((c, c), BF16)

    gb = gb_ref[...]
    gcum_all = _mm_exact_lhs(lower_bf, gb)

    for h in range(B_HEADS):
        sl = slice(B_DK * h, B_DK * (h + 1))
        q = _l2norm(conv[:, sl]) * (B_DK ** -0.5)
        k = _l2norm(conv[:, B_QK + B_DK * h:B_QK + B_DK * (h + 1)])
        v = conv[:, 2 * B_QK + B_DV * h:2 * B_QK + B_DV * (h + 1)]
        g = gb[:, h:h + 1]
        beta = gb[:, B_HEADS + h:B_HEADS + h + 1]
        gcol = gcum_all[:, h:h + 1]
        grow = _mm_exact_lhs(ones_bf, jnp.where(row <= col, jnp.broadcast_to(g, (c, c)), 0.0))
        dec = jnp.where(incl, jnp.exp(jnp.where(incl, gcol - grow, 0.0)), 0.0)

        kb = k.astype(BF16)
        kk = _mm(kb, kb, NT)
        t_inv = _unit_lower_inverse(jnp.where(strict, kk * dec * beta, 0.0), c)
        egc = jnp.exp(gcol)
        u = _mm3(t_inv, v * beta)
        wk = _mm3(t_inv, k * (beta * egc))

        s = s_scr[h]
        s_bf = s.astype(BF16)
        v_new = u - _mm(wk.astype(BF16), s_bf)
        vn_bf = v_new.astype(BF16)
        qk = _mm(q.astype(BF16), kb, NT) * dec
        o = _mm((q * egc).astype(BF16), s_bf) + _mm(qk.astype(BF16), vn_bf)
        glast = gcol[c - 1:c, :]
        kdec = k * jnp.exp(glast - gcol)
        s_scr[h] = s * jnp.exp(glast) + _mm(kdec.astype(BF16), vn_bf, TN)

        ob_ref[:, sl] = (_rmsnorm(o, bn_ref[...]) * _silu(z_ref[:, sl])).astype(BF16)

    @pl.when(ci == pl.num_programs(1) - 1)
    def _():
        sout_ref[...] = s_scr[...]


def _delta(bqkv, bz, gb, conv0, s0, cw, bn, li, c):
    bsz, t, _ = bqkv.shape
    assert t % c == 0
    tok = lambda n: pl.BlockSpec((None, c, n), lambda b, ci: (b, ci, 0))
    return pl.pallas_call(
        functools.partial(_delta_kernel, c=c),
        grid=(bsz, t // c),
        in_specs=[
            tok(B_CONV_DIM), tok(B_V), tok(LANES),
            pl.BlockSpec((None, SUBLANES, B_CONV_DIM), lambda b, ci: (b, 0, 0)),
            pl.BlockSpec((None, B_HEADS, B_DK, B_DV), lambda b, ci: (b, 0, 0, 0)),
            pl.BlockSpec((None, CONV_W, B_CONV_DIM), lambda b, ci: (li, 0, 0)),
            pl.BlockSpec((None, 1, B_DV), lambda b, ci: (li, 0, 0)),
        ],
        out_specs=[tok(B_V), pl.BlockSpec((None, B_HEADS, B_DK, B_DV), lambda b, ci: (b, 0, 0, 0))],
        out_shape=[jax.ShapeDtypeStruct((bsz, t, B_V), BF16),
                   jax.ShapeDtypeStruct((bsz, B_HEADS, B_DK, B_DV), F32)],
        scratch_shapes=[pltpu.VMEM((SUBLANES + c, B_CONV_DIM), F32),
                        pltpu.VMEM((B_HEADS, B_DK, B_DV), F32)],
        compiler_params=_cp("parallel", "arbitrary"),
        name="delta",
    )(bqkv, bz, gb, conv0, s0, cw, bn)


def _outproj_even_kernel(x_ref, oa_ref, ob_ref, w_ref, o_ref):
    o_ref[...] = x_ref[...] + (_mm(oa_ref[...], w_ref[0:A_V, :]) + _mm(ob_ref[...], w_ref[A_V:A_V + B_V, :]))


def _outproj_even(x, oa, ob, w, li, tm):
    m, d = x.shape
    return pl.pallas_call(
        _outproj_even_kernel,
        grid=(m // tm,),
        in_specs=[
            pl.BlockSpec((tm, d), lambda i: (i, 0)),
            pl.BlockSpec((tm, A_V), lambda i: (i, 0)),
            pl.BlockSpec((tm, B_V), lambda i: (i, 0)),
            pl.BlockSpec((None, A_V + B_V, d), lambda i: (li, 0, 0)),
        ],
        out_specs=pl.BlockSpec((tm, d), lambda i: (i, 0)),
        out_shape=jax.ShapeDtypeStruct((m, d), F32),
        compiler_params=_cp("parallel"),
        name="outproj_even",
    )(x, oa, ob, w)


def _inproj_odd_kernel(x_ref, g_ref, w_ref, ws_ref, w2_ref, gbias_ref, q_ref, k_ref, v_ref, r_ref, gk_ref):
    h = _rmsnorm(x_ref[...], g_ref[...]).astype(BF16)
    q_ref[...] = _mm(h, w_ref[:, 0:C_QK])
    k_ref[...] = _mm(h, w_ref[:, C_QK:2 * C_QK])
    v_ref[...] = _mm(h, w_ref[:, 2 * C_QK:2 * C_QK + C_V]).astype(BF16)
    r_ref[...] = _mm(h, w_ref[:, 2 * C_QK + C_V:2 * C_QK + 2 * C_V])
    glr = _mm(h, ws_ref[...]).astype(BF16)
    y = _mm(glr, w2_ref[...]) + gbias_ref[...]
    gk_ref[...] = -_softplus(-y) * (1.0 / GLA_TAU)


def _inproj_odd(x, g, w, ws, w2, gbias, layer, li, tm):
    m, d = x.shape
    wmain = 2 * C_QK + 2 * C_V
    row = lambda n: pl.BlockSpec((tm, n), lambda i: (i, 0))
    out_shapes = [
        jax.ShapeDtypeStruct((m, C_QK), F32),
        jax.ShapeDtypeStruct((m, C_QK), F32),
        jax.ShapeDtypeStruct((m, C_V), BF16),
        jax.ShapeDtypeStruct((m, C_V), F32),
        jax.ShapeDtypeStruct((m, C_QK), F32),
    ]
    return pl.pallas_call(
        _inproj_odd_kernel,
        grid=(m // tm,),
        in_specs=[
            row(d),
            pl.BlockSpec((None, 1, d), lambda i: (layer, 0, 0)),
            pl.BlockSpec((None, d, wmain), lambda i: (li, 0, 0)),
            pl.BlockSpec((None, d, LANES), lambda i: (li, 0, 0)),
            pl.BlockSpec((None, LANES, C_QK), lambda i: (li, 0, 0)),
            pl.BlockSpec((None, 1, C_QK), lambda i: (li, 0, 0)),
        ],
        out_specs=[row(s.shape[1]) for s in out_shapes],
        out_shape=out_shapes,
        compiler_params=_cp("parallel"),
        name="inproj_odd",
    )(x, g, w, ws, w2, gbias)


def _gla_kernel(q_ref, k_ref, v_ref, r_ref, gk_ref, s0_ref, cn_ref, o_ref, sout_ref, s_scr, *, c, sub):
    ci = pl.program_id(1)

    @pl.when(ci == 0)
    def _():
        s_scr[...] = s0_ref[...]

    row = _iota((c, c), 0)
    col = _iota((c, c), 1)
    lower_bf = (row >= col).astype(BF16)
    in_sub = row & (sub - 1)
    sub_start = row - in_sub
    b_all = _mm_exact_lhs(lower_bf, gk_ref[...])

    for h in range(C_HEADS):
        sl = slice(C_DK * h, C_DK * (h + 1))
        vl = slice(C_DV * h, C_DV * (h + 1))
        q = q_ref[:, sl] * (C_DK ** -0.5)
        k = k_ref[:, sl]
        b = b_all[:, sl]
        v = v_ref[:, vl]
        s = s_scr[h]

        o = _mm((q * jnp.exp(b)).astype(BF16), s.astype(BF16))

        blocks = [jnp.zeros((sub, c), F32)]
        for r in range(1, c // sub):
            ref = b[sub * r:sub * r + 1, :]
            qt = q[sub * r:sub * (r + 1)] * jnp.exp(b[sub * r:sub * (r + 1)] - ref)
            kt = k * jnp.exp(jnp.minimum(ref - b, 0.0))
            blocks.append(_mm(qt.astype(BF16), kt.astype(BF16), NT))
        a = jnp.where(col < sub_start, jnp.concatenate(blocks, axis=0), 0.0)
        for d in range(sub):
            kd = k if d == 0 else pltpu.roll(k, d, axis=0)
            bd = b if d == 0 else pltpu.roll(b, d, axis=0)
            ad = jnp.sum(q * kd * jnp.exp(jnp.minimum(b - bd, 0.0)), axis=1, keepdims=True)
            a = jnp.where((col == row - d) & (in_sub >= d), ad, a)
        o = o + _mm(a.astype(BF16), v)

        blast = b[c - 1:c, :]
        kdec = (k * jnp.exp(blast - b)).astype(BF16)
        s_scr[h] = s * _row_to_col(jnp.exp(blast), C_DK) + _mm(kdec, v, TN)

        o_ref[:, vl] = (_rmsnorm(o, cn_ref[...]) * _silu(r_ref[:, vl])).astype(BF16)

    @pl.when(ci == pl.num_programs(1) - 1)
    def _():
        sout_ref[...] = s_scr[...]


def _gla(q, k, v, r, gk, s0, cn, li, c):
    bsz, t, _ = q.shape
    assert t % c == 0
    sub = min(GLA_SUB, c)
    tok = lambda n: pl.BlockSpec((None, c, n), lambda b, ci: (b, ci, 0))
    state = pl.BlockSpec((None, C_HEADS, C_DK, C_DV), lambda b, ci: (b, 0, 0, 0))
    return pl.pallas_call(
        functools.partial(_gla_kernel, c=c, sub=sub),
        grid=(bsz, t // c),
        in_specs=[tok(C_QK), tok(C_QK), tok(C_V), tok(C_V), tok(C_QK), state,
                  pl.BlockSpec((None, 1, C_DV), lambda b, ci: (li, 0, 0))],
        out_specs=[tok(C_V), state],
        out_shape=[jax.ShapeDtypeStruct((bsz, t, C_V), BF16),
                   jax.ShapeDtypeStruct((bsz, C_HEADS, C_DK, C_DV), F32)],
        scratch_shapes=[pltpu.VMEM((C_HEADS, C_DK, C_DV), F32)],
        compiler_params=_cp("parallel", "arbitrary"),
        name="gla",
    )(q, k, v, r, gk, s0, cn)


def _outproj_odd_kernel(x_ref, o_ref_in, w_ref, o_ref):
    o_ref[...] = x_ref[...] + _mm(o_ref_in[...], w_ref[...])


def _outproj_odd(x, o, w, li, tm):
    m, d = x.shape
    return pl.pallas_call(
        _outproj_odd_kernel,
        grid=(m // tm,),
        in_specs=[
            pl.BlockSpec((tm, d), lambda i: (i, 0)),
            pl.BlockSpec((tm, C_V), lambda i: (i, 0)),
            pl.BlockSpec((None, C_V, d), lambda i: (li, 0, 0)),
        ],
        out_specs=pl.BlockSpec((tm, d), lambda i: (i, 0)),
        out_shape=jax.ShapeDtypeStruct((m, d), F32),
        compiler_params=_cp("parallel"),
        name="outproj_odd",
    )(x, o, w)


def _pad_rows(a, rows):
    return jnp.pad(a, ((0, 0), (0, rows - a.shape[1]), (0, 0)))


def _pad_lanes(a):
    return jnp.pad(a, [(0, 0)] * (a.ndim - 1) + [(0, LANES - a.shape[-1])])


def kernel(x_prompt, x_sample, cache_k, cache_v, state_delta, state_conv, state_gla, page_table, rel_bias, norm_ffn1, norm_mix, norm_ffn2, ffn_in, ffn_out, w_in_even, a_qk_norm, a_lambda, a_subln, b_conv, b_a_log, b_dt_bias, b_norm, w_out_even, w_in_odd, c_gate_w2, c_gate_b, c_norm, w_out_odd):
    bp, tp, d = x_prompt.shape
    bs, ts, _ = x_sample.shape
    depth = norm_ffn1.shape[0]
    n_att = w_in_even.shape[0]
    n_pool, page = cache_k.shape[1], cache_k.shape[2]
    past = page_table.shape[1] * page
    tq = ATT_TILE
    cs = SUBLANES
    assert page == LANES and tp % tq == 0 and ts <= cs and ts >= CONV_W - 1
    assert _far_bucket_saturated(tq + 1) and _far_bucket_saturated(page + 1)

    ffn_in_bf = ffn_in.astype(BF16)
    ffn_out_bf = ffn_out.astype(BF16)
    w_even_bf = w_in_even.astype(BF16)
    w_even_small = _pad_lanes(w_in_even[:, :, IN_EVEN_MAIN:]).astype(BF16)
    w_out_even_bf = w_out_even.astype(BF16)
    w_odd_bf = w_in_odd.astype(BF16)
    w_odd_small = _pad_lanes(w_in_odd[:, :, IN_ODD_MAIN:]).astype(BF16)
    w_out_odd_bf = w_out_odd.astype(BF16)
    gate_w2 = jnp.pad(c_gate_w2, ((0, 0), (0, LANES - GLA_LOWRANK), (0, 0))).astype(BF16)
    gate_b = c_gate_b[:, None, :]
    n1 = norm_ffn1[:, None, :]
    nm = norm_mix[:, None, :]
    n2 = norm_ffn2[:, None, :]
    grp = jnp.kron(jnp.eye(2 * A_HEADS, dtype=F32), jnp.ones((A_DK, A_DK), F32)).astype(BF16)
    qgain = jnp.tile(a_qk_norm[:, 0:1, :], (1, 1, 2 * A_HEADS))
    kgain = jnp.tile(a_qk_norm[:, 1:2, :], (1, 1, 2 * A_HEADS))
    gparams = jnp.stack([_pad_lanes(-jnp.exp(b_a_log)), _pad_lanes(b_dt_bias)], axis=1)
    subln = a_subln[:, None, :]
    bnorm = b_norm[:, None, :]
    cnorm = c_norm[:, None, :]
    cache_k2 = cache_k.reshape(n_att, n_pool, page, A_QK)
    cache_v2 = cache_v.reshape(n_att, n_pool, page, A_V)

    bias_p, bias_s = _bias_tiles(rel_bias, tq, past, ts)
    bias_s = bias_s.reshape(A_HEADS * 2 * ts, past + LANES)

    xp = x_prompt.reshape(bp * tp, d)
    xs = x_sample.reshape(bs * ts, d)
    tm_p = ROWS_DENSE
    tm_s = bs * ts
    k_rows = [[], []]
    v_rows = [[], []]
    deltas = [[], []]
    convs = [[], []]
    glas = [[], []]

    for i in range(depth):
        li = i // 2
        xp = _ffn(xp, n1, ffn_in_bf, ffn_out_bf, i, 0, tm_p)
        xs = _ffn(xs, n1, ffn_in_bf, ffn_out_bf, i, 0, tm_s)
        if i % 2 == 0:
            lam_init = 0.8 - 0.6 * math.exp(-0.3 * i)
            lq1, lk1, lq2, lk2 = a_lambda[li]
            lam = jnp.exp(jnp.sum(lq1 * lk1)) - jnp.exp(jnp.sum(lq2 * lk2)) + lam_init
            sc = jnp.stack([lam, jnp.asarray(1.0 - lam_init, F32)]).astype(F32)

            q, k32, kbf, v32, vaug, bqkv, bz, gb = _inproj_even(
                xp, nm, w_even_bf, w_even_small, grp, qgain, kgain, gparams, i, li, ROWS_INPROJ)
            oa = _attn_prompt(sc, q.reshape(bp, tp, A_QK), kbf.reshape(bp, tp, A_QK),
                              vaug.reshape(bp, tp, 2 * A_V), bias_p, subln, li, tq)
            bqkv3 = bqkv.reshape(bp, tp, B_CONV_DIM)
            ob, s_new = _delta(bqkv3, bz.reshape(bp, tp, B_V), gb.reshape(bp, tp, LANES),
                               jnp.zeros((bp, SUBLANES, B_CONV_DIM), F32),
                               jnp.zeros((bp, B_HEADS, B_DK, B_DV), F32), b_conv, bnorm, li, SEQ_CHUNK)
            xp = _outproj_even(xp, oa.reshape(bp * tp, A_V), ob.reshape(bp * tp, B_V), w_out_even_bf, li, tm_p)
            k_rows[0].append(k32.reshape(bp, tp, A_HEADS, 2 * A_DK))
            v_rows[0].append(v32.reshape(bp, tp, A_HEADS, A_DV))
            deltas[0].append(s_new)
            convs[0].append(bqkv3[:, tp - (CONV_W - 1):, :])

            q, k32, kbf, v32, vaug, bqkv, bz, gb = _inproj_even(
                xs, nm, w_even_bf, w_even_small, grp, qgain, kgain, gparams, i, li, tm_s)
            q4 = q.reshape(bs, ts, A_HEADS, 2, A_DK)
            eye_h = jnp.eye(A_HEADS, dtype=BF16)
            eye_c = jnp.eye(2, dtype=BF16)
            qbd = jnp.einsum('bthcd,hg,ce->bhctged', q4, eye_h, eye_c).reshape(bs, A_HEADS * 2 * ts, A_QK)
            knew = _pad_rows(kbf.reshape(bs, ts, A_QK), LANES)
            vnew = _pad_rows(v32.reshape(bs, ts, A_V).astype(BF16), LANES)
            oa = _attn_sample(page_table, sc, qbd, bias_s, knew, vnew, subln, cache_k2, cache_v2, li, ts)
            oa = oa[:, :ts, :].astype(BF16)
            bqkv3 = bqkv.reshape(bs, ts, B_CONV_DIM)
            conv0 = jnp.pad(state_conv[li], ((0, 0), (SUBLANES - (CONV_W - 1), 0), (0, 0)))
            ob, s_new = _delta(_pad_rows(bqkv3, cs), _pad_rows(bz.reshape(bs, ts, B_V), cs),
                               _pad_rows(gb.reshape(bs, ts, LANES), cs), conv0, state_delta[li],
                               b_conv, bnorm, li, cs)
            xs = _outproj_even(xs, oa.reshape(bs * ts, A_V), ob[:, :ts, :].reshape(bs * ts, B_V),
                               w_out_even_bf, li, tm_s)
            k_rows[1].append(k32.reshape(bs, ts, A_HEADS, 2 * A_DK))
            v_rows[1].append(v32.reshape(bs, ts, A_HEADS, A_DV))
            deltas[1].append(s_new)
            convs[1].append(jnp.concatenate([state_conv[li], bqkv3], axis=1)[:, ts:, :])
        else:
            q, k, v, r, gk = _inproj_odd(xp, nm, w_odd_bf, w_odd_small, gate_w2, gate_b, i, li, ROWS_INPROJ)
            o, s_new = _gla(q.reshape(bp, tp, C_QK), k.reshape(bp, tp, C_QK), v.reshape(bp, tp, C_V),
                            r.reshape(bp, tp, C_V), gk.reshape(bp, tp, C_QK),
                            jnp.zeros((bp, C_HEADS, C_DK, C_DV), F32), cnorm, li, SEQ_CHUNK)
            xp = _outproj_odd(xp, o.reshape(bp * tp, C_V), w_out_odd_bf, li, tm_p)
            glas[0].append(s_new)

            q, k, v, r, gk = _inproj_odd(xs, nm, w_odd_bf, w_odd_small, gate_w2, gate_b, i, li, tm_s)
            pad = lambda a, n: _pad_rows(a.reshape(bs, ts, n), cs)
            o, s_new = _gla(pad(q, C_QK), pad(k, C_QK), pad(v, C_V), pad(r, C_V), pad(gk, C_QK),
                            state_gla[li], cnorm, li, cs)
            xs = _outproj_odd(xs, o[:, :ts, :].reshape(bs * ts, C_V), w_out_odd_bf, li, tm_s)
            glas[1].append(s_new)
        xp = _ffn(xp, n2, ffn_in_bf, ffn_out_bf, i, 1, tm_p)
        xs = _ffn(xs, n2, ffn_in_bf, ffn_out_bf, i, 1, tm_s)

    return (xp.reshape(bp, tp, d), xs.reshape(bs, ts, d),
            jnp.stack(k_rows[0]), jnp.stack(v_rows[0]), jnp.stack(deltas[0]), jnp.stack(convs[0]),
            jnp.stack(glas[0]),
            jnp.stack(k_rows[1]), jnp.stack(v_rows[1]), jnp.stack(deltas[1]), jnp.stack(convs[1]),
            jnp.stack(glas[1]))


IN_EVEN_MAIN = 2 * A_QK + A_V + B_CONV_DIM + B_V
IN_ODD_MAIN = 2 * C_QK + 2 * C_V
```

```python
import functools
import math

import jax
import jax.numpy as jnp
from jax import lax
from jax.experimental import pallas as pl
from jax.experimental.pallas import tpu as pltpu

F32 = jnp.float32
BF16 = jnp.bfloat16

A_HEADS, A_DK, A_DV = 4, 64, 128
B_HEADS, B_DK, B_DV = 4, 128, 128
CONV_W = 4
C_HEADS, C_DK, C_DV = 4, 128, 256
GLA_LOWRANK, GLA_TAU = 16, 16.0
N_BUCKETS, MAX_DIST = 32, 128
EPS = 1e-6
NEG = -1e30
LOG2E = 1.4426950408889634

A_QK = A_HEADS * 2 * A_DK
A_V = A_HEADS * A_DV
B_QK = B_HEADS * B_DK
B_V = B_HEADS * B_DV
B_CONV_DIM = 2 * B_QK + B_V
C_QK = C_HEADS * C_DK
C_V = C_HEADS * C_DV
IN_EVEN_MAIN = 2 * A_QK + A_V + B_CONV_DIM + B_V
IN_ODD_MAIN = 2 * C_QK + 2 * C_V

LANES = 128
SUBLANES = 8
PACKED_SUBLANES = 16
VMEM_LIMIT = 56 * 2 ** 20

ROWS_DENSE = 512
ROWS_INPROJ = 256
ATT_TILE = 512
VT_ROWS = A_DV + PACKED_SUBLANES
SEQ_CHUNK = 64
DELTA_PREP_CHUNKS = 4
GLA_SUB = 16
PAGES_PER_STEP = 8

NT = (((1,), (1,)), ((), ()))
TN = (((0,), (0,)), ((), ()))


def _cp(*sem):
    return pltpu.CompilerParams(dimension_semantics=sem, vmem_limit_bytes=VMEM_LIMIT)


def _mm(a, b, dn=None):
    if dn is None:
        return jnp.dot(a, b, preferred_element_type=F32)
    return lax.dot_general(a, b, dn, preferred_element_type=F32)


def _split2(x):
    hi = x.astype(BF16)
    lo = (x - hi.astype(F32)).astype(BF16)
    return hi, lo


def _split3(x):
    hi = x.astype(BF16)
    r = x - hi.astype(F32)
    mid = r.astype(BF16)
    lo = (r - mid.astype(F32)).astype(BF16)
    return hi, mid, lo


def _mm3(a, b):
    ah, al = _split2(a)
    bh, bl = _split2(b)
    return _mm(ah, bh) + (_mm(ah, bl) + _mm(al, bh))


def _mm_exact_lhs(a_bf, x):
    hi, mid, lo = _split3(x)
    return _mm(a_bf, hi) + (_mm(a_bf, mid) + _mm(a_bf, lo))


def _sigmoid(x):
    return 1.0 / (1.0 + jnp.exp(-x))


def _silu(x):
    return x * _sigmoid(x)


def _softplus(x):
    return jnp.maximum(x, 0.0) + jnp.log(1.0 + jnp.exp(-jnp.abs(x)))


def _rmsnorm(xf, g):
    return xf * lax.rsqrt(jnp.mean(xf * xf, axis=-1, keepdims=True) + EPS) * g


def _iota(shape, dim):
    return lax.broadcasted_iota(jnp.int32, shape, dim)


def _row_to_col(row, n):
    eye = _iota((n, n), 0) == _iota((n, n), 1)
    return jnp.sum(jnp.where(eye, jnp.broadcast_to(row, (n, n)), 0.0), axis=1, keepdims=True)


def _ffn_kernel(x_ref, g_ref, wg_ref, wu_ref, wo_ref, o_ref, h_ref):
    j = pl.program_id(1)

    @pl.when(j == 0)
    def _():
        xf = x_ref[...]
        h_ref[...] = _rmsnorm(xf, g_ref[...]).astype(BF16)
        o_ref[...] = xf

    h = h_ref[...]
    gate = _mm(h, wg_ref[...])
    up = _mm(h, wu_ref[...])
    a = (_silu(gate) * up).astype(BF16)
    o_ref[...] += 0.5 * _mm(a, wo_ref[...])


def _ffn(x, g, w_in, w_out, layer, which, tm):
    m, d = x.shape
    f = w_out.shape[2]
    nf = 2
    tf = f // nf
    assert f % nf == 0 and tf % LANES == 0 and m % tm == 0
    return pl.pallas_call(
        _ffn_kernel,
        grid=(m // tm, nf),
        in_specs=[
            pl.BlockSpec((tm, d), lambda i, j: (i, 0)),
            pl.BlockSpec((None, 1, d), lambda i, j: (layer, 0, 0)),
            pl.BlockSpec((None, None, d, tf), lambda i, j: (layer, which, 0, j)),
            pl.BlockSpec((None, None, d, tf), lambda i, j: (layer, which, 0, j + nf)),
            pl.BlockSpec((None, None, tf, d), lambda i, j: (layer, which, j, 0)),
        ],
        out_specs=pl.BlockSpec((tm, d), lambda i, j: (i, 0)),
        out_shape=jax.ShapeDtypeStruct((m, d), F32),
        scratch_shapes=[pltpu.VMEM((tm, d), BF16)],
        compiler_params=_cp("parallel", "arbitrary"),
        name="ffn",
    )(x, g, w_in, w_in, w_out)


def _group_rms(x, grp_bf, gain, width):
    hi, lo = _split2(x * x)
    ms = (_mm(hi, grp_bf) + _mm(lo, grp_bf)) * (1.0 / width)
    return x * lax.rsqrt(ms + EPS) * gain


def _inproj_even_kernel(x_ref, g_ref, w_ref, ws_ref, grp_ref, qg_ref, kg_ref, gp_ref,
                        q_ref, k32_ref, kbf_ref, v32_ref, vt_ref, bqkv_ref, bz_ref, gb_ref):
    h = _rmsnorm(x_ref[...], g_ref[...]).astype(BF16)
    grp = grp_ref[...]

    aq = _mm(h, w_ref[:, 0:A_QK])
    q = _group_rms(aq, grp, qg_ref[...], A_DK)
    q_ref[...] = (q * (A_DK ** -0.5 * LOG2E)).astype(BF16)

    ak = _mm(h, w_ref[:, A_QK:2 * A_QK])
    k = _group_rms(ak, grp, kg_ref[...], A_DK)
    k32_ref[...] = k
    kbf_ref[...] = k.astype(BF16)

    v = _mm(h, w_ref[:, 2 * A_QK:2 * A_QK + A_V])
    v32_ref[...] = v
    ones = jnp.ones((VT_ROWS - A_DV, v.shape[0]), BF16)
    for hd in range(A_HEADS):
        vt_ref[hd, 0:A_DV, :] = v[:, A_DV * hd:A_DV * (hd + 1)].T.astype(BF16)
        vt_ref[hd, A_DV:VT_ROWS, :] = ones

    o = 2 * A_QK + A_V
    bqkv_ref[...] = _mm(h, w_ref[:, o:o + B_CONV_DIM])
    o += B_CONV_DIM
    bz_ref[...] = _mm(h, w_ref[:, o:o + B_V])

    y = _mm(h, ws_ref[...])
    lane = _iota(y.shape, 1)
    g = gp_ref[0:1, :] * _softplus(y + gp_ref[1:2, :])
    gb_ref[...] = jnp.where(lane < B_HEADS, g, jnp.where(lane < 2 * B_HEADS, _sigmoid(y), 0.0))


def _inproj_even(x, g, w, ws, grp, qg, kg, gp, layer, li, tm):
    m, d = x.shape
    wmain = IN_EVEN_MAIN
    row = lambda n: pl.BlockSpec((tm, n), lambda i: (i, 0))
    out_shapes = [
        jax.ShapeDtypeStruct((m, A_QK), BF16),
        jax.ShapeDtypeStruct((m, A_QK), F32),
        jax.ShapeDtypeStruct((m, A_QK), BF16),
        jax.ShapeDtypeStruct((m, A_V), F32),
        jax.ShapeDtypeStruct((A_HEADS, m // tm, VT_ROWS, tm), BF16),
        jax.ShapeDtypeStruct((m, B_CONV_DIM), F32),
        jax.ShapeDtypeStruct((m, B_V), F32),
        jax.ShapeDtypeStruct((m, LANES), F32),
    ]
    out_specs = [row(s.shape[1]) for s in out_shapes]
    out_specs[4] = pl.BlockSpec((A_HEADS, None, VT_ROWS, tm), lambda i: (0, i, 0, 0))
    return pl.pallas_call(
        _inproj_even_kernel,
        grid=(m // tm,),
        in_specs=[
            row(d),
            pl.BlockSpec((None, 1, d), lambda i: (layer, 0, 0)),
            pl.BlockSpec((None, d, wmain), lambda i: (li, 0, 0)),
            pl.BlockSpec((None, d, LANES), lambda i: (li, 0, 0)),
            pl.BlockSpec((A_QK, A_QK), lambda i: (0, 0)),
            pl.BlockSpec((None, 1, A_QK), lambda i: (li, 0, 0)),
            pl.BlockSpec((None, 1, A_QK), lambda i: (li, 0, 0)),
            pl.BlockSpec((None, 2, LANES), lambda i: (li, 0, 0)),
        ],
        out_specs=out_specs,
        out_shape=out_shapes,
        compiler_params=_cp("parallel"),
        name="inproj_even",
    )(x, g, w, ws, grp, qg, kg, gp)


def _bias_kernel(rb_ref, bp_ref, bs_ref, *, tq, past, tnew):
    h = pl.program_id(0)
    far = rb_ref[N_BUCKETS - 1, h]
    exact = N_BUCKETS // 2

    def table(rel):
        n = jnp.maximum(rel, 0)
        nf = jnp.maximum(n, 1).astype(F32)
        large = exact + (jnp.log(nf / exact) / math.log(MAX_DIST / exact) * (N_BUCKETS - exact)).astype(jnp.int32)
        bucket = jnp.where(n < exact, n, jnp.minimum(large, N_BUCKETS - 1))
        val = lax.fori_loop(0, N_BUCKETS, lambda b, acc: jnp.where(bucket == b, rb_ref[b, h], acc),
                            jnp.zeros(rel.shape, F32))
        return jnp.where(rel >= 0, (val - far) * LOG2E, NEG)

    key_p = _iota((tq, tq), 0)
    qry_p = _iota((tq, tq), 1)
    for d in range(2):
        bp_ref[d] = table(d * tq + qry_p - key_p)

    ncol = past * A_HEADS + LANES
    t_idx = _iota((2 * tnew, ncol), 0) & (tnew - 1)
    col = _iota((2 * tnew, ncol), 1)
    key = col >> 2
    rel_new = jnp.where(key - past < tnew, t_idx - (key - past), -1)
    rel = jnp.where(key < past, past + t_idx - key, rel_new)
    bs_ref[...] = table(jnp.where((col & (A_HEADS - 1)) == h, rel, -1))


def _bias_tiles(rel_bias, tq, past, tnew):
    assert tnew & (tnew - 1) == 0 and 2 * tnew == SUBLANES and A_HEADS == 4
    ncol = past * A_HEADS + LANES
    return pl.pallas_call(
        functools.partial(_bias_kernel, tq=tq, past=past, tnew=tnew),
        grid=(A_HEADS,),
        in_specs=[pl.BlockSpec(memory_space=pltpu.SMEM)],
        out_specs=[
            pl.BlockSpec((None, 2, tq, tq), lambda h: (h, 0, 0, 0)),
            pl.BlockSpec((None, 2 * tnew, ncol), lambda h: (h, 0, 0)),
        ],
        out_shape=[
            jax.ShapeDtypeStruct((A_HEADS, 2, tq, tq), F32),
            jax.ShapeDtypeStruct((A_HEADS, 2 * tnew, ncol), F32),
        ],
        compiler_params=_cp("parallel"),
        name="t5_bias",
    )(rel_bias)


def _far_bucket_saturated(dist):
    exact = N_BUCKETS // 2
    v = math.log(dist / exact) / math.log(MAX_DIST / exact) * (N_BUCKETS - exact)
    return dist >= exact and exact + v >= N_BUCKETS - 1 + 1e-3


def _attn_prompt_kernel(sc_ref, q_ref, k_ref, vt_ref, bias_ref, sg_ref, o_ref, *, tq, vblk):
    i = pl.program_id(2)
    q = q_ref[...]
    lane = _iota(q.shape, 1)
    zero = jnp.zeros_like(q)
    qbd = jnp.concatenate([jnp.where(lane < A_DK, q, zero), jnp.where(lane >= A_DK, q, zero)], axis=0)
    nsub = tq // vblk

    def scores(j):
        start = pl.multiple_of(j * tq, tq)
        return _mm(k_ref[pl.ds(start, tq), :], qbd, NT)

    def absorb(j, s, m, acc, bias):
        if bias is not None:
            s = s + jnp.concatenate([bias, bias], axis=1)
        m_new = jnp.maximum(m, jnp.max(s, axis=0, keepdims=True))
        alpha = jnp.exp2(m - m_new)
        p = jnp.exp2(s - m_new).astype(BF16)
        pv = _mm(vt_ref[j * nsub], p[0:vblk])
        for u in range(1, nsub):
            pv = pv + _mm(vt_ref[j * nsub + u], p[u * vblk:(u + 1) * vblk])
        return m_new, alpha * acc + pv

    def step(j, c, bias):
        return absorb(j, scores(j), c[0], c[1], bias)

    carry = (jnp.full((1, 2 * tq), NEG, F32), jnp.zeros((VT_ROWS, 2 * tq), F32))
    carry = lax.fori_loop(0, jnp.maximum(i - 1, 0), lambda j, c: step(j, c, None), carry)
    carry = lax.cond(i >= 1, lambda c: step(i - 1, c, bias_ref[1]), lambda c: c, carry)
    _, acc = step(i, carry, bias_ref[0])

    o = acc[0:A_DV] / acc[A_DV:A_DV + 1]
    o = o[:, 0:tq] - sc_ref[0] * o[:, tq:2 * tq]
    o = o * lax.rsqrt(jnp.mean(o * o, axis=0, keepdims=True) + EPS)
    o_ref[...] = (o.T * (sg_ref[...] * sc_ref[1])).astype(BF16)


def _attn_prompt(sc, q, k, vt, bias_p, sg, li, tq):
    bsz, t, _ = q.shape
    vblk = vt.shape[3]
    assert tq % vblk == 0 and t % tq == 0
    return pl.pallas_call(
        functools.partial(_attn_prompt_kernel, tq=tq, vblk=vblk),
        grid=(bsz, A_HEADS, t // tq),
        in_specs=[
            pl.BlockSpec(memory_space=pltpu.SMEM),
            pl.BlockSpec((None, tq, 2 * A_DK), lambda b, h, i: (b, i, h)),
            pl.BlockSpec((None, t, 2 * A_DK), lambda b, h, i: (b, 0, h)),
            pl.BlockSpec((None, t // vblk, VT_ROWS, vblk), lambda b, h, i: (h, b, 0, 0)),
            pl.BlockSpec((None, 2, tq, tq), lambda b, h, i: (h, 0, 0, 0)),
            pl.BlockSpec((None, 1, A_DV), lambda b, h, i: (li, 0, 0)),
        ],
        out_specs=pl.BlockSpec((None, tq, A_DV), lambda b, h, i: (b, i, h)),
        out_shape=jax.ShapeDtypeStruct((bsz, t, A_V), BF16),
        compiler_params=_cp("parallel", "parallel", "arbitrary"),
        name="attn_prompt",
    )(sc, q, k, vt, bias_p, sg)


def _attn_sample_kernel(pt_ref, sc_ref, qbd_ref, bias_ref, biasn_ref, kn_ref, vn_ref, sg_ref, *rest, npg, tnew):
    k_refs = rest[:npg]
    v_refs = rest[npg:2 * npg]
    o_ref, m_ref, l_ref, acc_ref = rest[2 * npg:]
    pg = pl.program_id(1)
    qbd = qbd_ref[...]

    @pl.when(pg == 0)
    def _():
        m_ref[...] = jnp.full(m_ref.shape, NEG, F32)
        l_ref[...] = jnp.zeros(l_ref.shape, F32)
        acc_ref[...] = jnp.zeros(acc_ref.shape, F32)

    def update(s, vs):
        m_old = m_ref[...]
        m_new = jnp.maximum(m_old, jnp.max(s, axis=1, keepdims=True))
        alpha = jnp.exp2(m_old - m_new)
        p = jnp.exp2(s - m_new)
        l_ref[...] = alpha * l_ref[...] + jnp.sum(p, axis=1, keepdims=True)
        pb = p.astype(BF16)
        w = pb.shape[1] // len(vs)
        pv = _mm(pb[:, 0:w], vs[0])
        for g in range(1, len(vs)):
            pv = pv + _mm(pb[:, g * w:(g + 1) * w], vs[g])
        acc_ref[...] = alpha * acc_ref[...] + pv
        m_ref[...] = m_new

    s = jnp.concatenate([_mm(qbd, kr[...].astype(BF16), NT) for kr in k_refs], axis=1) + bias_ref[...]
    update(s, [vr[...].astype(BF16) for vr in v_refs])

    @pl.when(pg == pl.num_programs(1) - 1)
    def _():
        update(_mm(qbd, kn_ref[...], NT) + biasn_ref[...], [vn_ref[...]])
        lam = sc_ref[0]
        rows = 2 * tnew
        for h in range(A_HEADS):
            a = acc_ref[rows * h:rows * (h + 1), :] / l_ref[rows * h:rows * (h + 1), :]
            o = a - lam * pltpu.roll(a, tnew, axis=0)
            o_ref[:, A_DV * h:A_DV * (h + 1)] = _rmsnorm(o, sg_ref[...]) * sc_ref[1]


def _attn_sample(page_table, sc, qbd, bias_s, knew, vnew, sg, cache_k, cache_v, li, tnew):
    bsz, nrow, _ = qbd.shape
    n_pages = page_table.shape[1]
    npg = PAGES_PER_STEP
    assert n_pages % npg == 0
    prow = cache_k.shape[2]
    past_cols = n_pages * prow

    def kvspec(g):
        return pl.BlockSpec((None, None, prow, 2 * A_DK), lambda b, pg, pt: (li, pt[b, pg * npg + g], 0, 0))

    grid_spec = pltpu.PrefetchScalarGridSpec(
        num_scalar_prefetch=1,
        grid=(bsz, n_pages // npg),
        in_specs=[
            pl.BlockSpec(memory_space=pltpu.SMEM),
            pl.BlockSpec((None, nrow, 2 * A_DK), lambda b, pg, pt: (b, 0, 0)),
            pl.BlockSpec((nrow, npg * prow), lambda b, pg, pt: (0, pg)),
            pl.BlockSpec((nrow, LANES), lambda b, pg, pt: (0, past_cols // LANES)),
            pl.BlockSpec((None, LANES, 2 * A_DK), lambda b, pg, pt: (b, 0, 0)),
            pl.BlockSpec((None, LANES, A_DV), lambda b, pg, pt: (b, 0, 0)),
            pl.BlockSpec((None, 1, A_DV), lambda b, pg, pt: (li, 0, 0)),
        ] + [kvspec(g) for g in range(npg)] + [kvspec(g) for g in range(npg)],
        out_specs=pl.BlockSpec((None, 2 * tnew, A_V), lambda b, pg, pt: (b, 0, 0)),
        scratch_shapes=[pltpu.VMEM((nrow, 1), F32), pltpu.VMEM((nrow, 1), F32), pltpu.VMEM((nrow, A_DV), F32)],
    )
    return pl.pallas_call(
        functools.partial(_attn_sample_kernel, npg=npg, tnew=tnew),
        grid_spec=grid_spec,
        out_shape=jax.ShapeDtypeStruct((bsz, 2 * tnew, A_V), F32),
        compiler_params=_cp("parallel", "arbitrary"),
        name="attn_sample",
    )(page_table, sc, qbd, bias_s, bias_s, knew, vnew, sg, *([cache_k] * npg), *([cache_v] * npg))


def _l2norm(x):
    return x * lax.rsqrt(jnp.sum(x * x, axis=-1, keepdims=True) + EPS)


def _unit_lower_inverses(mats, c):
    n = mats[0].shape[0]
    eye = (_iota((n, n), 0) == _iota((n, n), 1)).astype(F32)
    ps = [eye - a for a in mats]
    qs = [_split2(a) for a in mats]
    span = 2
    while span < c:
        qs = [_split2(_mm(qh, qh) + (_mm(qh, ql) + _mm(ql, qh))) for qh, ql in qs]
        pss = [_split2(p) for p in ps]
        ps = [p + (_mm(ph, qh) + (_mm(ph, ql) + _mm(pl_, qh))) for p, (ph, pl_), (qh, ql) in zip(ps, pss, qs)]
        span *= 2
    return ps


def _cumsum_rows(x, n):
    row = _iota(x.shape, 0)
    k = 1
    while k < n:
        x = x + jnp.where(row >= k, pltpu.roll(x, k, axis=0), 0.0)
        k *= 2
    return x


def _delta_prep_kernel(x_ref, halo_ref, c0_ref, gb_ref, cw_ref, u_ref, wk_ref, qe_ref, kd_ref, qk_ref, xbuf,
                       *, c, nck):
    ci = pl.program_id(1)
    ch = B_HEADS * c
    rows = nck * c
    xbuf[0:SUBLANES, :] = jnp.where(ci == 0, c0_ref[...], halo_ref[...])
    xbuf[SUBLANES:SUBLANES + rows, :] = x_ref[...]
    first = SUBLANES - (CONV_W - 1)
    conv = xbuf[pl.ds(first, rows), :] * cw_ref[0:1, :]
    for j in range(1, CONV_W):
        conv = conv + xbuf[pl.ds(first + j, rows), :] * cw_ref[j:j + 1, :]
    conv = _silu(conv)

    row = _iota((ch, ch), 0)
    col = _iota((ch, ch), 1)
    shift = c.bit_length() - 1
    same = (row >> shift) == (col >> shift)
    incl = same & (row >= col)
    strict = same & (row > col)

    def stack_lane(a, off):
        return jnp.concatenate([a[:, off + h:off + h + 1] for h in range(B_HEADS)], axis=0)

    parts = []
    for n in range(nck):
        rs = slice(n * c, (n + 1) * c)

        def stack(off, width):
            return jnp.concatenate([conv[rs, off + width * h:off + width * (h + 1)] for h in range(B_HEADS)], axis=0)

        q = _l2norm(stack(0, B_DK)) * (B_DK ** -0.5)
        k = _l2norm(stack(B_QK, B_DK))
        v = stack(2 * B_QK, B_DV)
        gb = gb_ref[rs, :]
        gcum = _cumsum_rows(gb, c)
        gcol = stack_lane(gcum, 0)
        beta = stack_lane(gb, B_HEADS)
        glast = stack_lane(jnp.broadcast_to(gcum[c - 1:c, :], gcum.shape), 0)
        grow = jnp.broadcast_to(gcol, (ch, LANES)).T[0:1, :]
        dec = jnp.where(incl, jnp.exp(jnp.where(incl, gcol - grow, 0.0)), 0.0)
        kb = k.astype(BF16)
        a = jnp.where(strict, _mm(kb, kb, NT) * dec * beta, 0.0)
        egc = jnp.exp(gcol)
        qe_ref[n] = (q * egc).astype(BF16)
        kd_ref[n] = (k * jnp.exp(glast - gcol)).astype(BF16)
        qk_ref[n] = (_mm(q.astype(BF16), kb, NT) * dec).astype(BF16)
        parts.append((a, jnp.concatenate([v * beta, k * (beta * egc)], axis=1)))

    t_invs = _unit_lower_inverses([a for a, _ in parts], c)
    for n in range(nck):
        uw = _mm3(t_invs[n], parts[n][1])
        u_ref[n] = uw[:, 0:B_DV]
        wk_ref[n] = uw[:, B_DV:B_DV + B_DK].astype(BF16)


def _delta_scan_kernel(u_ref, wk_ref, qe_ref, kd_ref, qk_ref, gb_ref, z_ref, s0_ref, bn_ref, ob_ref, sout_ref,
                       s_scr, *, c, nb):
    ci = pl.program_id(1)

    @pl.when(ci == 0)
    def _():
        s_scr[...] = s0_ref[...]

    hs = [slice(c * h, c * (h + 1)) for h in range(B_HEADS)]
    for b in range(nb):
        decay = jnp.exp(jnp.sum(gb_ref[b], axis=0, keepdims=True))
        s = [s_scr[b, h] for h in range(B_HEADS)]
        s_bf = [x.astype(BF16) for x in s]
        v_new = u_ref[b] - jnp.concatenate([_mm(wk_ref[b, hs[h], :], s_bf[h]) for h in range(B_HEADS)], axis=0)
        vn_bf = v_new.astype(BF16)
        o = (jnp.concatenate([_mm(qe_ref[b, hs[h], :], s_bf[h]) for h in range(B_HEADS)], axis=0)
             + _mm(qk_ref[b], vn_bf))
        for h in range(B_HEADS):
            s_scr[b, h] = s[h] * decay[:, h:h + 1] + _mm(kd_ref[b, hs[h], :], vn_bf[hs[h]], TN)
            sl = slice(B_DV * h, B_DV * (h + 1))
            ob_ref[b, :, sl] = (_rmsnorm(o[hs[h]], bn_ref[...]) * _silu(z_ref[b, :, sl])).astype(BF16)

    @pl.when(ci == pl.num_programs(1) - 1)
    def _():
        sout_ref[...] = s_scr[...]


def _delta(bqkv, bz, gb, conv0, s0, cw, bn, li, c):
    bsz, t, _ = bqkv.shape
    assert t % c == 0 and c % SUBLANES == 0 and c & (c - 1) == 0
    nch = t // c
    ch = B_HEADS * c
    nck = min(DELTA_PREP_CHUNKS, nch)
    assert nch % nck == 0
    hb = nck * c // SUBLANES
    tok = lambda n: pl.BlockSpec((None, c, n), lambda b, ci: (b, ci, 0))
    step_tok = lambda n: pl.BlockSpec((None, nck * c, n), lambda b, ci: (b, ci, 0))
    step_chunks = lambda n: pl.BlockSpec((None, nck, ch, n), lambda b, ci: (b, ci, 0, 0))
    chunk_shape = lambda n, dt: jax.ShapeDtypeStruct((bsz, nch, ch, n), dt)
    u, wk, qe, kd, qk = pl.pallas_call(
        functools.partial(_delta_prep_kernel, c=c, nck=nck),
        grid=(bsz, nch // nck),
        in_specs=[
            step_tok(B_CONV_DIM),
            pl.BlockSpec((None, SUBLANES, B_CONV_DIM), lambda b, ci: (b, jnp.maximum(ci * hb - 1, 0), 0)),
            pl.BlockSpec((None, SUBLANES, B_CONV_DIM), lambda b, ci: (b, 0, 0)),
            step_tok(LANES),
            pl.BlockSpec((None, CONV_W, B_CONV_DIM), lambda b, ci: (li, 0, 0)),
        ],
        out_specs=[step_chunks(B_DV), step_chunks(B_DK), step_chunks(B_DK), step_chunks(B_DK), step_chunks(ch)],
        out_shape=[chunk_shape(B_DV, F32), chunk_shape(B_DK, BF16), chunk_shape(B_DK, BF16),
                   chunk_shape(B_DK, BF16), chunk_shape(ch, BF16)],
        scratch_shapes=[pltpu.VMEM((SUBLANES + nck * c, B_CONV_DIM), F32)],
        compiler_params=_cp("parallel", "parallel"),
        name="delta_prep",
    )(bqkv, bqkv, conv0, gb, cw)

    nb = 2
    assert bsz % nb == 0
    grp_chunk = lambda n: pl.BlockSpec((nb, None, ch, n), lambda b, ci: (b, ci, 0, 0))
    grp_tok = lambda n: pl.BlockSpec((nb, c, n), lambda b, ci: (b, ci, 0))
    state = pl.BlockSpec((nb, B_HEADS, B_DK, B_DV), lambda b, ci: (b, 0, 0, 0))
    return pl.pallas_call(
        functools.partial(_delta_scan_kernel, c=c, nb=nb),
        grid=(bsz // nb, nch),
        in_specs=[grp_chunk(B_DV), grp_chunk(B_DK), grp_chunk(B_DK), grp_chunk(B_DK), grp_chunk(ch),
                  grp_tok(LANES), grp_tok(B_V), state,
                  pl.BlockSpec((None, 1, B_DV), lambda b, ci: (li, 0, 0))],
        out_specs=[grp_tok(B_V), state],
        out_shape=[jax.ShapeDtypeStruct((bsz, t, B_V), BF16),
                   jax.ShapeDtypeStruct((bsz, B_HEADS, B_DK, B_DV), F32)],
        scratch_shapes=[pltpu.VMEM((nb, B_HEADS, B_DK, B_DV), F32)],
        compiler_params=_cp("parallel", "arbitrary"),
        name="delta_scan",
    )(u, wk, qe, kd, qk, gb, bz, s0, bn)


def _outproj_even_kernel(x_ref, oa_ref, ob_ref, w_ref, o_ref):
    o_ref[...] = x_ref[...] + (_mm(oa_ref[...], w_ref[0:A_V, :]) + _mm(ob_ref[...], w_ref[A_V:A_V + B_V, :]))


def _outproj_even(x, oa, ob, w, li, tm):
    m, d = x.shape
    return pl.pallas_call(
        _outproj_even_kernel,
        grid=(m // tm,),
        in_specs=[
            pl.BlockSpec((tm, d), lambda i: (i, 0)),
            pl.BlockSpec((tm, A_V), lambda i: (i, 0)),
            pl.BlockSpec((tm, B_V), lambda i: (i, 0)),
            pl.BlockSpec((None, A_V + B_V, d), lambda i: (li, 0, 0)),
        ],
        out_specs=pl.BlockSpec((tm, d), lambda i: (i, 0)),
        out_shape=jax.ShapeDtypeStruct((m, d), F32),
        compiler_params=_cp("parallel"),
        name="outproj_even",
    )(x, oa, ob, w)


def _inproj_odd_kernel(x_ref, g_ref, w_ref, ws_ref, w2_ref, gbias_ref, q_ref, k_ref, v_ref, r_ref, gk_ref):
    h = _rmsnorm(x_ref[...], g_ref[...]).astype(BF16)
    q_ref[...] = _mm(h, w_ref[:, 0:C_QK])
    k_ref[...] = _mm(h, w_ref[:, C_QK:2 * C_QK])
    v_ref[...] = _mm(h, w_ref[:, 2 * C_QK:2 * C_QK + C_V]).astype(BF16)
    r_ref[...] = _mm(h, w_ref[:, 2 * C_QK + C_V:2 * C_QK + 2 * C_V])
    glr = _mm(h, ws_ref[...]).astype(BF16)
    y = _mm(glr, w2_ref[...]) + gbias_ref[...]
    gk_ref[...] = -_softplus(-y) * (1.0 / GLA_TAU)


def _inproj_odd(x, g, w, ws, w2, gbias, layer, li, tm):
    m, d = x.shape
    wmain = 2 * C_QK + 2 * C_V
    row = lambda n: pl.BlockSpec((tm, n), lambda i: (i, 0))
    out_shapes = [
        jax.ShapeDtypeStruct((m, C_QK), F32),
        jax.ShapeDtypeStruct((m, C_QK), F32),
        jax.ShapeDtypeStruct((m, C_V), BF16),
        jax.ShapeDtypeStruct((m, C_V), F32),
        jax.ShapeDtypeStruct((m, C_QK), F32),
    ]
    return pl.pallas_call(
        _inproj_odd_kernel,
        grid=(m // tm,),
        in_specs=[
            row(d),
            pl.BlockSpec((None, 1, d), lambda i: (layer, 0, 0)),
            pl.BlockSpec((None, d, wmain), lambda i: (li, 0, 0)),
            pl.BlockSpec((None, d, LANES), lambda i: (li, 0, 0)),
            pl.BlockSpec((None, LANES, C_QK), lambda i: (li, 0, 0)),
            pl.BlockSpec((None, 1, C_QK), lambda i: (li, 0, 0)),
        ],
        out_specs=[row(s.shape[1]) for s in out_shapes],
        out_shape=out_shapes,
        compiler_params=_cp("parallel"),
        name="inproj_odd",
    )(x, g, w, ws, w2, gbias)


def _gla_kernel(q_ref, k_ref, v_ref, r_ref, gk_ref, s0_ref, cn_ref, o_ref, sout_ref, s_scr, *, c, sub):
    ci = pl.program_id(1)

    @pl.when(ci == 0)
    def _():
        s_scr[...] = s0_ref[...]

    row = _iota((c, c), 0)
    col = _iota((c, c), 1)
    lower_bf = (row >= col).astype(BF16)
    in_sub = row & (sub - 1)
    sub_start = row - in_sub
    b_all = _mm_exact_lhs(lower_bf, gk_ref[...])

    for h in range(C_HEADS):
        sl = slice(C_DK * h, C_DK * (h + 1))
        vl = slice(C_DV * h, C_DV * (h + 1))
        q = q_ref[:, sl] * (C_DK ** -0.5)
        k = k_ref[:, sl]
        b = b_all[:, sl]
        v = v_ref[:, vl]
        s = s_scr[h]

        o = _mm((q * jnp.exp(b)).astype(BF16), s.astype(BF16))

        blocks = [jnp.zeros((sub, c), F32)]
        for r in range(1, c // sub):
            ref = b[sub * r:sub * r + 1, :]
            qt = q[sub * r:sub * (r + 1)] * jnp.exp(b[sub * r:sub * (r + 1)] - ref)
            kt = k * jnp.exp(jnp.minimum(ref - b, 0.0))
            blocks.append(_mm(qt.astype(BF16), kt.astype(BF16), NT))
        a = jnp.where(col < sub_start, jnp.concatenate(blocks, axis=0), 0.0)
        for d in range(sub):
            kd = k if d == 0 else pltpu.roll(k, d, axis=0)
            bd = b if d == 0 else pltpu.roll(b, d, axis=0)
            ad = jnp.sum(q * kd * jnp.exp(jnp.minimum(b - bd, 0.0)), axis=1, keepdims=True)
            a = jnp.where((col == row - d) & (in_sub >= d), ad, a)
        o = o + _mm(a.astype(BF16), v)

        blast = b[c - 1:c, :]
        kdec = (k * jnp.exp(blast - b)).astype(BF16)
        s_scr[h] = s * _row_to_col(jnp.exp(blast), C_DK) + _mm(kdec, v, TN)

        o_ref[:, vl] = (_rmsnorm(o, cn_ref[...]) * _silu(r_ref[:, vl])).astype(BF16)

    @pl.when(ci == pl.num_programs(1) - 1)
    def _():
        sout_ref[...] = s_scr[...]


def _gla(q, k, v, r, gk, s0, cn, li, c):
    bsz, t, _ = q.shape
    assert t % c == 0
    sub = min(GLA_SUB, c)
    tok = lambda n: pl.BlockSpec((None, c, n), lambda b, ci: (b, ci, 0))
    state = pl.BlockSpec((None, C_HEADS, C_DK, C_DV), lambda b, ci: (b, 0, 0, 0))
    return pl.pallas_call(
        functools.partial(_gla_kernel, c=c, sub=sub),
        grid=(bsz, t // c),
        in_specs=[tok(C_QK), tok(C_QK), tok(C_V), tok(C_V), tok(C_QK), state,
                  pl.BlockSpec((None, 1, C_DV), lambda b, ci: (li, 0, 0))],
        out_specs=[tok(C_V), state],
        out_shape=[jax.ShapeDtypeStruct((bsz, t, C_V), BF16),
                   jax.ShapeDtypeStruct((bsz, C_HEADS, C_DK, C_DV), F32)],
        scratch_shapes=[pltpu.VMEM((C_HEADS, C_DK, C_DV), F32)],
        compiler_params=_cp("parallel", "arbitrary"),
        name="gla",
    )(q, k, v, r, gk, s0, cn)


def _outproj_odd_kernel(x_ref, o_ref_in, w_ref, o_ref):
    o_ref[...] = x_ref[...] + _mm(o_ref_in[...], w_ref[...])


def _outproj_odd(x, o, w, li, tm):
    m, d = x.shape
    return pl.pallas_call(
        _outproj_odd_kernel,
        grid=(m // tm,),
        in_specs=[
            pl.BlockSpec((tm, d), lambda i: (i, 0)),
            pl.BlockSpec((tm, C_V), lambda i: (i, 0)),
            pl.BlockSpec((None, C_V, d), lambda i: (li, 0, 0)),
        ],
        out_specs=pl.BlockSpec((tm, d), lambda i: (i, 0)),
        out_shape=jax.ShapeDtypeStruct((m, d), F32),
        compiler_params=_cp("parallel"),
        name="outproj_odd",
    )(x, o, w)


def _pad_rows(a, rows):
    return jnp.pad(a, ((0, 0), (0, rows - a.shape[1]), (0, 0)))


def _pad_lanes(a):
    return jnp.pad(a, [(0, 0)] * (a.ndim - 1) + [(0, LANES - a.shape[-1])])


def kernel(x_prompt, x_sample, cache_k, cache_v, state_delta, state_conv, state_gla, page_table, rel_bias, norm_ffn1, norm_mix, norm_ffn2, ffn_in, ffn_out, w_in_even, a_qk_norm, a_lambda, a_subln, b_conv, b_a_log, b_dt_bias, b_norm, w_out_even, w_in_odd, c_gate_w2, c_gate_b, c_norm, w_out_odd):
    bp, tp, d = x_prompt.shape
    bs, ts, _ = x_sample.shape
    depth = norm_ffn1.shape[0]
    n_att = w_in_even.shape[0]
    n_pool, page = cache_k.shape[1], cache_k.shape[2]
    past = page_table.shape[1] * page
    tq = ATT_TILE
    cs = SUBLANES
    assert page == LANES and tp % tq == 0 and ts <= cs and ts >= CONV_W - 1
    assert _far_bucket_saturated(tq + 1) and _far_bucket_saturated(page + 1)

    ffn_in_bf = ffn_in.astype(BF16)
    ffn_out_bf = ffn_out.astype(BF16)
    w_even_bf = w_in_even.astype(BF16)
    w_even_small = _pad_lanes(w_in_even[:, :, IN_EVEN_MAIN:]).astype(BF16)
    w_out_even_bf = w_out_even.astype(BF16)
    w_odd_bf = w_in_odd.astype(BF16)
    w_odd_small = _pad_lanes(w_in_odd[:, :, IN_ODD_MAIN:]).astype(BF16)
    w_out_odd_bf = w_out_odd.astype(BF16)
    gate_w2 = jnp.pad(c_gate_w2, ((0, 0), (0, LANES - GLA_LOWRANK), (0, 0))).astype(BF16)
    gate_b = c_gate_b[:, None, :]
    n1 = norm_ffn1[:, None, :]
    nm = norm_mix[:, None, :]
    n2 = norm_ffn2[:, None, :]
    grp = jnp.kron(jnp.eye(2 * A_HEADS, dtype=F32), jnp.ones((A_DK, A_DK), F32)).astype(BF16)
    qgain = jnp.tile(a_qk_norm[:, 0:1, :], (1, 1, 2 * A_HEADS))
    kgain = jnp.tile(a_qk_norm[:, 1:2, :], (1, 1, 2 * A_HEADS))
    gparams = jnp.stack([_pad_lanes(-jnp.exp(b_a_log)), _pad_lanes(b_dt_bias)], axis=1)
    subln = a_subln[:, None, :]
    bnorm = b_norm[:, None, :]
    cnorm = c_norm[:, None, :]
    cache_k2 = cache_k.reshape(n_att, n_pool, page * A_HEADS, 2 * A_DK)
    cache_v2 = cache_v.reshape(n_att, n_pool, page * A_HEADS, A_DV)

    bias_p, bias_s = _bias_tiles(rel_bias, tq, past, ts)
    bias_s = bias_s.reshape(A_HEADS * 2 * ts, past * A_HEADS + LANES)

    xp = x_prompt.reshape(bp * tp, d)
    xs = x_sample.reshape(bs * ts, d)
    tm_p = ROWS_DENSE
    tm_s = bs * ts
    k_rows = [[], []]
    v_rows = [[], []]
    deltas = [[], []]
    convs = [[], []]
    glas = [[], []]

    for i in range(depth):
        li = i // 2
        xp = _ffn(xp, n1, ffn_in_bf, ffn_out_bf, i, 0, tm_p)
        xs = _ffn(xs, n1, ffn_in_bf, ffn_out_bf, i, 0, tm_s)
        if i % 2 == 0:
            lam_init = 0.8 - 0.6 * math.exp(-0.3 * i)
            lq1, lk1, lq2, lk2 = a_lambda[li]
            lam = jnp.exp(jnp.sum(lq1 * lk1)) - jnp.exp(jnp.sum(lq2 * lk2)) + lam_init
            sc = jnp.stack([lam, jnp.asarray(1.0 - lam_init, F32)]).astype(F32)

            q, k32, kbf, v32, vt, bqkv, bz, gb = _inproj_even(
                xp, nm, w_even_bf, w_even_small, grp, qgain, kgain, gparams, i, li, ROWS_INPROJ)
            oa = _attn_prompt(sc, q.reshape(bp, tp, A_QK), kbf.reshape(bp, tp, A_QK), vt, bias_p, subln, li, tq)
            bqkv3 = bqkv.reshape(bp, tp, B_CONV_DIM)
            ob, s_new = _delta(bqkv3, bz.reshape(bp, tp, B_V), gb.reshape(bp, tp, LANES),
                               jnp.zeros((bp, SUBLANES, B_CONV_DIM), F32),
                               jnp.zeros((bp, B_HEADS, B_DK, B_DV), F32), b_conv, bnorm, li, SEQ_CHUNK)
            xp = _outproj_even(xp, oa.reshape(bp * tp, A_V), ob.reshape(bp * tp, B_V), w_out_even_bf, li, tm_p)
            k_rows[0].append(k32.reshape(bp, tp, A_HEADS, 2 * A_DK))
            v_rows[0].append(v32.reshape(bp, tp, A_HEADS, A_DV))
            deltas[0].append(s_new)
            convs[0].append(bqkv3[:, tp - (CONV_W - 1):, :])

            q, k32, kbf, v32, _, bqkv, bz, gb = _inproj_even(
                xs, nm, w_even_bf, w_even_small, grp, qgain, kgain, gparams, i, li, tm_s)
            q4 = q.reshape(bs, ts, A_HEADS, 2, A_DK)
            qbd = jnp.einsum('bthcd,ce->bhcted', q4, jnp.eye(2, dtype=BF16)).reshape(bs, A_HEADS * 2 * ts, 2 * A_DK)
            knew = _pad_rows(kbf.reshape(bs, ts * A_HEADS, 2 * A_DK), LANES)
            vnew = _pad_rows(v32.reshape(bs, ts * A_HEADS, A_DV).astype(BF16), LANES)
            oa = _attn_sample(page_table, sc, qbd, bias_s, knew, vnew, subln, cache_k2, cache_v2, li, ts)
            oa = oa[:, :ts, :].astype(BF16)
            bqkv3 = bqkv.reshape(bs, ts, B_CONV_DIM)
            conv0 = jnp.pad(state_conv[li], ((0, 0), (SUBLANES - (CONV_W - 1), 0), (0, 0)))
            ob, s_new = _delta(_pad_rows(bqkv3, cs), _pad_rows(bz.reshape(bs, ts, B_V), cs),
                               _pad_rows(gb.reshape(bs, ts, LANES), cs), conv0, state_delta[li],
                               b_conv, bnorm, li, cs)
            xs = _outproj_even(xs, oa.reshape(bs * ts, A_V), ob[:, :ts, :].reshape(bs * ts, B_V),
                               w_out_even_bf, li, tm_s)
            k_rows[1].append(k32.reshape(bs, ts, A_HEADS, 2 * A_DK))
            v_rows[1].append(v32.reshape(bs, ts, A_HEADS, A_DV))
            deltas[1].append(s_new)
            convs[1].append(jnp.concatenate([state_conv[li], bqkv3], axis=1)[:, ts:, :])
        else:
            q, k, v, r, gk = _inproj_odd(xp, nm, w_odd_bf, w_odd_small, gate_w2, gate_b, i, li, ROWS_INPROJ)
            o, s_new = _gla(q.reshape(bp, tp, C_QK), k.reshape(bp, tp, C_QK), v.reshape(bp, tp, C_V),
                            r.reshape(bp, tp, C_V), gk.reshape(bp, tp, C_QK),
                            jnp.zeros((bp, C_HEADS, C_DK, C_DV), F32), cnorm, li, SEQ_CHUNK)
            xp = _outproj_odd(xp, o.reshape(bp * tp, C_V), w_out_odd_bf, li, tm_p)
            glas[0].append(s_new)

            q, k, v, r, gk = _inproj_odd(xs, nm, w_odd_bf, w_odd_small, gate_w2, gate_b, i, li, tm_s)
            pad = lambda a, n: _pad_rows(a.reshape(bs, ts, n), cs)
            o, s_new = _gla(pad(q, C_QK), pad(k, C_QK), pad(v, C_V), pad(r, C_V), pad(gk, C_QK),
                            state_gla[li], cnorm, li, cs)
            xs = _outproj_odd(xs, o[:, :ts, :].reshape(bs * ts, C_V), w_out_odd_bf, li, tm_s)
            glas[1].append(s_new)
        xp = _ffn(xp, n2, ffn_in_bf, ffn_out_bf, i, 1, tm_p)
        xs = _ffn(xs, n2, ffn_in_bf, ffn_out_bf, i, 1, tm_s)

    return (xp.reshape(bp, tp, d), xs.reshape(bs, ts, d),
            jnp.stack(k_rows[0]), jnp.stack(v_rows[0]), jnp.stack(deltas[0]), jnp.stack(convs[0]),
            jnp.stack(glas[0]),
            jnp.stack(k_rows[1]), jnp.stack(v_rows[1]), jnp.stack(deltas[1]), jnp.stack(convs[1]),
            jnp.stack(glas[1]))
```

```python
import functools
import math

import jax
import jax.numpy as jnp
from jax import lax
from jax.experimental import pallas as pl
from jax.experimental.pallas import tpu as pltpu

F32 = jnp.float32
BF16 = jnp.bfloat16

A_HEADS, A_DK, A_DV = 4, 64, 128
B_HEADS, B_DK, B_DV = 4, 128, 128
CONV_W = 4
C_HEADS, C_DK, C_DV = 4, 128, 256
GLA_LOWRANK, GLA_TAU = 16, 16.0
N_BUCKETS, MAX_DIST = 32, 128
EPS = 1e-6
NEG = -1e30
LOG2E = 1.4426950408889634

A_QK = A_HEADS * 2 * A_DK
A_V = A_HEADS * A_DV
B_QK = B_HEADS * B_DK
B_V = B_HEADS * B_DV
B_CONV_DIM = 2 * B_QK + B_V
C_QK = C_HEADS * C_DK
C_V = C_HEADS * C_DV
IN_EVEN_MAIN = 2 * A_QK + A_V + B_CONV_DIM + B_V
IN_ODD_MAIN = 2 * C_QK + 2 * C_V

LANES = 128
SUBLANES = 8
PACKED_SUBLANES = 16
VMEM_LIMIT = 56 * 2 ** 20

ROWS_DENSE = 512
ROWS_INPROJ = 256
ATT_TILE = 512
VT_ROWS = A_DV + PACKED_SUBLANES
SAFE_SPAN = 96.0
SEQ_CHUNK = 64
DELTA_PREP_CHUNKS = 4
GLA_SUB = 16
PAGES_PER_STEP = 8

NT = (((1,), (1,)), ((), ()))
TN = (((0,), (0,)), ((), ()))


def _cp(*sem):
    return pltpu.CompilerParams(dimension_semantics=sem, vmem_limit_bytes=VMEM_LIMIT)


def _mm(a, b, dn=None):
    if dn is None:
        return jnp.dot(a, b, preferred_element_type=F32)
    return lax.dot_general(a, b, dn, preferred_element_type=F32)


def _split2(x):
    hi = x.astype(BF16)
    lo = (x - hi.astype(F32)).astype(BF16)
    return hi, lo


def _split3(x):
    hi = x.astype(BF16)
    r = x - hi.astype(F32)
    mid = r.astype(BF16)
    lo = (r - mid.astype(F32)).astype(BF16)
    return hi, mid, lo


def _mm3(a, b):
    ah, al = _split2(a)
    bh, bl = _split2(b)
    return _mm(ah, bh) + (_mm(ah, bl) + _mm(al, bh))


def _mm_exact_lhs(a_bf, x):
    hi, mid, lo = _split3(x)
    return _mm(a_bf, hi) + (_mm(a_bf, mid) + _mm(a_bf, lo))


def _sigmoid(x):
    return 1.0 / (1.0 + jnp.exp(-x))


def _silu(x):
    return x * _sigmoid(x)


def _softplus(x):
    return jnp.maximum(x, 0.0) + jnp.log(1.0 + jnp.exp(-jnp.abs(x)))


def _rmsnorm(xf, g):
    return xf * lax.rsqrt(jnp.mean(xf * xf, axis=-1, keepdims=True) + EPS) * g


def _iota(shape, dim):
    return lax.broadcasted_iota(jnp.int32, shape, dim)


def _row_to_col(row, n):
    eye = _iota((n, n), 0) == _iota((n, n), 1)
    return jnp.sum(jnp.where(eye, jnp.broadcast_to(row, (n, n)), 0.0), axis=1, keepdims=True)


def _ffn_kernel(x_ref, g_ref, wg_ref, wu_ref, wo_ref, o_ref, h_ref):
    j = pl.program_id(1)

    @pl.when(j == 0)
    def _():
        xf = x_ref[...]
        h_ref[...] = _rmsnorm(xf, g_ref[...]).astype(BF16)
        o_ref[...] = xf

    h = h_ref[...]
    gate = _mm(h, wg_ref[...])
    up = _mm(h, wu_ref[...])
    a = (_silu(gate) * up).astype(BF16)
    o_ref[...] += 0.5 * _mm(a, wo_ref[...])


def _ffn(x, g, w_in, w_out, layer, which, tm):
    m, d = x.shape
    f = w_out.shape[2]
    nf = 2
    tf = f // nf
    assert f % nf == 0 and tf % LANES == 0 and m % tm == 0
    return pl.pallas_call(
        _ffn_kernel,
        grid=(m // tm, nf),
        in_specs=[
            pl.BlockSpec((tm, d), lambda i, j: (i, 0)),
            pl.BlockSpec((None, 1, d), lambda i, j: (layer, 0, 0)),
            pl.BlockSpec((None, None, d, tf), lambda i, j: (layer, which, 0, j)),
            pl.BlockSpec((None, None, d, tf), lambda i, j: (layer, which, 0, j + nf)),
            pl.BlockSpec((None, None, tf, d), lambda i, j: (layer, which, j, 0)),
        ],
        out_specs=pl.BlockSpec((tm, d), lambda i, j: (i, 0)),
        out_shape=jax.ShapeDtypeStruct((m, d), F32),
        scratch_shapes=[pltpu.VMEM((tm, d), BF16)],
        compiler_params=_cp("parallel", "arbitrary"),
        name="ffn",
    )(x, g, w_in, w_in, w_out)


def _group_rms(x, grp_bf, gain, width):
    hi, lo = _split2(x * x)
    ms = (_mm(hi, grp_bf) + _mm(lo, grp_bf)) * (1.0 / width)
    return x * lax.rsqrt(ms + EPS) * gain


def _inproj_even_kernel(x_ref, g_ref, w_ref, ws_ref, grp_ref, qg_ref, kg_ref, gp_ref,
                        q_ref, k32_ref, kbf_ref, v32_ref, vt_ref, bqkv_ref, bz_ref, gb_ref):
    h = _rmsnorm(x_ref[...], g_ref[...]).astype(BF16)
    grp = grp_ref[...]

    aq = _mm(h, w_ref[:, 0:A_QK])
    q = _group_rms(aq, grp, qg_ref[...], A_DK)
    q_ref[...] = (q * (A_DK ** -0.5 * LOG2E)).astype(BF16)

    ak = _mm(h, w_ref[:, A_QK:2 * A_QK])
    k = _group_rms(ak, grp, kg_ref[...], A_DK)
    kbf_ref[...] = k.astype(BF16)

    v = _mm(h, w_ref[:, 2 * A_QK:2 * A_QK + A_V])
    tm = v.shape[0]
    ones = jnp.ones((VT_ROWS - A_DV, tm), BF16)
    for hd in range(A_HEADS):
        k32_ref[pl.ds(hd, tm, stride=A_HEADS), :] = k[:, 2 * A_DK * hd:2 * A_DK * (hd + 1)]
        v32_ref[pl.ds(hd, tm, stride=A_HEADS), :] = v[:, A_DV * hd:A_DV * (hd + 1)]
        vt_ref[hd, 0:A_DV, :] = v[:, A_DV * hd:A_DV * (hd + 1)].T.astype(BF16)
        vt_ref[hd, A_DV:VT_ROWS, :] = ones

    o = 2 * A_QK + A_V
    bqkv_ref[...] = _mm(h, w_ref[:, o:o + B_CONV_DIM])
    o += B_CONV_DIM
    bz_ref[...] = _mm(h, w_ref[:, o:o + B_V])

    y = _mm(h, ws_ref[...])
    lane = _iota(y.shape, 1)
    g = gp_ref[0:1, :] * _softplus(y + gp_ref[1:2, :])
    gb_ref[...] = jnp.where(lane < B_HEADS, g, jnp.where(lane < 2 * B_HEADS, _sigmoid(y), 0.0))


def _inproj_even(x, g, w, ws, grp, qg, kg, gp, layer, li, tm):
    m, d = x.shape
    wmain = IN_EVEN_MAIN
    row = lambda n: pl.BlockSpec((tm, n), lambda i: (i, 0))
    out_shapes = [
        jax.ShapeDtypeStruct((m, A_QK), BF16),
        jax.ShapeDtypeStruct((m * A_HEADS, 2 * A_DK), F32),
        jax.ShapeDtypeStruct((m, A_QK), BF16),
        jax.ShapeDtypeStruct((m * A_HEADS, A_DV), F32),
        jax.ShapeDtypeStruct((A_HEADS, m // tm, VT_ROWS, tm), BF16),
        jax.ShapeDtypeStruct((m, B_CONV_DIM), F32),
        jax.ShapeDtypeStruct((m, B_V), F32),
        jax.ShapeDtypeStruct((m, LANES), F32),
    ]
    out_specs = [row(s.shape[1]) for s in out_shapes]
    out_specs[1] = pl.BlockSpec((tm * A_HEADS, 2 * A_DK), lambda i: (i, 0))
    out_specs[3] = pl.BlockSpec((tm * A_HEADS, A_DV), lambda i: (i, 0))
    out_specs[4] = pl.BlockSpec((A_HEADS, None, VT_ROWS, tm), lambda i: (0, i, 0, 0))
    return pl.pallas_call(
        _inproj_even_kernel,
        grid=(m // tm,),
        in_specs=[
            row(d),
            pl.BlockSpec((None, 1, d), lambda i: (layer, 0, 0)),
            pl.BlockSpec((None, d, wmain), lambda i: (li, 0, 0)),
            pl.BlockSpec((None, d, LANES), lambda i: (li, 0, 0)),
            pl.BlockSpec((A_QK, A_QK), lambda i: (0, 0)),
            pl.BlockSpec((None, 1, A_QK), lambda i: (li, 0, 0)),
            pl.BlockSpec((None, 1, A_QK), lambda i: (li, 0, 0)),
            pl.BlockSpec((None, 2, LANES), lambda i: (li, 0, 0)),
        ],
        out_specs=out_specs,
        out_shape=out_shapes,
        compiler_params=_cp("parallel"),
        name="inproj_even",
    )(x, g, w, ws, grp, qg, kg, gp)


def _bias_kernel(rb_ref, bp_ref, bs_ref, *, tq, past, tnew):
    h = pl.program_id(0)
    far = rb_ref[N_BUCKETS - 1, h]
    exact = N_BUCKETS // 2

    def table(rel):
        n = jnp.maximum(rel, 0)
        nf = jnp.maximum(n, 1).astype(F32)
        large = exact + (jnp.log(nf / exact) / math.log(MAX_DIST / exact) * (N_BUCKETS - exact)).astype(jnp.int32)
        bucket = jnp.where(n < exact, n, jnp.minimum(large, N_BUCKETS - 1))
        val = lax.fori_loop(0, N_BUCKETS, lambda b, acc: jnp.where(bucket == b, rb_ref[b, h], acc),
                            jnp.zeros(rel.shape, F32))
        return jnp.where(rel >= 0, (val - far) * LOG2E, NEG)

    key_p = _iota((tq, tq), 0)
    qry_p = _iota((tq, tq), 1)
    for d in range(2):
        bp_ref[d] = table(d * tq + qry_p - key_p)

    ncol = past * A_HEADS + LANES
    t_idx = _iota((2 * tnew, ncol), 0) & (tnew - 1)
    col = _iota((2 * tnew, ncol), 1)
    key = col >> 2
    rel_new = jnp.where(key - past < tnew, t_idx - (key - past), -1)
    rel = jnp.where(key < past, past + t_idx - key, rel_new)
    bs_ref[...] = table(jnp.where((col & (A_HEADS - 1)) == h, rel, -1))


def _bias_tiles(rel_bias, tq, past, tnew):
    assert tnew & (tnew - 1) == 0 and 2 * tnew == SUBLANES and A_HEADS == 4
    ncol = past * A_HEADS + LANES
    return pl.pallas_call(
        functools.partial(_bias_kernel, tq=tq, past=past, tnew=tnew),
        grid=(A_HEADS,),
        in_specs=[pl.BlockSpec(memory_space=pltpu.SMEM)],
        out_specs=[
            pl.BlockSpec((None, 2, tq, tq), lambda h: (h, 0, 0, 0)),
            pl.BlockSpec((None, 2 * tnew, ncol), lambda h: (h, 0, 0)),
        ],
        out_shape=[
            jax.ShapeDtypeStruct((A_HEADS, 2, tq, tq), F32),
            jax.ShapeDtypeStruct((A_HEADS, 2 * tnew, ncol), F32),
        ],
        compiler_params=_cp("parallel"),
        name="t5_bias",
    )(rel_bias)


def _far_bucket_saturated(dist):
    exact = N_BUCKETS // 2
    v = math.log(dist / exact) / math.log(MAX_DIST / exact) * (N_BUCKETS - exact)
    return dist >= exact and exact + v >= N_BUCKETS - 1 + 1e-3


def _attn_prompt_kernel(sc_ref, q_ref, k_ref, vt_ref, bias_ref, sg_ref, o_ref, *, tq, vblk):
    head = pl.program_id(1)
    i = pl.program_id(2)
    q = q_ref[...]
    lane = _iota(q.shape, 1)
    zero = jnp.zeros_like(q)
    qbd = jnp.concatenate([jnp.where(lane < A_DK, q, zero), jnp.where(lane >= A_DK, q, zero)], axis=0)
    nsub = tq // vblk

    def scores(j):
        start = pl.multiple_of(j * tq, tq)
        return _mm(k_ref[pl.ds(start, tq), :], qbd, NT)

    def biased(j, bias):
        s = scores(j)
        return s if bias is None else s + jnp.concatenate([bias, bias], axis=1)

    def weighted_values(j, p):
        pv = _mm(vt_ref[j * nsub], p[0:vblk])
        for u in range(1, nsub):
            pv = pv + _mm(vt_ref[j * nsub + u], p[u * vblk:(u + 1) * vblk])
        return pv

    def key_blocks(step, carry):
        nfar = jnp.maximum(i - 1, 0)
        carry = lax.fori_loop(0, nfar // 2, lambda j, c: step(2 * j + 1, step(2 * j, c, None), None), carry)
        carry = lax.cond(nfar % 2 == 1, lambda c: step(nfar - 1, c, None), lambda c: c, carry)
        return lax.cond(i >= 1, lambda c: step(i, step(i - 1, c, bias_ref[1]), bias_ref[0]),
                        lambda c: step(i, c, bias_ref[0]), carry)

    def fixed_shift(_):
        shift = sc_ref[3 + head]

        def step(j, acc, bias):
            return acc + weighted_values(j, jnp.exp2(biased(j, bias) - shift).astype(BF16))

        return key_blocks(step, jnp.zeros((VT_ROWS, 2 * tq), F32))

    def running_max(_):
        def step(j, c, bias):
            m, acc = c
            s = biased(j, bias)
            m_new = jnp.maximum(m, jnp.max(s, axis=0, keepdims=True))
            p = jnp.exp2(s - m_new).astype(BF16)
            return m_new, jnp.exp2(m - m_new) * acc + weighted_values(j, p)

        return key_blocks(step, (jnp.full((1, 2 * tq), NEG, F32), jnp.zeros((VT_ROWS, 2 * tq), F32)))[1]

    acc = lax.cond(sc_ref[2] > 0.5, fixed_shift, running_max, 0)

    o = acc[0:A_DV] / acc[A_DV:A_DV + 1]
    o = o[:, 0:tq] - sc_ref[0] * o[:, tq:2 * tq]
    o = o * lax.rsqrt(jnp.mean(o * o, axis=0, keepdims=True) + EPS)
    o_ref[...] = (o.T * (sg_ref[...] * sc_ref[1])).astype(BF16)


def _attn_prompt(sc, q, k, vt, bias_p, sg, li, tq):
    bsz, t, _ = q.shape
    vblk = vt.shape[3]
    assert tq % vblk == 0 and t % tq == 0
    return pl.pallas_call(
        functools.partial(_attn_prompt_kernel, tq=tq, vblk=vblk),
        grid=(bsz, A_HEADS, t // tq),
        in_specs=[
            pl.BlockSpec(memory_space=pltpu.SMEM),
            pl.BlockSpec((None, tq, 2 * A_DK), lambda b, h, i: (b, i, h)),
            pl.BlockSpec((None, t, 2 * A_DK), lambda b, h, i: (b, 0, h)),
            pl.BlockSpec((None, t // vblk, VT_ROWS, vblk), lambda b, h, i: (h, b, 0, 0)),
            pl.BlockSpec((None, 2, tq, tq), lambda b, h, i: (h, 0, 0, 0)),
            pl.BlockSpec((None, 1, A_DV), lambda b, h, i: (li, 0, 0)),
        ],
        out_specs=pl.BlockSpec((None, tq, A_DV), lambda b, h, i: (b, i, h)),
        out_shape=jax.ShapeDtypeStruct((bsz, t, A_V), BF16),
        compiler_params=_cp("parallel", "parallel", "arbitrary"),
        name="attn_prompt",
    )(sc, q, k, vt, bias_p, sg)


def _attn_sample_kernel(pt_ref, sc_ref, qbd_ref, bias_ref, biasn_ref, kn_ref, vn_ref, sg_ref, *rest, npg, tnew):
    k_refs = rest[:npg]
    v_refs = rest[npg:2 * npg]
    o_ref, m_ref, l_ref, acc_ref = rest[2 * npg:]
    pg = pl.program_id(1)
    qbd = qbd_ref[...]

    @pl.when(pg == 0)
    def _():
        m_ref[...] = jnp.full(m_ref.shape, NEG, F32)
        l_ref[...] = jnp.zeros(l_ref.shape, F32)
        acc_ref[...] = jnp.zeros(acc_ref.shape, F32)

    def update(s, vs):
        m_old = m_ref[...]
        m_new = jnp.maximum(m_old, jnp.max(s, axis=1, keepdims=True))
        alpha = jnp.exp2(m_old - m_new)
        p = jnp.exp2(s - m_new)
        l_ref[...] = alpha * l_ref[...] + jnp.sum(p, axis=1, keepdims=True)
        pb = p.astype(BF16)
        w = pb.shape[1] // len(vs)
        pv = _mm(pb[:, 0:w], vs[0])
        for g in range(1, len(vs)):
            pv = pv + _mm(pb[:, g * w:(g + 1) * w], vs[g])
        acc_ref[...] = alpha * acc_ref[...] + pv
        m_ref[...] = m_new

    s = jnp.concatenate([_mm(qbd, kr[...].astype(BF16), NT) for kr in k_refs], axis=1) + bias_ref[...]
    update(s, [vr[...].astype(BF16) for vr in v_refs])

    @pl.when(pg == pl.num_programs(1) - 1)
    def _():
        update(_mm(qbd, kn_ref[...], NT) + biasn_ref[...], [vn_ref[...]])
        lam = sc_ref[0]
        rows = 2 * tnew
        for h in range(A_HEADS):
            a = acc_ref[rows * h:rows * (h + 1), :] / l_ref[rows * h:rows * (h + 1), :]
            o = a - lam * pltpu.roll(a, tnew, axis=0)
            o_ref[:, A_DV * h:A_DV * (h + 1)] = _rmsnorm(o, sg_ref[...]) * sc_ref[1]


def _attn_sample(page_table, sc, qbd, bias_s, knew, vnew, sg, cache_k, cache_v, li, tnew):
    bsz, nrow, _ = qbd.shape
    n_pages = page_table.shape[1]
    npg = PAGES_PER_STEP
    assert n_pages % npg == 0
    prow = cache_k.shape[2]
    past_cols = n_pages * prow

    def kvspec(g):
        return pl.BlockSpec((None, None, prow, 2 * A_DK), lambda b, pg, pt: (li, pt[b, pg * npg + g], 0, 0))

    grid_spec = pltpu.PrefetchScalarGridSpec(
        num_scalar_prefetch=1,
        grid=(bsz, n_pages // npg),
        in_specs=[
            pl.BlockSpec(memory_space=pltpu.SMEM),
            pl.BlockSpec((None, nrow, 2 * A_DK), lambda b, pg, pt: (b, 0, 0)),
            pl.BlockSpec((nrow, npg * prow), lambda b, pg, pt: (0, pg)),
            pl.BlockSpec((nrow, LANES), lambda b, pg, pt: (0, past_cols // LANES)),
            pl.BlockSpec((None, LANES, 2 * A_DK), lambda b, pg, pt: (b, 0, 0)),
            pl.BlockSpec((None, LANES, A_DV), lambda b, pg, pt: (b, 0, 0)),
            pl.BlockSpec((None, 1, A_DV), lambda b, pg, pt: (li, 0, 0)),
        ] + [kvspec(g) for g in range(npg)] + [kvspec(g) for g in range(npg)],
        out_specs=pl.BlockSpec((None, 2 * tnew, A_V), lambda b, pg, pt: (b, 0, 0)),
        scratch_shapes=[pltpu.VMEM((nrow, 1), F32), pltpu.VMEM((nrow, 1), F32), pltpu.VMEM((nrow, A_DV), F32)],
    )
    return pl.pallas_call(
        functools.partial(_attn_sample_kernel, npg=npg, tnew=tnew),
        grid_spec=grid_spec,
        out_shape=jax.ShapeDtypeStruct((bsz, 2 * tnew, A_V), F32),
        compiler_params=_cp("parallel", "arbitrary"),
        name="attn_sample",
    )(page_table, sc, qbd, bias_s, bias_s, knew, vnew, sg, *([cache_k] * npg), *([cache_v] * npg))


def _l2norm(x):
    return x * lax.rsqrt(jnp.sum(x * x, axis=-1, keepdims=True) + EPS)


def _unit_lower_inverses(mats, c):
    n = mats[0].shape[0]
    eye = (_iota((n, n), 0) == _iota((n, n), 1)).astype(F32)
    ps = [eye - a for a in mats]
    qs = [_split2(a) for a in mats]
    span = 2
    while span < c:
        qs = [_split2(_mm(qh, qh) + (_mm(qh, ql) + _mm(ql, qh))) for qh, ql in qs]
        pss = [_split2(p) for p in ps]
        ps = [p + (_mm(ph, qh) + (_mm(ph, ql) + _mm(pl_, qh))) for p, (ph, pl_), (qh, ql) in zip(ps, pss, qs)]
        span *= 2
    return ps


def _cumsum_rows(x, n):
    row = _iota(x.shape, 0)
    k = 1
    while k < n:
        x = x + jnp.where(row >= k, pltpu.roll(x, k, axis=0), 0.0)
        k *= 2
    return x


def _delta_prep_kernel(x_ref, halo_ref, c0_ref, gb_ref, cw_ref, u_ref, wk_ref, qe_ref, kd_ref, qk_ref, xbuf,
                       *, c, nck):
    ci = pl.program_id(1)
    ch = B_HEADS * c
    rows = nck * c
    xbuf[0:SUBLANES, :] = jnp.where(ci == 0, c0_ref[...], halo_ref[...])
    xbuf[SUBLANES:SUBLANES + rows, :] = x_ref[...]
    first = SUBLANES - (CONV_W - 1)
    conv = xbuf[pl.ds(first, rows), :] * cw_ref[0:1, :]
    for j in range(1, CONV_W):
        conv = conv + xbuf[pl.ds(first + j, rows), :] * cw_ref[j:j + 1, :]
    conv = _silu(conv)

    row = _iota((ch, ch), 0)
    col = _iota((ch, ch), 1)
    shift = c.bit_length() - 1
    same = (row >> shift) == (col >> shift)
    incl = same & (row >= col)
    strict = same & (row > col)

    def stack_lane(a, off):
        return jnp.concatenate([a[:, off + h:off + h + 1] for h in range(B_HEADS)], axis=0)

    parts = []
    for n in range(nck):
        rs = slice(n * c, (n + 1) * c)

        def stack(off, width):
            return jnp.concatenate([conv[rs, off + width * h:off + width * (h + 1)] for h in range(B_HEADS)], axis=0)

        q = _l2norm(stack(0, B_DK)) * (B_DK ** -0.5)
        k = _l2norm(stack(B_QK, B_DK))
        v = stack(2 * B_QK, B_DV)
        gb = gb_ref[rs, :]
        gcum = _cumsum_rows(gb, c)
        gcol = stack_lane(gcum, 0)
        beta = stack_lane(gb, B_HEADS)
        glast = stack_lane(jnp.broadcast_to(gcum[c - 1:c, :], gcum.shape), 0)
        grow = jnp.broadcast_to(gcol, (ch, LANES)).T[0:1, :]
        dec = jnp.where(incl, jnp.exp(jnp.where(incl, gcol - grow, 0.0)), 0.0)
        kb = k.astype(BF16)
        a = jnp.where(strict, _mm(kb, kb, NT) * dec * beta, 0.0)
        egc = jnp.exp(gcol)
        qe_ref[n] = (q * egc).astype(BF16)
        kd_ref[n] = (k * jnp.exp(glast - gcol)).astype(BF16)
        qk_ref[n] = (_mm(q.astype(BF16), kb, NT) * dec).astype(BF16)
        parts.append((a, jnp.concatenate([v * beta, k * (beta * egc)], axis=1)))

    t_invs = _unit_lower_inverses([a for a, _ in parts], c)
    for n in range(nck):
        uw = _mm3(t_invs[n], parts[n][1])
        u_ref[n] = uw[:, 0:B_DV]
        wk_ref[n] = uw[:, B_DV:B_DV + B_DK].astype(BF16)


def _delta_scan_kernel(u_ref, wk_ref, qe_ref, kd_ref, qk_ref, gb_ref, z_ref, s0_ref, bn_ref, ob_ref, sout_ref,
                       s_scr, *, c, nb):
    ci = pl.program_id(1)

    @pl.when(ci == 0)
    def _():
        s_scr[...] = s0_ref[...]

    hs = [slice(c * h, c * (h + 1)) for h in range(B_HEADS)]
    for b in range(nb):
        decay = jnp.exp(jnp.sum(gb_ref[b], axis=0, keepdims=True))
        s = [s_scr[b, h] for h in range(B_HEADS)]
        s_bf = [x.astype(BF16) for x in s]
        v_new = u_ref[b] - jnp.concatenate([_mm(wk_ref[b, hs[h], :], s_bf[h]) for h in range(B_HEADS)], axis=0)
        vn_bf = v_new.astype(BF16)
        o = (jnp.concatenate([_mm(qe_ref[b, hs[h], :], s_bf[h]) for h in range(B_HEADS)], axis=0)
             + _mm(qk_ref[b], vn_bf))
        for h in range(B_HEADS):
            s_scr[b, h] = s[h] * decay[:, h:h + 1] + _mm(kd_ref[b, hs[h], :], vn_bf[hs[h]], TN)
            sl = slice(B_DV * h, B_DV * (h + 1))
            ob_ref[b, :, sl] = (_rmsnorm(o[hs[h]], bn_ref[...]) * _silu(z_ref[b, :, sl])).astype(BF16)

    @pl.when(ci == pl.num_programs(1) - 1)
    def _():
        sout_ref[...] = s_scr[...]


def _delta(bqkv, bz, gb, conv0, s0, cw, bn, li, c):
    bsz, t, _ = bqkv.shape
    assert t % c == 0 and c % SUBLANES == 0 and c & (c - 1) == 0
    nch = t // c
    ch = B_HEADS * c
    nck = min(DELTA_PREP_CHUNKS, nch)
    assert nch % nck == 0
    hb = nck * c // SUBLANES
    tok = lambda n: pl.BlockSpec((None, c, n), lambda b, ci: (b, ci, 0))
    step_tok = lambda n: pl.BlockSpec((None, nck * c, n), lambda b, ci: (b, ci, 0))
    step_chunks = lambda n: pl.BlockSpec((None, nck, ch, n), lambda b, ci: (b, ci, 0, 0))
    chunk_shape = lambda n, dt: jax.ShapeDtypeStruct((bsz, nch, ch, n), dt)
    u, wk, qe, kd, qk = pl.pallas_call(
        functools.partial(_delta_prep_kernel, c=c, nck=nck),
        grid=(bsz, nch // nck),
        in_specs=[
            step_tok(B_CONV_DIM),
            pl.BlockSpec((None, SUBLANES, B_CONV_DIM), lambda b, ci: (b, jnp.maximum(ci * hb - 1, 0), 0)),
            pl.BlockSpec((None, SUBLANES, B_CONV_DIM), lambda b, ci: (b, 0, 0)),
            step_tok(LANES),
            pl.BlockSpec((None, CONV_W, B_CONV_DIM), lambda b, ci: (li, 0, 0)),
        ],
        out_specs=[step_chunks(B_DV), step_chunks(B_DK), step_chunks(B_DK), step_chunks(B_DK), step_chunks(ch)],
        out_shape=[chunk_shape(B_DV, F32), chunk_shape(B_DK, BF16), chunk_shape(B_DK, BF16),
                   chunk_shape(B_DK, BF16), chunk_shape(ch, BF16)],
        scratch_shapes=[pltpu.VMEM((SUBLANES + nck * c, B_CONV_DIM), F32)],
        compiler_params=_cp("parallel", "parallel"),
        name="delta_prep",
    )(bqkv, bqkv, conv0, gb, cw)

    nb = 2
    assert bsz % nb == 0
    grp_chunk = lambda n: pl.BlockSpec((nb, None, ch, n), lambda b, ci: (b, ci, 0, 0))
    grp_tok = lambda n: pl.BlockSpec((nb, c, n), lambda b, ci: (b, ci, 0))
    state = pl.BlockSpec((nb, B_HEADS, B_DK, B_DV), lambda b, ci: (b, 0, 0, 0))
    return pl.pallas_call(
        functools.partial(_delta_scan_kernel, c=c, nb=nb),
        grid=(bsz // nb, nch),
        in_specs=[grp_chunk(B_DV), grp_chunk(B_DK), grp_chunk(B_DK), grp_chunk(B_DK), grp_chunk(ch),
                  grp_tok(LANES), grp_tok(B_V), state,
                  pl.BlockSpec((None, 1, B_DV), lambda b, ci: (li, 0, 0))],
        out_specs=[grp_tok(B_V), state],
        out_shape=[jax.ShapeDtypeStruct((bsz, t, B_V), BF16),
                   jax.ShapeDtypeStruct((bsz, B_HEADS, B_DK, B_DV), F32)],
        scratch_shapes=[pltpu.VMEM((nb, B_HEADS, B_DK, B_DV), F32)],
        compiler_params=_cp("parallel", "arbitrary"),
        name="delta_scan",
    )(u, wk, qe, kd, qk, gb, bz, s0, bn)


def _outproj_even_kernel(x_ref, oa_ref, ob_ref, w_ref, o_ref):
    o_ref[...] = x_ref[...] + (_mm(oa_ref[...], w_ref[0:A_V, :]) + _mm(ob_ref[...], w_ref[A_V:A_V + B_V, :]))


def _outproj_even(x, oa, ob, w, li, tm):
    m, d = x.shape
    return pl.pallas_call(
        _outproj_even_kernel,
        grid=(m // tm,),
        in_specs=[
            pl.BlockSpec((tm, d), lambda i: (i, 0)),
            pl.BlockSpec((tm, A_V), lambda i: (i, 0)),
            pl.BlockSpec((tm, B_V), lambda i: (i, 0)),
            pl.BlockSpec((None, A_V + B_V, d), lambda i: (li, 0, 0)),
        ],
        out_specs=pl.BlockSpec((tm, d), lambda i: (i, 0)),
        out_shape=jax.ShapeDtypeStruct((m, d), F32),
        compiler_params=_cp("parallel"),
        name="outproj_even",
    )(x, oa, ob, w)


def _inproj_odd_kernel(x_ref, g_ref, w_ref, ws_ref, w2_ref, gbias_ref, q_ref, k_ref, v_ref, r_ref, gk_ref):
    h = _rmsnorm(x_ref[...], g_ref[...]).astype(BF16)
    q_ref[...] = _mm(h, w_ref[:, 0:C_QK])
    k_ref[...] = _mm(h, w_ref[:, C_QK:2 * C_QK])
    v_ref[...] = _mm(h, w_ref[:, 2 * C_QK:2 * C_QK + C_V]).astype(BF16)
    r_ref[...] = _mm(h, w_ref[:, 2 * C_QK + C_V:2 * C_QK + 2 * C_V])
    glr = _mm(h, ws_ref[...]).astype(BF16)
    y = _mm(glr, w2_ref[...]) + gbias_ref[...]
    gk_ref[...] = -_softplus(-y) * (1.0 / GLA_TAU)


def _inproj_odd(x, g, w, ws, w2, gbias, layer, li, tm):
    m, d = x.shape
    wmain = 2 * C_QK + 2 * C_V
    row = lambda n: pl.BlockSpec((tm, n), lambda i: (i, 0))
    out_shapes = [
        jax.ShapeDtypeStruct((m, C_QK), F32),
        jax.ShapeDtypeStruct((m, C_QK), F32),
        jax.ShapeDtypeStruct((m, C_V), BF16),
        jax.ShapeDtypeStruct((m, C_V), F32),
        jax.ShapeDtypeStruct((m, C_QK), F32),
    ]
    return pl.pallas_call(
        _inproj_odd_kernel,
        grid=(m // tm,),
        in_specs=[
            row(d),
            pl.BlockSpec((None, 1, d), lambda i: (layer, 0, 0)),
            pl.BlockSpec((None, d, wmain), lambda i: (li, 0, 0)),
            pl.BlockSpec((None, d, LANES), lambda i: (li, 0, 0)),
            pl.BlockSpec((None, LANES, C_QK), lambda i: (li, 0, 0)),
            pl.BlockSpec((None, 1, C_QK), lambda i: (li, 0, 0)),
        ],
        out_specs=[row(s.shape[1]) for s in out_shapes],
        out_shape=out_shapes,
        compiler_params=_cp("parallel"),
        name="inproj_odd",
    )(x, g, w, ws, w2, gbias)


def _gla_kernel(q_ref, k_ref, v_ref, r_ref, gk_ref, s0_ref, cn_ref, o_ref, sout_ref, s_scr, *, c, sub):
    ci = pl.program_id(1)

    @pl.when(ci == 0)
    def _():
        s_scr[...] = s0_ref[...]

    row = _iota((c, c), 0)
    col = _iota((c, c), 1)
    lower_bf = (row >= col).astype(BF16)
    in_sub = row & (sub - 1)
    sub_start = row - in_sub
    b_all = _mm_exact_lhs(lower_bf, gk_ref[...])

    for h in range(C_HEADS):
        sl = slice(C_DK * h, C_DK * (h + 1))
        vl = slice(C_DV * h, C_DV * (h + 1))
        q = q_ref[:, sl] * (C_DK ** -0.5)
        k = k_ref[:, sl]
        b = b_all[:, sl]
        v = v_ref[:, vl]
        s = s_scr[h]

        o = _mm((q * jnp.exp(b)).astype(BF16), s.astype(BF16))

        blocks = [jnp.zeros((sub, c), F32)]
        for r in range(1, c // sub):
            ref = b[sub * r:sub * r + 1, :]
            qt = q[sub * r:sub * (r + 1)] * jnp.exp(b[sub * r:sub * (r + 1)] - ref)
            kt = k * jnp.exp(jnp.minimum(ref - b, 0.0))
            blocks.append(_mm(qt.astype(BF16), kt.astype(BF16), NT))
        a = jnp.where(col < sub_start, jnp.concatenate(blocks, axis=0), 0.0)
        b2 = b * LOG2E
        for d in range(sub):
            kd = k if d == 0 else pltpu.roll(k, d, axis=0)
            bd = b2 if d == 0 else pltpu.roll(b2, d, axis=0)
            ad = jnp.sum(q * kd * jnp.exp2(b2 - bd), axis=1, keepdims=True)
            a = jnp.where((col == row - d) & (in_sub >= d), ad, a)
        o = o + _mm(a.astype(BF16), v)

        blast = b[c - 1:c, :]
        kdec = (k * jnp.exp(blast - b)).astype(BF16)
        s_scr[h] = s * _row_to_col(jnp.exp(blast), C_DK) + _mm(kdec, v, TN)

        o_ref[:, vl] = (_rmsnorm(o, cn_ref[...]) * _silu(r_ref[:, vl])).astype(BF16)

    @pl.when(ci == pl.num_programs(1) - 1)
    def _():
        sout_ref[...] = s_scr[...]


def _gla(q, k, v, r, gk, s0, cn, li, c):
    bsz, t, _ = q.shape
    assert t % c == 0
    sub = min(GLA_SUB, c)
    tok = lambda n: pl.BlockSpec((None, c, n), lambda b, ci: (b, ci, 0))
    state = pl.BlockSpec((None, C_HEADS, C_DK, C_DV), lambda b, ci: (b, 0, 0, 0))
    return pl.pallas_call(
        functools.partial(_gla_kernel, c=c, sub=sub),
        grid=(bsz, t // c),
        in_specs=[tok(C_QK), tok(C_QK), tok(C_V), tok(C_V), tok(C_QK), state,
                  pl.BlockSpec((None, 1, C_DV), lambda b, ci: (li, 0, 0))],
        out_specs=[tok(C_V), state],
        out_shape=[jax.ShapeDtypeStruct((bsz, t, C_V), BF16),
                   jax.ShapeDtypeStruct((bsz, C_HEADS, C_DK, C_DV), F32)],
        scratch_shapes=[pltpu.VMEM((C_HEADS, C_DK, C_DV), F32)],
        compiler_params=_cp("parallel", "arbitrary"),
        name="gla",
    )(q, k, v, r, gk, s0, cn)


def _outproj_odd_kernel(x_ref, o_ref_in, w_ref, o_ref):
    o_ref[...] = x_ref[...] + _mm(o_ref_in[...], w_ref[...])


def _outproj_odd(x, o, w, li, tm):
    m, d = x.shape
    return pl.pallas_call(
        _outproj_odd_kernel,
        grid=(m // tm,),
        in_specs=[
            pl.BlockSpec((tm, d), lambda i: (i, 0)),
            pl.BlockSpec((tm, C_V), lambda i: (i, 0)),
            pl.BlockSpec((None, C_V, d), lambda i: (li, 0, 0)),
        ],
        out_specs=pl.BlockSpec((tm, d), lambda i: (i, 0)),
        out_shape=jax.ShapeDtypeStruct((m, d), F32),
        compiler_params=_cp("parallel"),
        name="outproj_odd",
    )(x, o, w)


def _pad_rows(a, rows):
    return jnp.pad(a, ((0, 0), (0, rows - a.shape[1]), (0, 0)))


def _pad_lanes(a):
    return jnp.pad(a, [(0, 0)] * (a.ndim - 1) + [(0, LANES - a.shape[-1])])


def kernel(x_prompt, x_sample, cache_k, cache_v, state_delta, state_conv, state_gla, page_table, rel_bias, norm_ffn1, norm_mix, norm_ffn2, ffn_in, ffn_out, w_in_even, a_qk_norm, a_lambda, a_subln, b_conv, b_a_log, b_dt_bias, b_norm, w_out_even, w_in_odd, c_gate_w2, c_gate_b, c_norm, w_out_odd):
    bp, tp, d = x_prompt.shape
    bs, ts, _ = x_sample.shape
    depth = norm_ffn1.shape[0]
    n_att = w_in_even.shape[0]
    n_pool, page = cache_k.shape[1], cache_k.shape[2]
    past = page_table.shape[1] * page
    tq = ATT_TILE
    cs = SUBLANES
    assert page == LANES and tp % tq == 0 and ts <= cs and ts >= CONV_W - 1
    assert _far_bucket_saturated(tq + 1) and _far_bucket_saturated(page + 1)

    ffn_in_bf = ffn_in.astype(BF16)
    ffn_out_bf = ffn_out.astype(BF16)
    w_even_bf = w_in_even.astype(BF16)
    w_even_small = _pad_lanes(w_in_even[:, :, IN_EVEN_MAIN:]).astype(BF16)
    w_out_even_bf = w_out_even.astype(BF16)
    w_odd_bf = w_in_odd.astype(BF16)
    w_odd_small = _pad_lanes(w_in_odd[:, :, IN_ODD_MAIN:]).astype(BF16)
    w_out_odd_bf = w_out_odd.astype(BF16)
    gate_w2 = jnp.pad(c_gate_w2, ((0, 0), (0, LANES - GLA_LOWRANK), (0, 0))).astype(BF16)
    gate_b = c_gate_b[:, None, :]
    n1 = norm_ffn1[:, None, :]
    nm = norm_mix[:, None, :]
    n2 = norm_ffn2[:, None, :]
    grp = jnp.kron(jnp.eye(2 * A_HEADS, dtype=F32), jnp.ones((A_DK, A_DK), F32)).astype(BF16)
    qgain = jnp.tile(a_qk_norm[:, 0:1, :], (1, 1, 2 * A_HEADS))
    kgain = jnp.tile(a_qk_norm[:, 1:2, :], (1, 1, 2 * A_HEADS))
    gparams = jnp.stack([_pad_lanes(-jnp.exp(b_a_log)), _pad_lanes(b_dt_bias)], axis=1)
    subln = a_subln[:, None, :]
    bnorm = b_norm[:, None, :]
    cnorm = c_norm[:, None, :]
    cache_k2 = cache_k.reshape(n_att, n_pool, page * A_HEADS, 2 * A_DK)
    cache_v2 = cache_v.reshape(n_att, n_pool, page * A_HEADS, A_DV)

    bias_l2 = (rel_bias - rel_bias[N_BUCKETS - 1:N_BUCKETS]) * LOG2E
    bias_p, bias_s = _bias_tiles(rel_bias, tq, past, ts)
    bias_s = bias_s.reshape(A_HEADS * 2 * ts, past * A_HEADS + LANES)

    xp = x_prompt.reshape(bp * tp, d)
    xs = x_sample.reshape(bs * ts, d)
    tm_p = ROWS_DENSE
    tm_s = bs * ts
    k_rows = [[], []]
    v_rows = [[], []]
    deltas = [[], []]
    convs = [[], []]
    glas = [[], []]

    for i in range(depth):
        li = i // 2
        xp = _ffn(xp, n1, ffn_in_bf, ffn_out_bf, i, 0, tm_p)
        xs = _ffn(xs, n1, ffn_in_bf, ffn_out_bf, i, 0, tm_s)
        if i % 2 == 0:
            lam_init = 0.8 - 0.6 * math.exp(-0.3 * i)
            lq1, lk1, lq2, lk2 = a_lambda[li]
            lam = jnp.exp(jnp.sum(lq1 * lk1)) - jnp.exp(jnp.sum(lq2 * lk2)) + lam_init
            smax = (1.02 * A_DK * A_DK ** -0.5 * LOG2E) * jnp.max(jnp.abs(a_qk_norm[li, 0])) * jnp.max(
                jnp.abs(a_qk_norm[li, 1]))
            shift = smax + jnp.max(bias_l2, axis=0)
            fast = jnp.all(2.0 * smax + jnp.max(bias_l2, axis=0) - jnp.min(bias_l2, axis=0) <= SAFE_SPAN)
            sc = jnp.concatenate([jnp.stack([lam, jnp.asarray(1.0 - lam_init, F32), fast.astype(F32)]),
                                  shift]).astype(F32)

            q, k32, kbf, v32, vt, bqkv, bz, gb = _inproj_even(
                xp, nm, w_even_bf, w_even_small, grp, qgain, kgain, gparams, i, li, ROWS_INPROJ)
            oa = _attn_prompt(sc, q.reshape(bp, tp, A_QK), kbf.reshape(bp, tp, A_QK), vt, bias_p, subln, li, tq)
            bqkv3 = bqkv.reshape(bp, tp, B_CONV_DIM)
            ob, s_new = _delta(bqkv3, bz.reshape(bp, tp, B_V), gb.reshape(bp, tp, LANES),
                               jnp.zeros((bp, SUBLANES, B_CONV_DIM), F32),
                               jnp.zeros((bp, B_HEADS, B_DK, B_DV), F32), b_conv, bnorm, li, SEQ_CHUNK)
            xp = _outproj_even(xp, oa.reshape(bp * tp, A_V), ob.reshape(bp * tp, B_V), w_out_even_bf, li, tm_p)
            k_rows[0].append(k32.reshape(bp, tp, A_HEADS, 2 * A_DK))
            v_rows[0].append(v32.reshape(bp, tp, A_HEADS, A_DV))
            deltas[0].append(s_new)
            convs[0].append(bqkv3[:, tp - (CONV_W - 1):, :])

            q, k32, kbf, v32, _, bqkv, bz, gb = _inproj_even(
                xs, nm, w_even_bf, w_even_small, grp, qgain, kgain, gparams, i, li, tm_s)
            q4 = q.reshape(bs, ts, A_HEADS, 2, A_DK)
            qbd = jnp.einsum('bthcd,ce->bhcted', q4, jnp.eye(2, dtype=BF16)).reshape(bs, A_HEADS * 2 * ts, 2 * A_DK)
            knew = _pad_rows(kbf.reshape(bs, ts * A_HEADS, 2 * A_DK), LANES)
            vnew = _pad_rows(v32.reshape(bs, ts * A_HEADS, A_DV).astype(BF16), LANES)
            oa = _attn_sample(page_table, sc, qbd, bias_s, knew, vnew, subln, cache_k2, cache_v2, li, ts)
            oa = oa[:, :ts, :].astype(BF16)
            bqkv3 = bqkv.reshape(bs, ts, B_CONV_DIM)
            conv0 = jnp.pad(state_conv[li], ((0, 0), (SUBLANES - (CONV_W - 1), 0), (0, 0)))
            ob, s_new = _delta(_pad_rows(bqkv3, cs), _pad_rows(bz.reshape(bs, ts, B_V), cs),
                               _pad_rows(gb.reshape(bs, ts, LANES), cs), conv0, state_delta[li],
                               b_conv, bnorm, li, cs)
            xs = _outproj_even(xs, oa.reshape(bs * ts, A_V), ob[:, :ts, :].reshape(bs * ts, B_V),
                               w_out_even_bf, li, tm_s)
            k_rows[1].append(k32.reshape(bs, ts, A_HEADS, 2 * A_DK))
            v_rows[1].append(v32.reshape(bs, ts, A_HEADS, A_DV))
            deltas[1].append(s_new)
            convs[1].append(jnp.concatenate([state_conv[li], bqkv3], axis=1)[:, ts:, :])
        else:
            q, k, v, r, gk = _inproj_odd(xp, nm, w_odd_bf, w_odd_small, gate_w2, gate_b, i, li, ROWS_INPROJ)
            o, s_new = _gla(q.reshape(bp, tp, C_QK), k.reshape(bp, tp, C_QK), v.reshape(bp, tp, C_V),
                            r.reshape(bp, tp, C_V), gk.reshape(bp, tp, C_QK),
                            jnp.zeros((bp, C_HEADS, C_DK, C_DV), F32), cnorm, li, SEQ_CHUNK)
            xp = _outproj_odd(xp, o.reshape(bp * tp, C_V), w_out_odd_bf, li, tm_p)
            glas[0].append(s_new)

            q, k, v, r, gk = _inproj_odd(xs, nm, w_odd_bf, w_odd_small, gate_w2, gate_b, i, li, tm_s)
            pad = lambda a, n: _pad_rows(a.reshape(bs, ts, n), cs)
            o, s_new = _gla(pad(q, C_QK), pad(k, C_QK), pad(v, C_V), pad(r, C_V), pad(gk, C_QK),
                            state_gla[li], cnorm, li, cs)
            xs = _outproj_odd(xs, o[:, :ts, :].reshape(bs * ts, C_V), w_out_odd_bf, li, tm_s)
            glas[1].append(s_new)
        xp = _ffn(xp, n2, ffn_in_bf, ffn_out_bf, i, 1, tm_p)
        xs = _ffn(xs, n2, ffn_in_bf, ffn_out_bf, i, 1, tm_s)

    return (xp.reshape(bp, tp, d), xs.reshape(bs, ts, d),
            jnp.stack(k_rows[0]), jnp.stack(v_rows[0]), jnp.stack(deltas[0]), jnp.stack(convs[0]),
            jnp.stack(glas[0]),
            jnp.stack(k_rows[1]), jnp.stack(v_rows[1]), jnp.stack(deltas[1]), jnp.stack(convs[1]),
            jnp.stack(glas[1]))
```

```python
import functools
import math

import jax
import jax.numpy as jnp
from jax import lax
from jax.experimental import pallas as pl
from jax.experimental.pallas import tpu as pltpu

F32 = jnp.float32
BF16 = jnp.bfloat16

A_HEADS, A_DK, A_DV = 4, 64, 128
B_HEADS, B_DK, B_DV = 4, 128, 128
CONV_W = 4
C_HEADS, C_DK, C_DV = 4, 128, 256
GLA_LOWRANK, GLA_TAU = 16, 16.0
N_BUCKETS, MAX_DIST = 32, 128
EPS = 1e-6
NEG = -1e30
LOG2E = 1.4426950408889634

A_QK = A_HEADS * 2 * A_DK
A_V = A_HEADS * A_DV
B_QK = B_HEADS * B_DK
B_V = B_HEADS * B_DV
B_CONV_DIM = 2 * B_QK + B_V
C_QK = C_HEADS * C_DK
C_V = C_HEADS * C_DV
IN_EVEN_MAIN = 2 * A_QK + A_V + B_CONV_DIM + B_V
IN_ODD_MAIN = 2 * C_QK + 2 * C_V

LANES = 128
SUBLANES = 8
PACKED_SUBLANES = 16
VMEM_LIMIT = 56 * 2 ** 20

ROWS_DENSE = 512
ROWS_INPROJ = 256
ATT_TILE = 512
VT_ROWS = A_DV + PACKED_SUBLANES
SAFE_SPAN = 96.0
SEQ_CHUNK = 64
DELTA_PREP_CHUNKS = 4
GLA_SUB = 16
PAGES_PER_STEP = 16

NT = (((1,), (1,)), ((), ()))
TN = (((0,), (0,)), ((), ()))


def _cp(*sem):
    return pltpu.CompilerParams(dimension_semantics=sem, vmem_limit_bytes=VMEM_LIMIT)


def _mm(a, b, dn=None):
    if dn is None:
        return jnp.dot(a, b, preferred_element_type=F32)
    return lax.dot_general(a, b, dn, preferred_element_type=F32)


def _split2(x):
    hi = x.astype(BF16)
    lo = (x - hi.astype(F32)).astype(BF16)
    return hi, lo


def _split3(x):
    hi = x.astype(BF16)
    r = x - hi.astype(F32)
    mid = r.astype(BF16)
    lo = (r - mid.astype(F32)).astype(BF16)
    return hi, mid, lo


def _mm3(a, b):
    ah, al = _split2(a)
    bh, bl = _split2(b)
    return _mm(ah, bh) + (_mm(ah, bl) + _mm(al, bh))


def _mm_exact_lhs(a_bf, x):
    hi, mid, lo = _split3(x)
    return _mm(a_bf, hi) + (_mm(a_bf, mid) + _mm(a_bf, lo))


def _sigmoid(x):
    return 1.0 / (1.0 + jnp.exp(-x))


def _silu(x):
    return x * _sigmoid(x)


def _softplus(x):
    return jnp.maximum(x, 0.0) + jnp.log(1.0 + jnp.exp(-jnp.abs(x)))


def _rmsnorm(xf, g):
    return xf * lax.rsqrt(jnp.mean(xf * xf, axis=-1, keepdims=True) + EPS) * g


def _iota(shape, dim):
    return lax.broadcasted_iota(jnp.int32, shape, dim)


def _row_to_col(row, n):
    eye = _iota((n, n), 0) == _iota((n, n), 1)
    return jnp.sum(jnp.where(eye, jnp.broadcast_to(row, (n, n)), 0.0), axis=1, keepdims=True)


def _ffn_kernel(x_ref, g_ref, wg_ref, wu_ref, wo_ref, *rest):
    *mix_refs, o_ref, h_ref = rest
    j = pl.program_id(1)

    @pl.when(j == 0)
    def _():
        xf = x_ref[...]
        if mix_refs:
            wm_ref = mix_refs[-1]
            off = 0
            for m_ref in mix_refs[:-1]:
                n = m_ref.shape[1]
                xf = xf + _mm(m_ref[...], wm_ref[off:off + n, :])
                off += n
        h_ref[...] = _rmsnorm(xf, g_ref[...]).astype(BF16)
        o_ref[...] = xf

    h = h_ref[...]
    gate = _mm(h, wg_ref[...])
    up = _mm(h, wu_ref[...])
    a = (_silu(gate) * up).astype(BF16)
    o_ref[...] += 0.5 * _mm(a, wo_ref[...])


def _ffn(x, g, w_in, w_out, layer, which, tm, mix=(), w_mix=None, li=0):
    m, d = x.shape
    f = w_out.shape[2]
    nf = 2
    tf = f // nf
    assert f % nf == 0 and tf % LANES == 0 and m % tm == 0
    mix_specs = [pl.BlockSpec((tm, o.shape[1]), lambda i, j: (i, 0)) for o in mix]
    if mix:
        assert sum(o.shape[1] for o in mix) == w_mix.shape[1]
        mix_specs.append(pl.BlockSpec((None, w_mix.shape[1], d), lambda i, j: (li, 0, 0)))
    return pl.pallas_call(
        _ffn_kernel,
        grid=(m // tm, nf),
        in_specs=[
            pl.BlockSpec((tm, d), lambda i, j: (i, 0)),
            pl.BlockSpec((None, 1, d), lambda i, j: (layer, 0, 0)),
            pl.BlockSpec((None, None, d, tf), lambda i, j: (layer, which, 0, j)),
            pl.BlockSpec((None, None, d, tf), lambda i, j: (layer, which, 0, j + nf)),
            pl.BlockSpec((None, None, tf, d), lambda i, j: (layer, which, j, 0)),
        ] + mix_specs,
        out_specs=pl.BlockSpec((tm, d), lambda i, j: (i, 0)),
        out_shape=jax.ShapeDtypeStruct((m, d), F32),
        scratch_shapes=[pltpu.VMEM((tm, d), BF16)],
        compiler_params=_cp("parallel", "arbitrary"),
        name="ffn",
    )(x, g, w_in, w_in, w_out, *mix, *([w_mix] if mix else []))


def _group_rms(x, grp_bf, gain, width):
    hi, lo = _split2(x * x)
    ms = (_mm(hi, grp_bf) + _mm(lo, grp_bf)) * (1.0 / width)
    return x * lax.rsqrt(ms + EPS) * gain


def _inproj_even_kernel(x_ref, g_ref, w_ref, ws_ref, grp_ref, qg_ref, kg_ref, gp_ref,
                        q_ref, k32_ref, kbf_ref, v32_ref, vt_ref, bqkv_ref, bz_ref, gb_ref):
    h = _rmsnorm(x_ref[...], g_ref[...]).astype(BF16)
    grp = grp_ref[...]

    aq = _mm(h, w_ref[:, 0:A_QK])
    q = _group_rms(aq, grp, qg_ref[...], A_DK)
    q_ref[...] = (q * (A_DK ** -0.5 * LOG2E)).astype(BF16)

    ak = _mm(h, w_ref[:, A_QK:2 * A_QK])
    k = _group_rms(ak, grp, kg_ref[...], A_DK)
    kbf_ref[...] = k.astype(BF16)

    v = _mm(h, w_ref[:, 2 * A_QK:2 * A_QK + A_V])
    tm = v.shape[0]
    ones = jnp.ones((VT_ROWS - A_DV, tm), BF16)
    for hd in range(A_HEADS):
        k32_ref[pl.ds(hd, tm, stride=A_HEADS), :] = k[:, 2 * A_DK * hd:2 * A_DK * (hd + 1)]
        v32_ref[pl.ds(hd, tm, stride=A_HEADS), :] = v[:, A_DV * hd:A_DV * (hd + 1)]
        vt_ref[hd, 0:A_DV, :] = v[:, A_DV * hd:A_DV * (hd + 1)].T.astype(BF16)
        vt_ref[hd, A_DV:VT_ROWS, :] = ones

    o = 2 * A_QK + A_V
    bqkv_ref[...] = _mm(h, w_ref[:, o:o + B_CONV_DIM])
    o += B_CONV_DIM
    bz_ref[...] = _mm(h, w_ref[:, o:o + B_V])

    y = _mm(h, ws_ref[...])
    lane = _iota(y.shape, 1)
    g = gp_ref[0:1, :] * _softplus(y + gp_ref[1:2, :])
    gb_ref[...] = jnp.where(lane < B_HEADS, g, jnp.where(lane < 2 * B_HEADS, _sigmoid(y), 0.0))


def _inproj_even(x, g, w, ws, grp, qg, kg, gp, layer, li, tm):
    m, d = x.shape
    wmain = IN_EVEN_MAIN
    row = lambda n: pl.BlockSpec((tm, n), lambda i: (i, 0))
    out_shapes = [
        jax.ShapeDtypeStruct((m, A_QK), BF16),
        jax.ShapeDtypeStruct((m * A_HEADS, 2 * A_DK), F32),
        jax.ShapeDtypeStruct((m, A_QK), BF16),
        jax.ShapeDtypeStruct((m * A_HEADS, A_DV), F32),
        jax.ShapeDtypeStruct((A_HEADS, m // tm, VT_ROWS, tm), BF16),
        jax.ShapeDtypeStruct((m, B_CONV_DIM), F32),
        jax.ShapeDtypeStruct((m, B_V), F32),
        jax.ShapeDtypeStruct((m, LANES), F32),
    ]
    out_specs = [row(s.shape[1]) for s in out_shapes]
    out_specs[1] = pl.BlockSpec((tm * A_HEADS, 2 * A_DK), lambda i: (i, 0))
    out_specs[3] = pl.BlockSpec((tm * A_HEADS, A_DV), lambda i: (i, 0))
    out_specs[4] = pl.BlockSpec((A_HEADS, None, VT_ROWS, tm), lambda i: (0, i, 0, 0))
    return pl.pallas_call(
        _inproj_even_kernel,
        grid=(m // tm,),
        in_specs=[
            row(d),
            pl.BlockSpec((None, 1, d), lambda i: (layer, 0, 0)),
            pl.BlockSpec((None, d, wmain), lambda i: (li, 0, 0)),
            pl.BlockSpec((None, d, LANES), lambda i: (li, 0, 0)),
            pl.BlockSpec((A_QK, A_QK), lambda i: (0, 0)),
            pl.BlockSpec((None, 1, A_QK), lambda i: (li, 0, 0)),
            pl.BlockSpec((None, 1, A_QK), lambda i: (li, 0, 0)),
            pl.BlockSpec((None, 2, LANES), lambda i: (li, 0, 0)),
        ],
        out_specs=out_specs,
        out_shape=out_shapes,
        compiler_params=_cp("parallel"),
        name="inproj_even",
    )(x, g, w, ws, grp, qg, kg, gp)


def _bias_kernel(rb_ref, bp_ref, bs_ref, *, tq, past, tnew):
    h = pl.program_id(0)
    far = rb_ref[N_BUCKETS - 1, h]
    exact = N_BUCKETS // 2

    def table(rel):
        n = jnp.maximum(rel, 0)
        nf = jnp.maximum(n, 1).astype(F32)
        large = exact + (jnp.log(nf / exact) / math.log(MAX_DIST / exact) * (N_BUCKETS - exact)).astype(jnp.int32)
        bucket = jnp.where(n < exact, n, jnp.minimum(large, N_BUCKETS - 1))
        val = lax.fori_loop(0, N_BUCKETS, lambda b, acc: jnp.where(bucket == b, rb_ref[b, h], acc),
                            jnp.zeros(rel.shape, F32))
        return jnp.where(rel >= 0, (val - far) * LOG2E, NEG)

    key_p = _iota((tq, tq), 0)
    qry_p = _iota((tq, tq), 1)
    for d in range(2):
        bp_ref[d] = table(d * tq + qry_p - key_p)

    ncol = past * A_HEADS + LANES
    t_idx = _iota((2 * tnew, ncol), 0) & (tnew - 1)
    col = _iota((2 * tnew, ncol), 1)
    key = col >> 2
    rel_new = jnp.where(key - past < tnew, t_idx - (key - past), -1)
    rel = jnp.where(key < past, past + t_idx - key, rel_new)
    bs_ref[...] = table(jnp.where((col & (A_HEADS - 1)) == h, rel, -1))


def _bias_tiles(rel_bias, tq, past, tnew):
    assert tnew & (tnew - 1) == 0 and 2 * tnew == SUBLANES and A_HEADS == 4
    ncol = past * A_HEADS + LANES
    return pl.pallas_call(
        functools.partial(_bias_kernel, tq=tq, past=past, tnew=tnew),
        grid=(A_HEADS,),
        in_specs=[pl.BlockSpec(memory_space=pltpu.SMEM)],
        out_specs=[
            pl.BlockSpec((None, 2, tq, tq), lambda h: (h, 0, 0, 0)),
            pl.BlockSpec((None, 2 * tnew, ncol), lambda h: (h, 0, 0)),
        ],
        out_shape=[
            jax.ShapeDtypeStruct((A_HEADS, 2, tq, tq), F32),
            jax.ShapeDtypeStruct((A_HEADS, 2 * tnew, ncol), F32),
        ],
        compiler_params=_cp("parallel"),
        name="t5_bias",
    )(rel_bias)


def _far_bucket_saturated(dist):
    exact = N_BUCKETS // 2
    v = math.log(dist / exact) / math.log(MAX_DIST / exact) * (N_BUCKETS - exact)
    return dist >= exact and exact + v >= N_BUCKETS - 1 + 1e-3


def _attn_prompt_kernel(sc_ref, q_ref, k_ref, vt_ref, bias_ref, sg_ref, o_ref, *, tq, vblk):
    head = pl.program_id(1)
    i = pl.program_id(2)
    q = q_ref[...]
    lane = _iota(q.shape, 1)
    zero = jnp.zeros_like(q)
    qbd = jnp.concatenate([jnp.where(lane < A_DK, q, zero), jnp.where(lane >= A_DK, q, zero)], axis=0)
    nsub = tq // vblk

    def scores(j):
        start = pl.multiple_of(j * tq, tq)
        return _mm(k_ref[pl.ds(start, tq), :], qbd, NT)

    def biased(j, bias):
        s = scores(j)
        return s if bias is None else s + jnp.concatenate([bias, bias], axis=1)

    def weighted_values(j, p):
        pv = _mm(vt_ref[j * nsub], p[0:vblk])
        for u in range(1, nsub):
            pv = pv + _mm(vt_ref[j * nsub + u], p[u * vblk:(u + 1) * vblk])
        return pv

    def key_blocks(step, carry):
        nfar = jnp.maximum(i - 1, 0)
        carry = lax.fori_loop(0, nfar // 2, lambda j, c: step(2 * j + 1, step(2 * j, c, None), None), carry)
        carry = lax.cond(nfar % 2 == 1, lambda c: step(nfar - 1, c, None), lambda c: c, carry)
        return lax.cond(i >= 1, lambda c: step(i, step(i - 1, c, bias_ref[1]), bias_ref[0]),
                        lambda c: step(i, c, bias_ref[0]), carry)

    def fixed_shift(_):
        shift = sc_ref[3 + head]

        def step(j, acc, bias):
            return acc + weighted_values(j, jnp.exp2(biased(j, bias) - shift).astype(BF16))

        return key_blocks(step, jnp.zeros((VT_ROWS, 2 * tq), F32))

    def running_max(_):
        def step(j, c, bias):
            m, acc = c
            s = biased(j, bias)
            m_new = jnp.maximum(m, jnp.max(s, axis=0, keepdims=True))
            p = jnp.exp2(s - m_new).astype(BF16)
            return m_new, jnp.exp2(m - m_new) * acc + weighted_values(j, p)

        return key_blocks(step, (jnp.full((1, 2 * tq), NEG, F32), jnp.zeros((VT_ROWS, 2 * tq), F32)))[1]

    acc = lax.cond(sc_ref[2] > 0.5, fixed_shift, running_max, 0)

    o = acc[0:A_DV] / acc[A_DV:A_DV + 1]
    o = o[:, 0:tq] - sc_ref[0] * o[:, tq:2 * tq]
    o = o * lax.rsqrt(jnp.mean(o * o, axis=0, keepdims=True) + EPS)
    o_ref[...] = (o.T * (sg_ref[...] * sc_ref[1])).astype(BF16)


def _attn_prompt(sc, q, k, vt, bias_p, sg, li, tq):
    bsz, t, _ = q.shape
    vblk = vt.shape[3]
    assert tq % vblk == 0 and t % tq == 0
    return pl.pallas_call(
        functools.partial(_attn_prompt_kernel, tq=tq, vblk=vblk),
        grid=(bsz, A_HEADS, t // tq),
        in_specs=[
            pl.BlockSpec(memory_space=pltpu.SMEM),
            pl.BlockSpec((None, tq, 2 * A_DK), lambda b, h, i: (b, i, h)),
            pl.BlockSpec((None, t, 2 * A_DK), lambda b, h, i: (b, 0, h)),
            pl.BlockSpec((None, t // vblk, VT_ROWS, vblk), lambda b, h, i: (h, b, 0, 0)),
            pl.BlockSpec((None, 2, tq, tq), lambda b, h, i: (h, 0, 0, 0)),
            pl.BlockSpec((None, 1, A_DV), lambda b, h, i: (li, 0, 0)),
        ],
        out_specs=pl.BlockSpec((None, tq, A_DV), lambda b, h, i: (b, i, h)),
        out_shape=jax.ShapeDtypeStruct((bsz, t, A_V), BF16),
        compiler_params=_cp("parallel", "parallel", "arbitrary"),
        name="attn_prompt",
    )(sc, q, k, vt, bias_p, sg)


def _attn_sample_kernel(pt_ref, sc_ref, qbd_ref, bias_ref, biasn_ref, kn_ref, vn_ref, sg_ref, *rest, npg, tnew):
    k_refs = rest[:npg]
    v_refs = rest[npg:2 * npg]
    o_ref, m_ref, l_ref, acc_ref = rest[2 * npg:]
    pg = pl.program_id(1)
    qbd = qbd_ref[...]

    @pl.when(pg == 0)
    def _():
        m_ref[...] = jnp.full(m_ref.shape, NEG, F32)
        l_ref[...] = jnp.zeros(l_ref.shape, F32)
        acc_ref[...] = jnp.zeros(acc_ref.shape, F32)

    def update(s, vs):
        m_old = m_ref[...]
        m_new = jnp.maximum(m_old, jnp.max(s, axis=1, keepdims=True))
        alpha = jnp.exp2(m_old - m_new)
        p = jnp.exp2(s - m_new)
        l_ref[...] = alpha * l_ref[...] + jnp.sum(p, axis=1, keepdims=True)
        pb = p.astype(BF16)
        w = pb.shape[1] // len(vs)
        pv = _mm(pb[:, 0:w], vs[0])
        for g in range(1, len(vs)):
            pv = pv + _mm(pb[:, g * w:(g + 1) * w], vs[g])
        acc_ref[...] = alpha * acc_ref[...] + pv
        m_ref[...] = m_new

    s = jnp.concatenate([_mm(qbd, kr[...].astype(BF16), NT) for kr in k_refs], axis=1) + bias_ref[...]
    update(s, [vr[...].astype(BF16) for vr in v_refs])

    @pl.when(pg == pl.num_programs(1) - 1)
    def _():
        update(_mm(qbd, kn_ref[...], NT) + biasn_ref[...], [vn_ref[...]])
        lam = sc_ref[0]
        rows = 2 * tnew
        for h in range(A_HEADS):
            a = acc_ref[rows * h:rows * (h + 1), :] / l_ref[rows * h:rows * (h + 1), :]
            o = a - lam * pltpu.roll(a, tnew, axis=0)
            o_ref[:, A_DV * h:A_DV * (h + 1)] = _rmsnorm(o, sg_ref[...]) * sc_ref[1]


def _attn_sample(page_table, sc, qbd, bias_s, knew, vnew, sg, cache_k, cache_v, li, tnew):
    bsz, nrow, _ = qbd.shape
    n_pages = page_table.shape[1]
    npg = PAGES_PER_STEP
    assert n_pages % npg == 0
    prow = cache_k.shape[2]
    past_cols = n_pages * prow

    def kvspec(g):
        return pl.BlockSpec((None, None, prow, 2 * A_DK), lambda b, pg, pt: (li, pt[b, pg * npg + g], 0, 0))

    grid_spec = pltpu.PrefetchScalarGridSpec(
        num_scalar_prefetch=1,
        grid=(bsz, n_pages // npg),
        in_specs=[
            pl.BlockSpec(memory_space=pltpu.SMEM),
            pl.BlockSpec((None, nrow, 2 * A_DK), lambda b, pg, pt: (b, 0, 0)),
            pl.BlockSpec((nrow, npg * prow), lambda b, pg, pt: (0, pg)),
            pl.BlockSpec((nrow, LANES), lambda b, pg, pt: (0, past_cols // LANES)),
            pl.BlockSpec((None, LANES, 2 * A_DK), lambda b, pg, pt: (b, 0, 0)),
            pl.BlockSpec((None, LANES, A_DV), lambda b, pg, pt: (b, 0, 0)),
            pl.BlockSpec((None, 1, A_DV), lambda b, pg, pt: (li, 0, 0)),
        ] + [kvspec(g) for g in range(npg)] + [kvspec(g) for g in range(npg)],
        out_specs=pl.BlockSpec((None, 2 * tnew, A_V), lambda b, pg, pt: (b, 0, 0)),
        scratch_shapes=[pltpu.VMEM((nrow, 1), F32), pltpu.VMEM((nrow, 1), F32), pltpu.VMEM((nrow, A_DV), F32)],
    )
    return pl.pallas_call(
        functools.partial(_attn_sample_kernel, npg=npg, tnew=tnew),
        grid_spec=grid_spec,
        out_shape=jax.ShapeDtypeStruct((bsz, 2 * tnew, A_V), F32),
        compiler_params=_cp("parallel", "arbitrary"),
        name="attn_sample",
    )(page_table, sc, qbd, bias_s, bias_s, knew, vnew, sg, *([cache_k] * npg), *([cache_v] * npg))


def _l2norm(x):
    return x * lax.rsqrt(jnp.sum(x * x, axis=-1, keepdims=True) + EPS)


def _unit_lower_inverses(mats, c):
    n = mats[0].shape[0]
    nblk = n // c
    shift = c.bit_length() - 1
    same = (_iota((n, n), 0) >> shift) == (_iota((n, n), 1) >> shift)
    eye_w = ((_iota((c, n), 1) & (c - 1)) == _iota((c, n), 0)).astype(F32)

    def wide(sq):
        out = sq[0:c]
        for r in range(1, nblk):
            out = out + sq[r * c:(r + 1) * c]
        return out

    def square(w):
        return jnp.where(same, jnp.concatenate([w] * nblk, axis=0), jnp.zeros((), w.dtype))

    def mm3_split(lhs_split, rhs_split):
        (lh, ll), (rh, rl) = lhs_split, rhs_split
        return _mm(lh, rh) + (_mm(lh, rl) + _mm(ll, rh))

    qw = [_split2(wide(a)) for a in mats]
    qsq = [(square(h), square(l)) for h, l in qw]
    ps = [eye_w - wide(a) for a in mats]
    span = 2
    while span < c:
        qw = [_split2(mm3_split(q, s)) for q, s in zip(qw, qsq)]
        qsq = [(square(h), square(l)) for h, l in qw]
        ps = [p + mm3_split(_split2(p), s) for p, s in zip(ps, qsq)]
        span *= 2
    return [square(p) for p in ps]


def _cumsum_rows(x, n):
    row = _iota(x.shape, 0)
    k = 1
    while k < n:
        x = x + jnp.where(row >= k, pltpu.roll(x, k, axis=0), 0.0)
        k *= 2
    return x


def _delta_prep_kernel(x_ref, halo_ref, c0_ref, gb_ref, cw_ref, u_ref, wk_ref, qe_ref, kd_ref, qk_ref, xbuf,
                       *, c, nck):
    ci = pl.program_id(1)
    ch = B_HEADS * c
    rows = nck * c
    xbuf[0:SUBLANES, :] = jnp.where(ci == 0, c0_ref[...], halo_ref[...])
    xbuf[SUBLANES:SUBLANES + rows, :] = x_ref[...]
    first = SUBLANES - (CONV_W - 1)
    conv = xbuf[pl.ds(first, rows), :] * cw_ref[0:1, :]
    for j in range(1, CONV_W):
        conv = conv + xbuf[pl.ds(first + j, rows), :] * cw_ref[j:j + 1, :]
    conv = _silu(conv)

    row = _iota((ch, ch), 0)
    col = _iota((ch, ch), 1)
    shift = c.bit_length() - 1
    same = (row >> shift) == (col >> shift)
    incl = same & (row >= col)
    strict = same & (row > col)

    def stack_lane(a, off):
        return jnp.concatenate([a[:, off + h:off + h + 1] for h in range(B_HEADS)], axis=0)

    parts = []
    for n in range(nck):
        rs = slice(n * c, (n + 1) * c)

        def stack(off, width):
            return jnp.concatenate([conv[rs, off + width * h:off + width * (h + 1)] for h in range(B_HEADS)], axis=0)

        q = _l2norm(stack(0, B_DK)) * (B_DK ** -0.5)
        k = _l2norm(stack(B_QK, B_DK))
        v = stack(2 * B_QK, B_DV)
        gb = gb_ref[rs, :]
        gcum = _cumsum_rows(gb, c)
        gcol = stack_lane(gcum, 0)
        beta = stack_lane(gb, B_HEADS)
        glast = stack_lane(jnp.broadcast_to(gcum[c - 1:c, :], gcum.shape), 0)
        grow = jnp.broadcast_to(gcol, (ch, LANES)).T[0:1, :]
        dec = jnp.where(incl, jnp.exp(jnp.where(incl, gcol - grow, 0.0)), 0.0)
        kb = k.astype(BF16)
        a = jnp.where(strict, _mm(kb, kb, NT) * dec * beta, 0.0)
        egc = jnp.exp(gcol)
        qe_ref[n] = (q * egc).astype(BF16)
        kd_ref[n] = (k * jnp.exp(glast - gcol)).astype(BF16)
        qk_ref[n] = (_mm(q.astype(BF16), kb, NT) * dec).astype(BF16)
        parts.append((a, jnp.concatenate([v * beta, k * (beta * egc)], axis=1)))

    t_invs = _unit_lower_inverses([a for a, _ in parts], c)
    for n in range(nck):
        uw = _mm3(t_invs[n], parts[n][1])
        u_ref[n] = uw[:, 0:B_DV]
        wk_ref[n] = uw[:, B_DV:B_DV + B_DK].astype(BF16)


def _delta_scan_kernel(u_ref, wk_ref, qe_ref, kd_ref, qk_ref, gb_ref, z_ref, s0_ref, bn_ref, ob_ref, sout_ref,
                       s_scr, *, c, nb):
    ci = pl.program_id(1)

    @pl.when(ci == 0)
    def _():
        s_scr[...] = s0_ref[...]

    hs = [slice(c * h, c * (h + 1)) for h in range(B_HEADS)]
    for b in range(nb):
        decay = jnp.exp(jnp.sum(gb_ref[b], axis=0, keepdims=True))
        s = [s_scr[b, h] for h in range(B_HEADS)]
        s_bf = [x.astype(BF16) for x in s]
        v_new = u_ref[b] - jnp.concatenate([_mm(wk_ref[b, hs[h], :], s_bf[h]) for h in range(B_HEADS)], axis=0)
        vn_bf = v_new.astype(BF16)
        o = (jnp.concatenate([_mm(qe_ref[b, hs[h], :], s_bf[h]) for h in range(B_HEADS)], axis=0)
             + _mm(qk_ref[b], vn_bf))
        for h in range(B_HEADS):
            s_scr[b, h] = s[h] * decay[:, h:h + 1] + _mm(kd_ref[b, hs[h], :], vn_bf[hs[h]], TN)
            sl = slice(B_DV * h, B_DV * (h + 1))
            ob_ref[b, :, sl] = (_rmsnorm(o[hs[h]], bn_ref[...]) * _silu(z_ref[b, :, sl])).astype(BF16)

    @pl.when(ci == pl.num_programs(1) - 1)
    def _():
        sout_ref[...] = s_scr[...]


def _delta(bqkv, bz, gb, conv0, s0, cw, bn, li, c):
    bsz, t, _ = bqkv.shape
    assert t % c == 0 and c % SUBLANES == 0 and c & (c - 1) == 0
    nch = t // c
    ch = B_HEADS * c
    nck = min(DELTA_PREP_CHUNKS, nch)
    assert nch % nck == 0
    hb = nck * c // SUBLANES
    tok = lambda n: pl.BlockSpec((None, c, n), lambda b, ci: (b, ci, 0))
    step_tok = lambda n: pl.BlockSpec((None, nck * c, n), lambda b, ci: (b, ci, 0))
    step_chunks = lambda n: pl.BlockSpec((None, nck, ch, n), lambda b, ci: (b, ci, 0, 0))
    chunk_shape = lambda n, dt: jax.ShapeDtypeStruct((bsz, nch, ch, n), dt)
    u, wk, qe, kd, qk = pl.pallas_call(
        functools.partial(_delta_prep_kernel, c=c, nck=nck),
        grid=(bsz, nch // nck),
        in_specs=[
            step_tok(B_CONV_DIM),
            pl.BlockSpec((None, SUBLANES, B_CONV_DIM), lambda b, ci: (b, jnp.maximum(ci * hb - 1, 0), 0)),
            pl.BlockSpec((None, SUBLANES, B_CONV_DIM), lambda b, ci: (b, 0, 0)),
            step_tok(LANES),
            pl.BlockSpec((None, CONV_W, B_CONV_DIM), lambda b, ci: (li, 0, 0)),
        ],
        out_specs=[step_chunks(B_DV), step_chunks(B_DK), step_chunks(B_DK), step_chunks(B_DK), step_chunks(ch)],
        out_shape=[chunk_shape(B_DV, F32), chunk_shape(B_DK, BF16), chunk_shape(B_DK, BF16),
                   chunk_shape(B_DK, BF16), chunk_shape(ch, BF16)],
        scratch_shapes=[pltpu.VMEM((SUBLANES + nck * c, B_CONV_DIM), F32)],
        compiler_params=_cp("parallel", "parallel"),
        name="delta_prep",
    )(bqkv, bqkv, conv0, gb, cw)

    nb = 2
    assert bsz % nb == 0
    grp_chunk = lambda n: pl.BlockSpec((nb, None, ch, n), lambda b, ci: (b, ci, 0, 0))
    grp_tok = lambda n: pl.BlockSpec((nb, c, n), lambda b, ci: (b, ci, 0))
    state = pl.BlockSpec((nb, B_HEADS, B_DK, B_DV), lambda b, ci: (b, 0, 0, 0))
    return pl.pallas_call(
        functools.partial(_delta_scan_kernel, c=c, nb=nb),
        grid=(bsz // nb, nch),
        in_specs=[grp_chunk(B_DV), grp_chunk(B_DK), grp_chunk(B_DK), grp_chunk(B_DK), grp_chunk(ch),
                  grp_tok(LANES), grp_tok(B_V), state,
                  pl.BlockSpec((None, 1, B_DV), lambda b, ci: (li, 0, 0))],
        out_specs=[grp_tok(B_V), state],
        out_shape=[jax.ShapeDtypeStruct((bsz, t, B_V), BF16),
                   jax.ShapeDtypeStruct((bsz, B_HEADS, B_DK, B_DV), F32)],
        scratch_shapes=[pltpu.VMEM((nb, B_HEADS, B_DK, B_DV), F32)],
        compiler_params=_cp("parallel", "arbitrary"),
        name="delta_scan",
    )(u, wk, qe, kd, qk, gb, bz, s0, bn)


def _inproj_odd_kernel(x_ref, g_ref, w_ref, ws_ref, w2_ref, gbias_ref, q_ref, k_ref, v_ref, r_ref, gk_ref):
    h = _rmsnorm(x_ref[...], g_ref[...]).astype(BF16)
    q_ref[...] = _mm(h, w_ref[:, 0:C_QK])
    k_ref[...] = _mm(h, w_ref[:, C_QK:2 * C_QK])
    v_ref[...] = _mm(h, w_ref[:, 2 * C_QK:2 * C_QK + C_V]).astype(BF16)
    r_ref[...] = _mm(h, w_ref[:, 2 * C_QK + C_V:2 * C_QK + 2 * C_V])
    glr = _mm(h, ws_ref[...]).astype(BF16)
    y = _mm(glr, w2_ref[...]) + gbias_ref[...]
    gk_ref[...] = -_softplus(-y) * (1.0 / GLA_TAU)


def _inproj_odd(x, g, w, ws, w2, gbias, layer, li, tm):
    m, d = x.shape
    wmain = 2 * C_QK + 2 * C_V
    row = lambda n: pl.BlockSpec((tm, n), lambda i: (i, 0))
    out_shapes = [
        jax.ShapeDtypeStruct((m, C_QK), F32),
        jax.ShapeDtypeStruct((m, C_QK), F32),
        jax.ShapeDtypeStruct((m, C_V), BF16),
        jax.ShapeDtypeStruct((m, C_V), F32),
        jax.ShapeDtypeStruct((m, C_QK), F32),
    ]
    return pl.pallas_call(
        _inproj_odd_kernel,
        grid=(m // tm,),
        in_specs=[
            row(d),
            pl.BlockSpec((None, 1, d), lambda i: (layer, 0, 0)),
            pl.BlockSpec((None, d, wmain), lambda i: (li, 0, 0)),
            pl.BlockSpec((None, d, LANES), lambda i: (li, 0, 0)),
            pl.BlockSpec((None, LANES, C_QK), lambda i: (li, 0, 0)),
            pl.BlockSpec((None, 1, C_QK), lambda i: (li, 0, 0)),
        ],
        out_specs=[row(s.shape[1]) for s in out_shapes],
        out_shape=out_shapes,
        compiler_params=_cp("parallel"),
        name="inproj_odd",
    )(x, g, w, ws, w2, gbias)


def _gla_kernel(q_ref, k_ref, v_ref, r_ref, gk_ref, s0_ref, cn_ref, o_ref, sout_ref, s_scr, *, c, sub):
    ci = pl.program_id(1)

    @pl.when(ci == 0)
    def _():
        s_scr[...] = s0_ref[...]

    row = _iota((c, c), 0)
    col = _iota((c, c), 1)
    lower_bf = (row >= col).astype(BF16)
    in_sub = row & (sub - 1)
    sub_start = row - in_sub
    b_all = _mm_exact_lhs(lower_bf, gk_ref[...])

    for h in range(C_HEADS):
        sl = slice(C_DK * h, C_DK * (h + 1))
        vl = slice(C_DV * h, C_DV * (h + 1))
        q = q_ref[:, sl] * (C_DK ** -0.5)
        k = k_ref[:, sl]
        b = b_all[:, sl]
        v = v_ref[:, vl]
        s = s_scr[h]

        o = _mm((q * jnp.exp(b)).astype(BF16), s.astype(BF16))

        blocks = [jnp.zeros((sub, c), F32)]
        for r in range(1, c // sub):
            ref = b[sub * r:sub * r + 1, :]
            qt = q[sub * r:sub * (r + 1)] * jnp.exp(b[sub * r:sub * (r + 1)] - ref)
            kt = k * jnp.exp(jnp.minimum(ref - b, 0.0))
            blocks.append(_mm(qt.astype(BF16), kt.astype(BF16), NT))
        a = jnp.where(col < sub_start, jnp.concatenate(blocks, axis=0), 0.0)
        b2 = b * LOG2E
        for d in range(sub):
            kd = k if d == 0 else pltpu.roll(k, d, axis=0)
            bd = b2 if d == 0 else pltpu.roll(b2, d, axis=0)
            ad = jnp.sum(q * kd * jnp.exp2(b2 - bd), axis=1, keepdims=True)
            a = jnp.where((col == row - d) & (in_sub >= d), ad, a)
        o = o + _mm(a.astype(BF16), v)

        blast = b[c - 1:c, :]
        kdec = (k * jnp.exp(blast - b)).astype(BF16)
        s_scr[h] = s * _row_to_col(jnp.exp(blast), C_DK) + _mm(kdec, v, TN)

        o_ref[:, vl] = (_rmsnorm(o, cn_ref[...]) * _silu(r_ref[:, vl])).astype(BF16)

    @pl.when(ci == pl.num_programs(1) - 1)
    def _():
        sout_ref[...] = s_scr[...]


def _gla(q, k, v, r, gk, s0, cn, li, c):
    bsz, t, _ = q.shape
    assert t % c == 0
    sub = min(GLA_SUB, c)
    tok = lambda n: pl.BlockSpec((None, c, n), lambda b, ci: (b, ci, 0))
    state = pl.BlockSpec((None, C_HEADS, C_DK, C_DV), lambda b, ci: (b, 0, 0, 0))
    return pl.pallas_call(
        functools.partial(_gla_kernel, c=c, sub=sub),
        grid=(bsz, t // c),
        in_specs=[tok(C_QK), tok(C_QK), tok(C_V), tok(C_V), tok(C_QK), state,
                  pl.BlockSpec((None, 1, C_DV), lambda b, ci: (li, 0, 0))],
        out_specs=[tok(C_V), state],
        out_shape=[jax.ShapeDtypeStruct((bsz, t, C_V), BF16),
                   jax.ShapeDtypeStruct((bsz, C_HEADS, C_DK, C_DV), F32)],
        scratch_shapes=[pltpu.VMEM((C_HEADS, C_DK, C_DV), F32)],
        compiler_params=_cp("parallel", "arbitrary"),
        name="gla",
    )(q, k, v, r, gk, s0, cn)


def _pad_rows(a, rows):
    return jnp.pad(a, ((0, 0), (0, rows - a.shape[1]), (0, 0)))


def _pad_lanes(a):
    return jnp.pad(a, [(0, 0)] * (a.ndim - 1) + [(0, LANES - a.shape[-1])])


def kernel(x_prompt, x_sample, cache_k, cache_v, state_delta, state_conv, state_gla, page_table, rel_bias, norm_ffn1, norm_mix, norm_ffn2, ffn_in, ffn_out, w_in_even, a_qk_norm, a_lambda, a_subln, b_conv, b_a_log, b_dt_bias, b_norm, w_out_even, w_in_odd, c_gate_w2, c_gate_b, c_norm, w_out_odd):
    bp, tp, d = x_prompt.shape
    bs, ts, _ = x_sample.shape
    depth = norm_ffn1.shape[0]
    n_att = w_in_even.shape[0]
    n_pool, page = cache_k.shape[1], cache_k.shape[2]
    past = page_table.shape[1] * page
    tq = ATT_TILE
    cs = SUBLANES
    assert page == LANES and tp % tq == 0 and ts <= cs and ts >= CONV_W - 1
    assert _far_bucket_saturated(tq + 1) and _far_bucket_saturated(page + 1)

    ffn_in_bf = ffn_in.astype(BF16)
    ffn_out_bf = ffn_out.astype(BF16)
    w_even_bf = w_in_even.astype(BF16)
    w_even_small = _pad_lanes(w_in_even[:, :, IN_EVEN_MAIN:]).astype(BF16)
    w_out_even_bf = w_out_even.astype(BF16)
    w_odd_bf = w_in_odd.astype(BF16)
    w_odd_small = _pad_lanes(w_in_odd[:, :, IN_ODD_MAIN:]).astype(BF16)
    w_out_odd_bf = w_out_odd.astype(BF16)
    gate_w2 = jnp.pad(c_gate_w2, ((0, 0), (0, LANES - GLA_LOWRANK), (0, 0))).astype(BF16)
    gate_b = c_gate_b[:, None, :]
    n1 = norm_ffn1[:, None, :]
    nm = norm_mix[:, None, :]
    n2 = norm_ffn2[:, None, :]
    grp = jnp.kron(jnp.eye(2 * A_HEADS, dtype=F32), jnp.ones((A_DK, A_DK), F32)).astype(BF16)
    qgain = jnp.tile(a_qk_norm[:, 0:1, :], (1, 1, 2 * A_HEADS))
    kgain = jnp.tile(a_qk_norm[:, 1:2, :], (1, 1, 2 * A_HEADS))
    gparams = jnp.stack([_pad_lanes(-jnp.exp(b_a_log)), _pad_lanes(b_dt_bias)], axis=1)
    subln = a_subln[:, None, :]
    bnorm = b_norm[:, None, :]
    cnorm = c_norm[:, None, :]
    cache_k2 = cache_k.reshape(n_att, n_pool, page * A_HEADS, 2 * A_DK)
    cache_v2 = cache_v.reshape(n_att, n_pool, page * A_HEADS, A_DV)

    bias_l2 = (rel_bias - rel_bias[N_BUCKETS - 1:N_BUCKETS]) * LOG2E
    bias_p, bias_s = _bias_tiles(rel_bias, tq, past, ts)
    bias_s = bias_s.reshape(A_HEADS * 2 * ts, past * A_HEADS + LANES)

    xp = x_prompt.reshape(bp * tp, d)
    xs = x_sample.reshape(bs * ts, d)
    tm_p = ROWS_DENSE
    tm_s = bs * ts
    k_rows = [[], []]
    v_rows = [[], []]
    deltas = [[], []]
    convs = [[], []]
    glas = [[], []]

    for i in range(depth):
        li = i // 2
        xp = _ffn(xp, n1, ffn_in_bf, ffn_out_bf, i, 0, tm_p)
        xs = _ffn(xs, n1, ffn_in_bf, ffn_out_bf, i, 0, tm_s)
        if i % 2 == 0:
            lam_init = 0.8 - 0.6 * math.exp(-0.3 * i)
            lq1, lk1, lq2, lk2 = a_lambda[li]
            lam = jnp.exp(jnp.sum(lq1 * lk1)) - jnp.exp(jnp.sum(lq2 * lk2)) + lam_init
            smax = (1.02 * A_DK * A_DK ** -0.5 * LOG2E) * jnp.max(jnp.abs(a_qk_norm[li, 0])) * jnp.max(
                jnp.abs(a_qk_norm[li, 1]))
            shift = smax + jnp.max(bias_l2, axis=0)
            fast = jnp.all(2.0 * smax + jnp.max(bias_l2, axis=0) - jnp.min(bias_l2, axis=0) <= SAFE_SPAN)
            sc = jnp.concatenate([jnp.stack([lam, jnp.asarray(1.0 - lam_init, F32), fast.astype(F32)]),
                                  shift]).astype(F32)

            q, k32, kbf, v32, vt, bqkv, bz, gb = _inproj_even(
                xp, nm, w_even_bf, w_even_small, grp, qgain, kgain, gparams, i, li, ROWS_INPROJ)
            oa = _attn_prompt(sc, q.reshape(bp, tp, A_QK), kbf.reshape(bp, tp, A_QK), vt, bias_p, subln, li, tq)
            bqkv3 = bqkv.reshape(bp, tp, B_CONV_DIM)
            ob, s_new = _delta(bqkv3, bz.reshape(bp, tp, B_V), gb.reshape(bp, tp, LANES),
                               jnp.zeros((bp, SUBLANES, B_CONV_DIM), F32),
                               jnp.zeros((bp, B_HEADS, B_DK, B_DV), F32), b_conv, bnorm, li, SEQ_CHUNK)
            mix_p, w_mix = (oa.reshape(bp * tp, A_V), ob.reshape(bp * tp, B_V)), w_out_even_bf
            k_rows[0].append(k32.reshape(bp, tp, A_HEADS, 2 * A_DK))
            v_rows[0].append(v32.reshape(bp, tp, A_HEADS, A_DV))
            deltas[0].append(s_new)
            convs[0].append(bqkv3[:, tp - (CONV_W - 1):, :])

            q, k32, kbf, v32, _, bqkv, bz, gb = _inproj_even(
                xs, nm, w_even_bf, w_even_small, grp, qgain, kgain, gparams, i, li, tm_s)
            q4 = q.reshape(bs, ts, A_HEADS, 2, A_DK)
            qbd = jnp.einsum('bthcd,ce->bhcted', q4, jnp.eye(2, dtype=BF16)).reshape(bs, A_HEADS * 2 * ts, 2 * A_DK)
            knew = _pad_rows(kbf.reshape(bs, ts * A_HEADS, 2 * A_DK), LANES)
            vnew = _pad_rows(v32.reshape(bs, ts * A_HEADS, A_DV).astype(BF16), LANES)
            oa = _attn_sample(page_table, sc, qbd, bias_s, knew, vnew, subln, cache_k2, cache_v2, li, ts)
            oa = oa[:, :ts, :].astype(BF16)
            bqkv3 = bqkv.reshape(bs, ts, B_CONV_DIM)
            conv0 = jnp.pad(state_conv[li], ((0, 0), (SUBLANES - (CONV_W - 1), 0), (0, 0)))
            ob, s_new = _delta(_pad_rows(bqkv3, cs), _pad_rows(bz.reshape(bs, ts, B_V), cs),
                               _pad_rows(gb.reshape(bs, ts, LANES), cs), conv0, state_delta[li],
                               b_conv, bnorm, li, cs)
            mix_s = (oa.reshape(bs * ts, A_V), ob[:, :ts, :].reshape(bs * ts, B_V))
            k_rows[1].append(k32.reshape(bs, ts, A_HEADS, 2 * A_DK))
            v_rows[1].append(v32.reshape(bs, ts, A_HEADS, A_DV))
            deltas[1].append(s_new)
            convs[1].append(jnp.concatenate([state_conv[li], bqkv3], axis=1)[:, ts:, :])
        else:
            q, k, v, r, gk = _inproj_odd(xp, nm, w_odd_bf, w_odd_small, gate_w2, gate_b, i, li, ROWS_INPROJ)
            o, s_new = _gla(q.reshape(bp, tp, C_QK), k.reshape(bp, tp, C_QK), v.reshape(bp, tp, C_V),
                            r.reshape(bp, tp, C_V), gk.reshape(bp, tp, C_QK),
                            jnp.zeros((bp, C_HEADS, C_DK, C_DV), F32), cnorm, li, SEQ_CHUNK)
            mix_p, w_mix = (o.reshape(bp * tp, C_V),), w_out_odd_bf
            glas[0].append(s_new)

            q, k, v, r, gk = _inproj_odd(xs, nm, w_odd_bf, w_odd_small, gate_w2, gate_b, i, li, tm_s)
            pad = lambda a, n: _pad_rows(a.reshape(bs, ts, n), cs)
            o, s_new = _gla(pad(q, C_QK), pad(k, C_QK), pad(v, C_V), pad(r, C_V), pad(gk, C_QK),
                            state_gla[li], cnorm, li, cs)
            mix_s = (o[:, :ts, :].reshape(bs * ts, C_V),)
            glas[1].append(s_new)
        xp = _ffn(xp, n2, ffn_in_bf, ffn_out_bf, i, 1, tm_p, mix_p, w_mix, li)
        xs = _ffn(xs, n2, ffn_in_bf, ffn_out_bf, i, 1, tm_s, mix_s, w_mix, li)

    return (xp.reshape(bp, tp, d), xs.reshape(bs, ts, d),
            jnp.stack(k_rows[0]), jnp.stack(v_rows[0]), jnp.stack(deltas[0]), jnp.stack(convs[0]),
            jnp.stack(glas[0]),
            jnp.stack(k_rows[1]), jnp.stack(v_rows[1]), jnp.stack(deltas[1]), jnp.stack(convs[1]),
            jnp.stack(glas[1]))
```

```python
import functools
import math

import jax
import jax.numpy as jnp
from jax import lax
from jax.experimental import pallas as pl
from jax.experimental.pallas import tpu as pltpu

F32 = jnp.float32
BF16 = jnp.bfloat16

A_HEADS, A_DK, A_DV = 4, 64, 128
B_HEADS, B_DK, B_DV = 4, 128, 128
CONV_W = 4
C_HEADS, C_DK, C_DV = 4, 128, 256
GLA_LOWRANK, GLA_TAU = 16, 16.0
N_BUCKETS, MAX_DIST = 32, 128
EPS = 1e-6
NEG = -1e30
LOG2E = 1.4426950408889634

A_QK = A_HEADS * 2 * A_DK
A_V = A_HEADS * A_DV
B_QK = B_HEADS * B_DK
B_V = B_HEADS * B_DV
B_CONV_DIM = 2 * B_QK + B_V
C_QK = C_HEADS * C_DK
C_V = C_HEADS * C_DV
IN_EVEN_MAIN = 2 * A_QK + A_V + B_CONV_DIM + B_V
IN_ODD_MAIN = 2 * C_QK + 2 * C_V

LANES = 128
SUBLANES = 8
PACKED_SUBLANES = 16
VMEM_LIMIT = 56 * 2 ** 20

ROWS_DENSE = 512
ROWS_INPROJ = 256
ATT_TILE = 512
VT_ROWS = A_DV + PACKED_SUBLANES
SAFE_SPAN = 96.0
SEQ_CHUNK = 64
DELTA_PREP_CHUNKS = 4
GLA_SUB = 16
PAGES_PER_STEP = 16

NT = (((1,), (1,)), ((), ()))
TN = (((0,), (0,)), ((), ()))


def _cp(*sem):
    return pltpu.CompilerParams(dimension_semantics=sem, vmem_limit_bytes=VMEM_LIMIT)


def _mm(a, b, dn=None):
    if dn is None:
        return jnp.dot(a, b, preferred_element_type=F32)
    return lax.dot_general(a, b, dn, preferred_element_type=F32)


def _split2(x):
    hi = x.astype(BF16)
    lo = (x - hi.astype(F32)).astype(BF16)
    return hi, lo


def _split3(x):
    hi = x.astype(BF16)
    r = x - hi.astype(F32)
    mid = r.astype(BF16)
    lo = (r - mid.astype(F32)).astype(BF16)
    return hi, mid, lo


def _mm3(a, b):
    ah, al = _split2(a)
    bh, bl = _split2(b)
    return _mm(ah, bh) + (_mm(ah, bl) + _mm(al, bh))


def _mm_exact_lhs(a_bf, x):
    hi, mid, lo = _split3(x)
    return _mm(a_bf, hi) + (_mm(a_bf, mid) + _mm(a_bf, lo))


def _sigmoid(x):
    return 1.0 / (1.0 + jnp.exp(-x))


def _silu(x):
    return x * _sigmoid(x)


def _softplus(x):
    return jnp.maximum(x, 0.0) + jnp.log(1.0 + jnp.exp(-jnp.abs(x)))


def _rmsnorm(xf, g):
    return xf * lax.rsqrt(jnp.mean(xf * xf, axis=-1, keepdims=True) + EPS) * g


def _iota(shape, dim):
    return lax.broadcasted_iota(jnp.int32, shape, dim)


def _row_to_col(row, n):
    eye = _iota((n, n), 0) == _iota((n, n), 1)
    return jnp.sum(jnp.where(eye, jnp.broadcast_to(row, (n, n)), 0.0), axis=1, keepdims=True)


def _ffn_kernel(x_ref, g_ref, wg_ref, wu_ref, wo_ref, *rest):
    *mix_refs, o_ref, h_ref = rest
    j = pl.program_id(1)

    @pl.when(j == 0)
    def _():
        xf = x_ref[...]
        if mix_refs:
            wm_ref = mix_refs[-1]
            off = 0
            for m_ref in mix_refs[:-1]:
                n = m_ref.shape[1]
                xf = xf + _mm(m_ref[...], wm_ref[off:off + n, :])
                off += n
        h_ref[...] = _rmsnorm(xf, g_ref[...]).astype(BF16)
        o_ref[...] = xf

    h = h_ref[...]
    gate = _mm(h, wg_ref[...])
    up = _mm(h, wu_ref[...])
    a = (_silu(gate) * up).astype(BF16)
    o_ref[...] += 0.5 * _mm(a, wo_ref[...])


def _ffn(x, g, w_in, w_out, layer, which, tm, mix=(), w_mix=None, li=0):
    m, d = x.shape
    f = w_out.shape[2]
    nf = 2
    tf = f // nf
    assert f % nf == 0 and tf % LANES == 0 and m % tm == 0
    mix_specs = [pl.BlockSpec((tm, o.shape[1]), lambda i, j: (i, 0)) for o in mix]
    if mix:
        assert sum(o.shape[1] for o in mix) == w_mix.shape[1]
        mix_specs.append(pl.BlockSpec((None, w_mix.shape[1], d), lambda i, j: (li, 0, 0)))
    return pl.pallas_call(
        _ffn_kernel,
        grid=(m // tm, nf),
        in_specs=[
            pl.BlockSpec((tm, d), lambda i, j: (i, 0)),
            pl.BlockSpec((None, 1, d), lambda i, j: (layer, 0, 0)),
            pl.BlockSpec((None, None, d, tf), lambda i, j: (layer, which, 0, j)),
            pl.BlockSpec((None, None, d, tf), lambda i, j: (layer, which, 0, j + nf)),
            pl.BlockSpec((None, None, tf, d), lambda i, j: (layer, which, j, 0)),
        ] + mix_specs,
        out_specs=pl.BlockSpec((tm, d), lambda i, j: (i, 0)),
        out_shape=jax.ShapeDtypeStruct((m, d), F32),
        scratch_shapes=[pltpu.VMEM((tm, d), BF16)],
        compiler_params=_cp("parallel", "arbitrary"),
        name="ffn",
    )(x, g, w_in, w_in, w_out, *mix, *([w_mix] if mix else []))


def _group_rms(x, grp_bf, gain, width):
    hi, lo = _split2(x * x)
    ms = (_mm(hi, grp_bf) + _mm(lo, grp_bf)) * (1.0 / width)
    return x * lax.rsqrt(ms + EPS) * gain


def _inproj_even_kernel(x_ref, g_ref, w_ref, ws_ref, grp_ref, qg_ref, kg_ref, gp_ref,
                        q_ref, k32_ref, kbf_ref, v32_ref, vt_ref, bqkv_ref, bz_ref, gb_ref):
    h = _rmsnorm(x_ref[...], g_ref[...]).astype(BF16)
    grp = grp_ref[...]

    aq = _mm(h, w_ref[:, 0:A_QK])
    q = _group_rms(aq, grp, qg_ref[...], A_DK)
    q_ref[...] = (q * (A_DK ** -0.5 * LOG2E)).astype(BF16)

    ak = _mm(h, w_ref[:, A_QK:2 * A_QK])
    k = _group_rms(ak, grp, kg_ref[...], A_DK)
    kbf_ref[...] = k.astype(BF16)

    v = _mm(h, w_ref[:, 2 * A_QK:2 * A_QK + A_V])
    tm = v.shape[0]
    ones = jnp.ones((VT_ROWS - A_DV, tm), BF16)
    for hd in range(A_HEADS):
        k32_ref[pl.ds(hd, tm, stride=A_HEADS), :] = k[:, 2 * A_DK * hd:2 * A_DK * (hd + 1)]
        v32_ref[pl.ds(hd, tm, stride=A_HEADS), :] = v[:, A_DV * hd:A_DV * (hd + 1)]
        vt_ref[hd, 0:A_DV, :] = v[:, A_DV * hd:A_DV * (hd + 1)].T.astype(BF16)
        vt_ref[hd, A_DV:VT_ROWS, :] = ones

    o = 2 * A_QK + A_V
    bqkv_ref[...] = _mm(h, w_ref[:, o:o + B_CONV_DIM])
    o += B_CONV_DIM
    bz_ref[...] = _mm(h, w_ref[:, o:o + B_V])

    y = _mm(h, ws_ref[...])
    lane = _iota(y.shape, 1)
    g = gp_ref[0:1, :] * _softplus(y + gp_ref[1:2, :])
    gb_ref[...] = jnp.where(lane < B_HEADS, g, jnp.where(lane < 2 * B_HEADS, _sigmoid(y), 0.0))


def _inproj_even(x, g, w, ws, grp, qg, kg, gp, layer, li, tm):
    m, d = x.shape
    wmain = IN_EVEN_MAIN
    row = lambda n: pl.BlockSpec((tm, n), lambda i: (i, 0))
    out_shapes = [
        jax.ShapeDtypeStruct((m, A_QK), BF16),
        jax.ShapeDtypeStruct((m * A_HEADS, 2 * A_DK), F32),
        jax.ShapeDtypeStruct((m, A_QK), BF16),
        jax.ShapeDtypeStruct((m * A_HEADS, A_DV), F32),
        jax.ShapeDtypeStruct((A_HEADS, m // tm, VT_ROWS, tm), BF16),
        jax.ShapeDtypeStruct((m, B_CONV_DIM), F32),
        jax.ShapeDtypeStruct((m, B_V), F32),
        jax.ShapeDtypeStruct((m, LANES), F32),
    ]
    out_specs = [row(s.shape[1]) for s in out_shapes]
    out_specs[1] = pl.BlockSpec((tm * A_HEADS, 2 * A_DK), lambda i: (i, 0))
    out_specs[3] = pl.BlockSpec((tm * A_HEADS, A_DV), lambda i: (i, 0))
    out_specs[4] = pl.BlockSpec((A_HEADS, None, VT_ROWS, tm), lambda i: (0, i, 0, 0))
    return pl.pallas_call(
        _inproj_even_kernel,
        grid=(m // tm,),
        in_specs=[
            row(d),
            pl.BlockSpec((None, 1, d), lambda i: (layer, 0, 0)),
            pl.BlockSpec((None, d, wmain), lambda i: (li, 0, 0)),
            pl.BlockSpec((None, d, LANES), lambda i: (li, 0, 0)),
            pl.BlockSpec((A_QK, A_QK), lambda i: (0, 0)),
            pl.BlockSpec((None, 1, A_QK), lambda i: (li, 0, 0)),
            pl.BlockSpec((None, 1, A_QK), lambda i: (li, 0, 0)),
            pl.BlockSpec((None, 2, LANES), lambda i: (li, 0, 0)),
        ],
        out_specs=out_specs,
        out_shape=out_shapes,
        compiler_params=_cp("parallel"),
        name="inproj_even",
    )(x, g, w, ws, grp, qg, kg, gp)


def _bias_kernel(rb_ref, bp_ref, bs_ref, *, tq, past, page, tnew):
    h = pl.program_id(0)
    far = rb_ref[N_BUCKETS - 1, h]
    exact = N_BUCKETS // 2

    def table(rel):
        n = jnp.maximum(rel, 0)
        nf = jnp.maximum(n, 1).astype(F32)
        large = exact + (jnp.log(nf / exact) / math.log(MAX_DIST / exact) * (N_BUCKETS - exact)).astype(jnp.int32)
        bucket = jnp.where(n < exact, n, jnp.minimum(large, N_BUCKETS - 1))
        val = jnp.zeros(rel.shape, F32)
        for b in range(N_BUCKETS):
            val = jnp.where(bucket == b, rb_ref[b, h], val)
        return jnp.where(rel >= 0, (val - far) * LOG2E, NEG)

    rows = 2 * SUBLANES
    for d in range(2):
        def fill(c, carry, d=d):
            r0 = pl.multiple_of(c * rows, rows)
            key_p = r0 + _iota((rows, tq), 0)
            bp_ref[d, pl.ds(r0, rows), :] = table(d * tq + _iota((rows, tq), 1) - key_p)
            return carry

        lax.fori_loop(0, tq // rows, fill, 0)

    ncol = page * A_HEADS + LANES
    t_idx = _iota((2 * tnew, ncol), 0) & (tnew - 1)
    col = _iota((2 * tnew, ncol), 1)
    key = past - page + (col >> 2)
    rel_new = jnp.where(key - past < tnew, t_idx - (key - past), -1)
    rel = jnp.where(key < past, past + t_idx - key, rel_new)
    bs_ref[...] = table(jnp.where((col & (A_HEADS - 1)) == h, rel, -1))


def _bias_tiles(rel_bias, tq, past, page, tnew):
    assert tnew & (tnew - 1) == 0 and 2 * tnew == SUBLANES and A_HEADS == 4
    ncol = page * A_HEADS + LANES
    return pl.pallas_call(
        functools.partial(_bias_kernel, tq=tq, past=past, page=page, tnew=tnew),
        grid=(A_HEADS,),
        in_specs=[pl.BlockSpec(memory_space=pltpu.SMEM)],
        out_specs=[
            pl.BlockSpec((None, 2, tq, tq), lambda h: (h, 0, 0, 0)),
            pl.BlockSpec((None, 2 * tnew, ncol), lambda h: (h, 0, 0)),
        ],
        out_shape=[
            jax.ShapeDtypeStruct((A_HEADS, 2, tq, tq), F32),
            jax.ShapeDtypeStruct((A_HEADS, 2 * tnew, ncol), F32),
        ],
        compiler_params=_cp("parallel"),
        name="t5_bias",
    )(rel_bias)


def _far_bucket_saturated(dist):
    exact = N_BUCKETS // 2
    v = math.log(dist / exact) / math.log(MAX_DIST / exact) * (N_BUCKETS - exact)
    return dist >= exact and exact + v >= N_BUCKETS - 1 + 1e-3


def _attn_prompt_kernel(sc_ref, q_ref, k_ref, vt_ref, bias_ref, sg_ref, o_ref, *, tq, vblk):
    head = pl.program_id(1)
    i = pl.program_id(2)
    q = q_ref[...]
    lane = _iota(q.shape, 1)
    zero = jnp.zeros_like(q)
    qbd = jnp.concatenate([jnp.where(lane < A_DK, q, zero), jnp.where(lane >= A_DK, q, zero)], axis=0)
    nsub = tq // vblk

    def scores(j):
        start = pl.multiple_of(j * tq, tq)
        return _mm(k_ref[pl.ds(start, tq), :], qbd, NT)

    def biased(j, bias):
        s = scores(j)
        return s if bias is None else s + jnp.concatenate([bias, bias], axis=1)

    def weighted_values(j, p):
        pv = _mm(vt_ref[j * nsub], p[0:vblk])
        for u in range(1, nsub):
            pv = pv + _mm(vt_ref[j * nsub + u], p[u * vblk:(u + 1) * vblk])
        return pv

    def key_blocks(step, carry):
        nfar = jnp.maximum(i - 1, 0)
        per_trip = 4

        def trip(j, c):
            for u in range(per_trip):
                c = step(per_trip * j + u, c, None)
            return c

        carry = lax.fori_loop(0, nfar // per_trip, trip, carry)
        carry = lax.fori_loop(nfar - nfar % per_trip, nfar, lambda j, c: step(j, c, None), carry)
        return lax.cond(i >= 1, lambda c: step(i, step(i - 1, c, bias_ref[1]), bias_ref[0]),
                        lambda c: step(i, c, bias_ref[0]), carry)

    def fixed_shift(_):
        shift = sc_ref[3 + head]

        def step(j, acc, bias):
            return acc + weighted_values(j, jnp.exp2(biased(j, bias) - shift).astype(BF16))

        return key_blocks(step, jnp.zeros((VT_ROWS, 2 * tq), F32))

    def running_max(_):
        def step(j, c, bias):
            m, acc = c
            s = biased(j, bias)
            m_new = jnp.maximum(m, jnp.max(s, axis=0, keepdims=True))
            p = jnp.exp2(s - m_new).astype(BF16)
            return m_new, jnp.exp2(m - m_new) * acc + weighted_values(j, p)

        return key_blocks(step, (jnp.full((1, 2 * tq), NEG, F32), jnp.zeros((VT_ROWS, 2 * tq), F32)))[1]

    acc = lax.cond(sc_ref[2] > 0.5, fixed_shift, running_max, 0)

    o = acc[0:A_DV] / acc[A_DV:A_DV + 1]
    o = o[:, 0:tq] - sc_ref[0] * o[:, tq:2 * tq]
    o = o * lax.rsqrt(jnp.mean(o * o, axis=0, keepdims=True) + EPS)
    o_ref[...] = (o.T * (sg_ref[...] * sc_ref[1])).astype(BF16)


def _attn_prompt(sc, q, k, vt, bias_p, sg, li, tq):
    bsz, t, _ = q.shape
    vblk = vt.shape[3]
    assert tq % vblk == 0 and t % tq == 0
    return pl.pallas_call(
        functools.partial(_attn_prompt_kernel, tq=tq, vblk=vblk),
        grid=(bsz, A_HEADS, t // tq),
        in_specs=[
            pl.BlockSpec(memory_space=pltpu.SMEM),
            pl.BlockSpec((None, tq, 2 * A_DK), lambda b, h, i: (b, i, h)),
            pl.BlockSpec((None, t, 2 * A_DK), lambda b, h, i: (b, 0, h)),
            pl.BlockSpec((None, t // vblk, VT_ROWS, vblk), lambda b, h, i: (h, b, 0, 0)),
            pl.BlockSpec((None, 2, tq, tq), lambda b, h, i: (h, 0, 0, 0)),
            pl.BlockSpec((None, 1, A_DV), lambda b, h, i: (li, 0, 0)),
        ],
        out_specs=pl.BlockSpec((None, tq, A_DV), lambda b, h, i: (b, i, h)),
        out_shape=jax.ShapeDtypeStruct((bsz, t, A_V), BF16),
        compiler_params=_cp("parallel", "parallel", "arbitrary"),
        name="attn_prompt",
    )(sc, q, k, vt, bias_p, sg)


def _attn_sample_kernel(pt_ref, sc_ref, qbd_ref, bias_ref, biasn_ref, kn_ref, vn_ref, sg_ref, *rest, npg, tnew):
    k_refs = rest[:npg]
    v_refs = rest[npg:2 * npg]
    o_ref, m_ref, l_ref, acc_ref, mask_ref = rest[2 * npg:]
    pg = pl.program_id(1)
    last = pg == pl.num_programs(1) - 1
    qbd = qbd_ref[...]

    @pl.when(pg == 0)
    def _():
        m_ref[...] = jnp.full(m_ref.shape, NEG, F32)
        l_ref[...] = jnp.zeros(l_ref.shape, F32)
        acc_ref[...] = jnp.zeros(acc_ref.shape, F32)
        row_head = _iota(mask_ref.shape, 0) >> ((2 * tnew).bit_length() - 1)
        col_head = _iota(mask_ref.shape, 1) & (A_HEADS - 1)
        mask_ref[...] = jnp.where(row_head == col_head, 0.0, NEG)

    def update(s, vs):
        m_old = m_ref[...]
        m_new = jnp.maximum(m_old, jnp.max(s, axis=1, keepdims=True))
        alpha = jnp.exp2(m_old - m_new)
        p = jnp.exp2(s - m_new)
        l_ref[...] = alpha * l_ref[...] + jnp.sum(p, axis=1, keepdims=True)
        pb = p.astype(BF16)
        w = pb.shape[1] // len(vs)
        pv = _mm(pb[:, 0:w], vs[0])
        for g in range(1, len(vs)):
            pv = pv + _mm(pb[:, g * w:(g + 1) * w], vs[g])
        acc_ref[...] = alpha * acc_ref[...] + pv
        m_ref[...] = m_new

    mask = mask_ref[...]
    final_page = jnp.where(last, bias_ref[...], mask)
    s = jnp.concatenate([_mm(qbd, kr[...].astype(BF16), NT) + (mask if g < npg - 1 else final_page)
                         for g, kr in enumerate(k_refs)], axis=1)
    update(s, [vr[...].astype(BF16) for vr in v_refs])

    @pl.when(last)
    def _():
        update(_mm(qbd, kn_ref[...], NT) + biasn_ref[...], [vn_ref[...]])
        lam = sc_ref[0]
        rows = 2 * tnew
        for h in range(A_HEADS):
            a = acc_ref[rows * h:rows * (h + 1), :] / l_ref[rows * h:rows * (h + 1), :]
            o = a - lam * pltpu.roll(a, tnew, axis=0)
            o_ref[:, A_DV * h:A_DV * (h + 1)] = _rmsnorm(o, sg_ref[...]) * sc_ref[1]


def _attn_sample(page_table, sc, qbd, bias_s, knew, vnew, sg, cache_k, cache_v, li, tnew):
    bsz, nrow, _ = qbd.shape
    n_pages = page_table.shape[1]
    npg = PAGES_PER_STEP
    assert n_pages % npg == 0
    prow = cache_k.shape[2]
    assert bias_s.shape == (nrow, prow + LANES)

    def kvspec(g):
        return pl.BlockSpec((None, None, prow, 2 * A_DK), lambda b, pg, pt: (li, pt[b, pg * npg + g], 0, 0))

    grid_spec = pltpu.PrefetchScalarGridSpec(
        num_scalar_prefetch=1,
        grid=(bsz, n_pages // npg),
        in_specs=[
            pl.BlockSpec(memory_space=pltpu.SMEM),
            pl.BlockSpec((None, nrow, 2 * A_DK), lambda b, pg, pt: (b, 0, 0)),
            pl.BlockSpec((nrow, prow), lambda b, pg, pt: (0, 0)),
            pl.BlockSpec((nrow, LANES), lambda b, pg, pt: (0, prow // LANES)),
            pl.BlockSpec((None, LANES, 2 * A_DK), lambda b, pg, pt: (b, 0, 0)),
            pl.BlockSpec((None, LANES, A_DV), lambda b, pg, pt: (b, 0, 0)),
            pl.BlockSpec((None, 1, A_DV), lambda b, pg, pt: (li, 0, 0)),
        ] + [kvspec(g) for g in range(npg)] + [kvspec(g) for g in range(npg)],
        out_specs=pl.BlockSpec((None, 2 * tnew, A_V), lambda b, pg, pt: (b, 0, 0)),
        scratch_shapes=[pltpu.VMEM((nrow, 1), F32), pltpu.VMEM((nrow, 1), F32), pltpu.VMEM((nrow, A_DV), F32),
                        pltpu.VMEM((nrow, prow), F32)],
    )
    return pl.pallas_call(
        functools.partial(_attn_sample_kernel, npg=npg, tnew=tnew),
        grid_spec=grid_spec,
        out_shape=jax.ShapeDtypeStruct((bsz, 2 * tnew, A_V), F32),
        compiler_params=_cp("parallel", "arbitrary"),
        name="attn_sample",
    )(page_table, sc, qbd, bias_s, bias_s, knew, vnew, sg, *([cache_k] * npg), *([cache_v] * npg))


def _l2norm(x):
    return x * lax.rsqrt(jnp.sum(x * x, axis=-1, keepdims=True) + EPS)


def _unit_lower_inverses(mats, c):
    n = mats[0].shape[0]
    nblk = n // c
    shift = c.bit_length() - 1
    same = (_iota((n, n), 0) >> shift) == (_iota((n, n), 1) >> shift)
    eye_w = ((_iota((c, n), 1) & (c - 1)) == _iota((c, n), 0)).astype(F32)

    def wide(sq):
        out = sq[0:c]
        for r in range(1, nblk):
            out = out + sq[r * c:(r + 1) * c]
        return out

    def square(w):
        return jnp.where(same, jnp.concatenate([w] * nblk, axis=0), jnp.zeros((), w.dtype))

    def mm3_split(lhs_split, rhs_split):
        (lh, ll), (rh, rl) = lhs_split, rhs_split
        return _mm(lh, rh) + (_mm(lh, rl) + _mm(ll, rh))

    qw = [_split2(wide(a)) for a in mats]
    qsq = [(square(h), square(l)) for h, l in qw]
    ps = [eye_w - wide(a) for a in mats]
    span = 2
    while span < c:
        qw = [_split2(mm3_split(q, s)) for q, s in zip(qw, qsq)]
        qsq = [(square(h), square(l)) for h, l in qw]
        ps = [p + mm3_split(_split2(p), s) for p, s in zip(ps, qsq)]
        span *= 2
    return [square(p) for p in ps]


def _cumsum_rows(x, n):
    row = _iota(x.shape, 0)
    k = 1
    while k < n:
        x = x + jnp.where(row >= k, pltpu.roll(x, k, axis=0), 0.0)
        k *= 2
    return x


def _delta_prep_kernel(x_ref, halo_ref, c0_ref, gb_ref, cw_ref, u_ref, wk_ref, qe_ref, kd_ref, qk_ref, xbuf,
                       *, c, nck):
    ci = pl.program_id(1)
    ch = B_HEADS * c
    rows = nck * c
    xbuf[0:SUBLANES, :] = jnp.where(ci == 0, c0_ref[...], halo_ref[...])
    xbuf[SUBLANES:SUBLANES + rows, :] = x_ref[...]
    first = SUBLANES - (CONV_W - 1)
    conv = xbuf[pl.ds(first, rows), :] * cw_ref[0:1, :]
    for j in range(1, CONV_W):
        conv = conv + xbuf[pl.ds(first + j, rows), :] * cw_ref[j:j + 1, :]
    conv = _silu(conv)

    row = _iota((ch, ch), 0)
    col = _iota((ch, ch), 1)
    shift = c.bit_length() - 1
    same = (row >> shift) == (col >> shift)
    incl = same & (row >= col)
    strict = same & (row > col)

    def stack_lane(a, off):
        return jnp.concatenate([a[:, off + h:off + h + 1] for h in range(B_HEADS)], axis=0)

    parts = []
    for n in range(nck):
        rs = slice(n * c, (n + 1) * c)

        def stack(off, width):
            return jnp.concatenate([conv[rs, off + width * h:off + width * (h + 1)] for h in range(B_HEADS)], axis=0)

        q = _l2norm(stack(0, B_DK)) * (B_DK ** -0.5)
        k = _l2norm(stack(B_QK, B_DK))
        v = stack(2 * B_QK, B_DV)
        gb = gb_ref[rs, :]
        gcum = _cumsum_rows(gb, c)
        gcol = stack_lane(gcum, 0)
        beta = stack_lane(gb, B_HEADS)
        glast = stack_lane(jnp.broadcast_to(gcum[c - 1:c, :], gcum.shape), 0)
        grow = jnp.broadcast_to(gcol, (ch, LANES)).T[0:1, :]
        dec = jnp.where(incl, jnp.exp(jnp.where(incl, gcol - grow, 0.0)), 0.0)
        kb = k.astype(BF16)
        a = jnp.where(strict, _mm(kb, kb, NT) * dec * beta, 0.0)
        egc = jnp.exp(gcol)
        qe_ref[n] = (q * egc).astype(BF16)
        kd_ref[n] = (k * jnp.exp(glast - gcol)).astype(BF16)
        qk_ref[n] = (_mm(q.astype(BF16), kb, NT) * dec).astype(BF16)
        parts.append((a, jnp.concatenate([v * beta, k * (beta * egc)], axis=1)))

    t_invs = _unit_lower_inverses([a for a, _ in parts], c)
    for n in range(nck):
        uw = _mm3(t_invs[n], parts[n][1])
        u_ref[n] = uw[:, 0:B_DV]
        wk_ref[n] = uw[:, B_DV:B_DV + B_DK].astype(BF16)


def _delta_scan_kernel(u_ref, wk_ref, qe_ref, kd_ref, qk_ref, gb_ref, z_ref, s0_ref, bn_ref, ob_ref, sout_ref,
                       s_scr, *, c, nb):
    ci = pl.program_id(1)

    @pl.when(ci == 0)
    def _():
        s_scr[...] = s0_ref[...]

    hs = [slice(c * h, c * (h + 1)) for h in range(B_HEADS)]
    for b in range(nb):
        decay = jnp.exp(jnp.sum(gb_ref[b], axis=0, keepdims=True))
        s = [s_scr[b, h] for h in range(B_HEADS)]
        s_bf = [x.astype(BF16) for x in s]
        v_new = u_ref[b] - jnp.concatenate([_mm(wk_ref[b, hs[h], :], s_bf[h]) for h in range(B_HEADS)], axis=0)
        vn_bf = v_new.astype(BF16)
        o = (jnp.concatenate([_mm(qe_ref[b, hs[h], :], s_bf[h]) for h in range(B_HEADS)], axis=0)
             + _mm(qk_ref[b], vn_bf))
        for h in range(B_HEADS):
            s_scr[b, h] = s[h] * decay[:, h:h + 1] + _mm(kd_ref[b, hs[h], :], vn_bf[hs[h]], TN)
            sl = slice(B_DV * h, B_DV * (h + 1))
            ob_ref[b, :, sl] = (_rmsnorm(o[hs[h]], bn_ref[...]) * _silu(z_ref[b, :, sl])).astype(BF16)

    @pl.when(ci == pl.num_programs(1) - 1)
    def _():
        sout_ref[...] = s_scr[...]


def _delta(bqkv, bz, gb, conv0, s0, cw, bn, li, c):
    bsz, t, _ = bqkv.shape
    assert t % c == 0 and c % SUBLANES == 0 and c & (c - 1) == 0
    nch = t // c
    ch = B_HEADS * c
    nck = min(DELTA_PREP_CHUNKS, nch)
    assert nch % nck == 0
    hb = nck * c // SUBLANES
    tok = lambda n: pl.BlockSpec((None, c, n), lambda b, ci: (b, ci, 0))
    step_tok = lambda n: pl.BlockSpec((None, nck * c, n), lambda b, ci: (b, ci, 0))
    step_chunks = lambda n: pl.BlockSpec((None, nck, ch, n), lambda b, ci: (b, ci, 0, 0))
    chunk_shape = lambda n, dt: jax.ShapeDtypeStruct((bsz, nch, ch, n), dt)
    u, wk, qe, kd, qk = pl.pallas_call(
        functools.partial(_delta_prep_kernel, c=c, nck=nck),
        grid=(bsz, nch // nck),
        in_specs=[
            step_tok(B_CONV_DIM),
            pl.BlockSpec((None, SUBLANES, B_CONV_DIM), lambda b, ci: (b, jnp.maximum(ci * hb - 1, 0), 0)),
            pl.BlockSpec((None, SUBLANES, B_CONV_DIM), lambda b, ci: (b, 0, 0)),
            step_tok(LANES),
            pl.BlockSpec((None, CONV_W, B_CONV_DIM), lambda b, ci: (li, 0, 0)),
        ],
        out_specs=[step_chunks(B_DV), step_chunks(B_DK), step_chunks(B_DK), step_chunks(B_DK), step_chunks(ch)],
        out_shape=[chunk_shape(B_DV, F32), chunk_shape(B_DK, BF16), chunk_shape(B_DK, BF16),
                   chunk_shape(B_DK, BF16), chunk_shape(ch, BF16)],
        scratch_shapes=[pltpu.VMEM((SUBLANES + nck * c, B_CONV_DIM), F32)],
        compiler_params=_cp("parallel", "parallel"),
        name="delta_prep",
    )(bqkv, bqkv, conv0, gb, cw)

    nb = 2
    assert bsz % nb == 0
    grp_chunk = lambda n: pl.BlockSpec((nb, None, ch, n), lambda b, ci: (b, ci, 0, 0))
    grp_tok = lambda n: pl.BlockSpec((nb, c, n), lambda b, ci: (b, ci, 0))
    state = pl.BlockSpec((nb, B_HEADS, B_DK, B_DV), lambda b, ci: (b, 0, 0, 0))
    return pl.pallas_call(
        functools.partial(_delta_scan_kernel, c=c, nb=nb),
        grid=(bsz // nb, nch),
        in_specs=[grp_chunk(B_DV), grp_chunk(B_DK), grp_chunk(B_DK), grp_chunk(B_DK), grp_chunk(ch),
                  grp_tok(LANES), grp_tok(B_V), state,
                  pl.BlockSpec((None, 1, B_DV), lambda b, ci: (li, 0, 0))],
        out_specs=[grp_tok(B_V), state],
        out_shape=[jax.ShapeDtypeStruct((bsz, t, B_V), BF16),
                   jax.ShapeDtypeStruct((bsz, B_HEADS, B_DK, B_DV), F32)],
        scratch_shapes=[pltpu.VMEM((nb, B_HEADS, B_DK, B_DV), F32)],
        compiler_params=_cp("parallel", "arbitrary"),
        name="delta_scan",
    )(u, wk, qe, kd, qk, gb, bz, s0, bn)


def _inproj_odd_kernel(x_ref, g_ref, w_ref, ws_ref, w2_ref, gbias_ref, q_ref, k_ref, v_ref, r_ref, gk_ref):
    h = _rmsnorm(x_ref[...], g_ref[...]).astype(BF16)
    q_ref[...] = _mm(h, w_ref[:, 0:C_QK])
    k_ref[...] = _mm(h, w_ref[:, C_QK:2 * C_QK])
    v_ref[...] = _mm(h, w_ref[:, 2 * C_QK:2 * C_QK + C_V]).astype(BF16)
    r_ref[...] = _mm(h, w_ref[:, 2 * C_QK + C_V:2 * C_QK + 2 * C_V])
    glr = _mm(h, ws_ref[...]).astype(BF16)
    y = _mm(glr, w2_ref[...]) + gbias_ref[...]
    gk_ref[...] = -_softplus(-y) * (1.0 / GLA_TAU)


def _inproj_odd(x, g, w, ws, w2, gbias, layer, li, tm):
    m, d = x.shape
    wmain = 2 * C_QK + 2 * C_V
    row = lambda n: pl.BlockSpec((tm, n), lambda i: (i, 0))
    out_shapes = [
        jax.ShapeDtypeStruct((m, C_QK), F32),
        jax.ShapeDtypeStruct((m, C_QK), F32),
        jax.ShapeDtypeStruct((m, C_V), BF16),
        jax.ShapeDtypeStruct((m, C_V), F32),
        jax.ShapeDtypeStruct((m, C_QK), F32),
    ]
    return pl.pallas_call(
        _inproj_odd_kernel,
        grid=(m // tm,),
        in_specs=[
            row(d),
            pl.BlockSpec((None, 1, d), lambda i: (layer, 0, 0)),
            pl.BlockSpec((None, d, wmain), lambda i: (li, 0, 0)),
            pl.BlockSpec((None, d, LANES), lambda i: (li, 0, 0)),
            pl.BlockSpec((None, LANES, C_QK), lambda i: (li, 0, 0)),
            pl.BlockSpec((None, 1, C_QK), lambda i: (li, 0, 0)),
        ],
        out_specs=[row(s.shape[1]) for s in out_shapes],
        out_shape=out_shapes,
        compiler_params=_cp("parallel"),
        name="inproj_odd",
    )(x, g, w, ws, w2, gbias)


def _gla_kernel(q_ref, k_ref, v_ref, r_ref, gk_ref, s0_ref, cn_ref, o_ref, sout_ref, s_scr, *, c, sub):
    ci = pl.program_id(1)

    @pl.when(ci == 0)
    def _():
        s_scr[...] = s0_ref[...]

    row = _iota((c, c), 0)
    col = _iota((c, c), 1)
    lower_bf = (row >= col).astype(BF16)
    in_sub = row & (sub - 1)
    sub_start = row - in_sub
    b_all = _mm_exact_lhs(lower_bf, gk_ref[...])

    for h in range(C_HEADS):
        sl = slice(C_DK * h, C_DK * (h + 1))
        vl = slice(C_DV * h, C_DV * (h + 1))
        q = q_ref[:, sl] * (C_DK ** -0.5)
        k = k_ref[:, sl]
        b = b_all[:, sl]
        v = v_ref[:, vl]
        s = s_scr[h]

        o = _mm((q * jnp.exp(b)).astype(BF16), s.astype(BF16))

        blocks = [jnp.zeros((sub, c), F32)]
        for r in range(1, c // sub):
            ref = b[sub * r:sub * r + 1, :]
            qt = q[sub * r:sub * (r + 1)] * jnp.exp(b[sub * r:sub * (r + 1)] - ref)
            kt = k * jnp.exp(jnp.minimum(ref - b, 0.0))
            blocks.append(_mm(qt.astype(BF16), kt.astype(BF16), NT))
        a = jnp.where(col < sub_start, jnp.concatenate(blocks, axis=0), 0.0)
        b2 = b * LOG2E
        for d in range(sub):
            kd = k if d == 0 else pltpu.roll(k, d, axis=0)
            bd = b2 if d == 0 else pltpu.roll(b2, d, axis=0)
            ad = jnp.sum(q * kd * jnp.exp2(b2 - bd), axis=1, keepdims=True)
            a = jnp.where((col == row - d) & (in_sub >= d), ad, a)
        o = o + _mm(a.astype(BF16), v)

        blast = b[c - 1:c, :]
        kdec = (k * jnp.exp(blast - b)).astype(BF16)
        s_scr[h] = s * _row_to_col(jnp.exp(blast), C_DK) + _mm(kdec, v, TN)

        o_ref[:, vl] = (_rmsnorm(o, cn_ref[...]) * _silu(r_ref[:, vl])).astype(BF16)

    @pl.when(ci == pl.num_programs(1) - 1)
    def _():
        sout_ref[...] = s_scr[...]


def _gla(q, k, v, r, gk, s0, cn, li, c):
    bsz, t, _ = q.shape
    assert t % c == 0
    sub = min(GLA_SUB, c)
    tok = lambda n: pl.BlockSpec((None, c, n), lambda b, ci: (b, ci, 0))
    state = pl.BlockSpec((None, C_HEADS, C_DK, C_DV), lambda b, ci: (b, 0, 0, 0))
    return pl.pallas_call(
        functools.partial(_gla_kernel, c=c, sub=sub),
        grid=(bsz, t // c),
        in_specs=[tok(C_QK), tok(C_QK), tok(C_V), tok(C_V), tok(C_QK), state,
                  pl.BlockSpec((None, 1, C_DV), lambda b, ci: (li, 0, 0))],
        out_specs=[tok(C_V), state],
        out_shape=[jax.ShapeDtypeStruct((bsz, t, C_V), BF16),
                   jax.ShapeDtypeStruct((bsz, C_HEADS, C_DK, C_DV), F32)],
        scratch_shapes=[pltpu.VMEM((C_HEADS, C_DK, C_DV), F32)],
        compiler_params=_cp("parallel", "arbitrary"),
        name="gla",
    )(q, k, v, r, gk, s0, cn)


def _pad_rows(a, rows):
    return jnp.pad(a, ((0, 0), (0, rows - a.shape[1]), (0, 0)))


def _pad_lanes(a):
    return jnp.pad(a, [(0, 0)] * (a.ndim - 1) + [(0, LANES - a.shape[-1])])


def kernel(x_prompt, x_sample, cache_k, cache_v, state_delta, state_conv, state_gla, page_table, rel_bias, norm_ffn1, norm_mix, norm_ffn2, ffn_in, ffn_out, w_in_even, a_qk_norm, a_lambda, a_subln, b_conv, b_a_log, b_dt_bias, b_norm, w_out_even, w_in_odd, c_gate_w2, c_gate_b, c_norm, w_out_odd):
    bp, tp, d = x_prompt.shape
    bs, ts, _ = x_sample.shape
    depth = norm_ffn1.shape[0]
    n_att = w_in_even.shape[0]
    n_pool, page = cache_k.shape[1], cache_k.shape[2]
    past = page_table.shape[1] * page
    tq = ATT_TILE
    cs = SUBLANES
    assert page == LANES and tp % tq == 0 and ts <= cs and ts >= CONV_W - 1
    assert _far_bucket_saturated(tq + 1) and _far_bucket_saturated(page + 1)

    ffn_in_bf = ffn_in.astype(BF16)
    ffn_out_bf = ffn_out.astype(BF16)
    w_even_bf = w_in_even.astype(BF16)
    w_even_small = _pad_lanes(w_in_even[:, :, IN_EVEN_MAIN:]).astype(BF16)
    w_out_even_bf = w_out_even.astype(BF16)
    w_odd_bf = w_in_odd.astype(BF16)
    w_odd_small = _pad_lanes(w_in_odd[:, :, IN_ODD_MAIN:]).astype(BF16)
    w_out_odd_bf = w_out_odd.astype(BF16)
    gate_w2 = jnp.pad(c_gate_w2, ((0, 0), (0, LANES - GLA_LOWRANK), (0, 0))).astype(BF16)
    gate_b = c_gate_b[:, None, :]
    n1 = norm_ffn1[:, None, :]
    nm = norm_mix[:, None, :]
    n2 = norm_ffn2[:, None, :]
    grp = jnp.kron(jnp.eye(2 * A_HEADS, dtype=F32), jnp.ones((A_DK, A_DK), F32)).astype(BF16)
    qgain = jnp.tile(a_qk_norm[:, 0:1, :], (1, 1, 2 * A_HEADS))
    kgain = jnp.tile(a_qk_norm[:, 1:2, :], (1, 1, 2 * A_HEADS))
    gparams = jnp.stack([_pad_lanes(-jnp.exp(b_a_log)), _pad_lanes(b_dt_bias)], axis=1)
    subln = a_subln[:, None, :]
    bnorm = b_norm[:, None, :]
    cnorm = c_norm[:, None, :]
    cache_k2 = cache_k.reshape(n_att, n_pool, page * A_HEADS, 2 * A_DK)
    cache_v2 = cache_v.reshape(n_att, n_pool, page * A_HEADS, A_DV)

    bias_l2 = (rel_bias - rel_bias[N_BUCKETS - 1:N_BUCKETS]) * LOG2E
    bias_p, bias_s = _bias_tiles(rel_bias, tq, past, page, ts)
    bias_s = bias_s.reshape(A_HEADS * 2 * ts, page * A_HEADS + LANES)

    xp = x_prompt.reshape(bp * tp, d)
    xs = x_sample.reshape(bs * ts, d)
    tm_p = ROWS_DENSE
    tm_s = bs * ts
    k_rows = [[], []]
    v_rows = [[], []]
    deltas = [[], []]
    convs = [[], []]
    glas = [[], []]

    for i in range(depth):
        li = i // 2
        xp = _ffn(xp, n1, ffn_in_bf, ffn_out_bf, i, 0, tm_p)
        xs = _ffn(xs, n1, ffn_in_bf, ffn_out_bf, i, 0, tm_s)
        if i % 2 == 0:
            lam_init = 0.8 - 0.6 * math.exp(-0.3 * i)
            lq1, lk1, lq2, lk2 = a_lambda[li]
            lam = jnp.exp(jnp.sum(lq1 * lk1)) - jnp.exp(jnp.sum(lq2 * lk2)) + lam_init
            smax = (1.02 * A_DK * A_DK ** -0.5 * LOG2E) * jnp.max(jnp.abs(a_qk_norm[li, 0])) * jnp.max(
                jnp.abs(a_qk_norm[li, 1]))
            shift = smax + jnp.max(bias_l2, axis=0)
            fast = jnp.all(2.0 * smax + jnp.max(bias_l2, axis=0) - jnp.min(bias_l2, axis=0) <= SAFE_SPAN)
            sc = jnp.concatenate([jnp.stack([lam, jnp.asarray(1.0 - lam_init, F32), fast.astype(F32)]),
                                  shift]).astype(F32)

            q, k32, kbf, v32, vt, bqkv, bz, gb = _inproj_even(
                xp, nm, w_even_bf, w_even_small, grp, qgain, kgain, gparams, i, li, ROWS_INPROJ)
            oa = _attn_prompt(sc, q.reshape(bp, tp, A_QK), kbf.reshape(bp, tp, A_QK), vt, bias_p, subln, li, tq)
            bqkv3 = bqkv.reshape(bp, tp, B_CONV_DIM)
            ob, s_new = _delta(bqkv3, bz.reshape(bp, tp, B_V), gb.reshape(bp, tp, LANES),
                               jnp.zeros((bp, SUBLANES, B_CONV_DIM), F32),
                               jnp.zeros((bp, B_HEADS, B_DK, B_DV), F32), b_conv, bnorm, li, SEQ_CHUNK)
            mix_p, w_mix = (oa.reshape(bp * tp, A_V), ob.reshape(bp * tp, B_V)), w_out_even_bf
            k_rows[0].append(k32.reshape(bp, tp, A_HEADS, 2 * A_DK))
            v_rows[0].append(v32.reshape(bp, tp, A_HEADS, A_DV))
            deltas[0].append(s_new)
            convs[0].append(bqkv3[:, tp - (CONV_W - 1):, :])

            q, k32, kbf, v32, _, bqkv, bz, gb = _inproj_even(
                xs, nm, w_even_bf, w_even_small, grp, qgain, kgain, gparams, i, li, tm_s)
            q4 = q.reshape(bs, ts, A_HEADS, 2, A_DK)
            qbd = jnp.einsum('bthcd,ce->bhcted', q4, jnp.eye(2, dtype=BF16)).reshape(bs, A_HEADS * 2 * ts, 2 * A_DK)
            knew = _pad_rows(kbf.reshape(bs, ts * A_HEADS, 2 * A_DK), LANES)
            vnew = _pad_rows(v32.reshape(bs, ts * A_HEADS, A_DV).astype(BF16), LANES)
            oa = _attn_sample(page_table, sc, qbd, bias_s, knew, vnew, subln, cache_k2, cache_v2, li, ts)
            oa = oa[:, :ts, :].astype(BF16)
            bqkv3 = bqkv.reshape(bs, ts, B_CONV_DIM)
            conv0 = jnp.pad(state_conv[li], ((0, 0), (SUBLANES - (CONV_W - 1), 0), (0, 0)))
            ob, s_new = _delta(_pad_rows(bqkv3, cs), _pad_rows(bz.reshape(bs, ts, B_V), cs),
                               _pad_rows(gb.reshape(bs, ts, LANES), cs), conv0, state_delta[li],
                               b_conv, bnorm, li, cs)
            mix_s = (oa.reshape(bs * ts, A_V), ob[:, :ts, :].reshape(bs * ts, B_V))
            k_rows[1].append(k32.reshape(bs, ts, A_HEADS, 2 * A_DK))
            v_rows[1].append(v32.reshape(bs, ts, A_HEADS, A_DV))
            deltas[1].append(s_new)
            convs[1].append(jnp.concatenate([state_conv[li], bqkv3], axis=1)[:, ts:, :])
        else:
            q, k, v, r, gk = _inproj_odd(xp, nm, w_odd_bf, w_odd_small, gate_w2, gate_b, i, li, ROWS_INPROJ)
            o, s_new = _gla(q.reshape(bp, tp, C_QK), k.reshape(bp, tp, C_QK), v.reshape(bp, tp, C_V),
                            r.reshape(bp, tp, C_V), gk.reshape(bp, tp, C_QK),
                            jnp.zeros((bp, C_HEADS, C_DK, C_DV), F32), cnorm, li, SEQ_CHUNK)
            mix_p, w_mix = (o.reshape(bp * tp, C_V),), w_out_odd_bf
            glas[0].append(s_new)

            q, k, v, r, gk = _inproj_odd(xs, nm, w_odd_bf, w_odd_small, gate_w2, gate_b, i, li, tm_s)
            pad = lambda a, n: _pad_rows(a.reshape(bs, ts, n), cs)
            o, s_new = _gla(pad(q, C_QK), pad(k, C_QK), pad(v, C_V), pad(r, C_V), pad(gk, C_QK),
                            state_gla[li], cnorm, li, cs)
            mix_s = (o[:, :ts, :].reshape(bs * ts, C_V),)
            glas[1].append(s_new)
        xp = _ffn(xp, n2, ffn_in_bf, ffn_out_bf, i, 1, tm_p, mix_p, w_mix, li)
        xs = _ffn(xs, n2, ffn_in_bf, ffn_out_bf, i, 1, tm_s, mix_s, w_mix, li)

    return (xp.reshape(bp, tp, d), xs.reshape(bs, ts, d),
            jnp.stack(k_rows[0]), jnp.stack(v_rows[0]), jnp.stack(deltas[0]), jnp.stack(convs[0]),
            jnp.stack(glas[0]),
            jnp.stack(k_rows[1]), jnp.stack(v_rows[1]), jnp.stack(deltas[1]), jnp.stack(convs[1]),
            jnp.stack(glas[1]))
```

```python
import functools
import math

import jax
import jax.numpy as jnp
from jax import lax
from jax.experimental import pallas as pl
from jax.experimental.pallas import tpu as pltpu

F32 = jnp.float32
BF16 = jnp.bfloat16

A_HEADS, A_DK, A_DV = 4, 64, 128
B_HEADS, B_DK, B_DV = 4, 128, 128
CONV_W = 4
C_HEADS, C_DK, C_DV = 4, 128, 256
GLA_LOWRANK, GLA_TAU = 16, 16.0
N_BUCKETS, MAX_DIST = 32, 128
EPS = 1e-6
NEG = -1e30
LOG2E = 1.4426950408889634

A_QK = A_HEADS * 2 * A_DK
A_V = A_HEADS * A_DV
B_QK = B_HEADS * B_DK
B_V = B_HEADS * B_DV
B_CONV_DIM = 2 * B_QK + B_V
C_QK = C_HEADS * C_DK
C_V = C_HEADS * C_DV
IN_EVEN_MAIN = 2 * A_QK + A_V + B_CONV_DIM + B_V
IN_ODD_MAIN = 2 * C_QK + 2 * C_V

LANES = 128
SUBLANES = 8
PACKED_SUBLANES = 16
VMEM_LIMIT = 56 * 2 ** 20

ROWS_DENSE = 512
ROWS_INPROJ = 256
ATT_Q_TILE = 1024
ATT_K_TILE = ROWS_INPROJ
VT_ROWS = A_DV + PACKED_SUBLANES
SAFE_SPAN = 96.0
SEQ_CHUNK = 64
DELTA_PREP_CHUNKS = 4
GLA_SUB = 16
PAGES_PER_STEP = 16

NT = (((1,), (1,)), ((), ()))
TN = (((0,), (0,)), ((), ()))


def _cp(*sem):
    return pltpu.CompilerParams(dimension_semantics=sem, vmem_limit_bytes=VMEM_LIMIT)


def _mm(a, b, dn=None):
    if dn is None:
        return jnp.dot(a, b, preferred_element_type=F32)
    return lax.dot_general(a, b, dn, preferred_element_type=F32)


def _split2(x):
    hi = x.astype(BF16)
    lo = (x - hi.astype(F32)).astype(BF16)
    return hi, lo


def _split3(x):
    hi = x.astype(BF16)
    r = x - hi.astype(F32)
    mid = r.astype(BF16)
    lo = (r - mid.astype(F32)).astype(BF16)
    return hi, mid, lo


def _mm3(a, b):
    ah, al = _split2(a)
    bh, bl = _split2(b)
    return _mm(ah, bh) + (_mm(ah, bl) + _mm(al, bh))


def _mm_exact_lhs(a_bf, x):
    hi, mid, lo = _split3(x)
    return _mm(a_bf, hi) + (_mm(a_bf, mid) + _mm(a_bf, lo))


def _sigmoid(x):
    return 1.0 / (1.0 + jnp.exp(-x))


def _silu(x):
    return x * _sigmoid(x)


def _softplus(x):
    return jnp.maximum(x, 0.0) + jnp.log(1.0 + jnp.exp(-jnp.abs(x)))


def _rmsnorm(xf, g):
    return xf * lax.rsqrt(jnp.mean(xf * xf, axis=-1, keepdims=True) + EPS) * g


def _iota(shape, dim):
    return lax.broadcasted_iota(jnp.int32, shape, dim)


def _row_to_col(row, n):
    eye = _iota((n, n), 0) == _iota((n, n), 1)
    return jnp.sum(jnp.where(eye, jnp.broadcast_to(row, (n, n)), 0.0), axis=1, keepdims=True)


def _ffn_kernel(x_ref, g_ref, wg_ref, wu_ref, wo_ref, *rest):
    *mix_refs, o_ref, h_ref = rest
    j = pl.program_id(1)

    @pl.when(j == 0)
    def _():
        xf = x_ref[...]
        if mix_refs:
            wm_ref = mix_refs[-1]
            off = 0
            for m_ref in mix_refs[:-1]:
                n = m_ref.shape[1]
                xf = xf + _mm(m_ref[...], wm_ref[off:off + n, :])
                off += n
        h_ref[...] = _rmsnorm(xf, g_ref[...]).astype(BF16)
        o_ref[...] = xf

    h = h_ref[...]
    gate = _mm(h, wg_ref[...])
    up = _mm(h, wu_ref[...])
    a = (_silu(gate) * up).astype(BF16)
    o_ref[...] += 0.5 * _mm(a, wo_ref[...])


def _ffn(x, g, w_in, w_out, layer, which, tm, mix=(), w_mix=None, li=0):
    m, d = x.shape
    f = w_out.shape[2]
    nf = 2
    tf = f // nf
    assert f % nf == 0 and tf % LANES == 0 and m % tm == 0
    mix_specs = [pl.BlockSpec((tm, o.shape[1]), lambda i, j: (i, 0)) for o in mix]
    if mix:
        assert sum(o.shape[1] for o in mix) == w_mix.shape[1]
        mix_specs.append(pl.BlockSpec((None, w_mix.shape[1], d), lambda i, j: (li, 0, 0)))
    return pl.pallas_call(
        _ffn_kernel,
        grid=(m // tm, nf),
        in_specs=[
            pl.BlockSpec((tm, d), lambda i, j: (i, 0)),
            pl.BlockSpec((None, 1, d), lambda i, j: (layer, 0, 0)),
            pl.BlockSpec((None, None, d, tf), lambda i, j: (layer, which, 0, j)),
            pl.BlockSpec((None, None, d, tf), lambda i, j: (layer, which, 0, j + nf)),
            pl.BlockSpec((None, None, tf, d), lambda i, j: (layer, which, j, 0)),
        ] + mix_specs,
        out_specs=pl.BlockSpec((tm, d), lambda i, j: (i, 0)),
        out_shape=jax.ShapeDtypeStruct((m, d), F32),
        scratch_shapes=[pltpu.VMEM((tm, d), BF16)],
        compiler_params=_cp("parallel", "arbitrary"),
        name="ffn",
    )(x, g, w_in, w_in, w_out, *mix, *([w_mix] if mix else []))


def _group_rms(x, grp_bf, gain, width):
    hi, lo = _split2(x * x)
    ms = (_mm(hi, grp_bf) + _mm(lo, grp_bf)) * (1.0 / width)
    return x * lax.rsqrt(ms + EPS) * gain


def _inproj_even_kernel(x_ref, g_ref, w_ref, ws_ref, grp_ref, qg_ref, kg_ref, gp_ref,
                        q_ref, k32_ref, kbf_ref, v32_ref, vt_ref, bqkv_ref, bz_ref, gb_ref):
    h = _rmsnorm(x_ref[...], g_ref[...]).astype(BF16)
    grp = grp_ref[...]

    aq = _mm(h, w_ref[:, 0:A_QK])
    q = _group_rms(aq, grp, qg_ref[...], A_DK)
    q_ref[...] = (q * (A_DK ** -0.5 * LOG2E)).astype(BF16)

    ak = _mm(h, w_ref[:, A_QK:2 * A_QK])
    k = _group_rms(ak, grp, kg_ref[...], A_DK)
    kbf_ref[...] = k.astype(BF16)

    v = _mm(h, w_ref[:, 2 * A_QK:2 * A_QK + A_V])
    tm = v.shape[0]
    ones = jnp.ones((VT_ROWS - A_DV, tm), BF16)
    for hd in range(A_HEADS):
        k32_ref[pl.ds(hd, tm, stride=A_HEADS), :] = k[:, 2 * A_DK * hd:2 * A_DK * (hd + 1)]
        v32_ref[pl.ds(hd, tm, stride=A_HEADS), :] = v[:, A_DV * hd:A_DV * (hd + 1)]
        vt_ref[hd, 0:A_DV, :] = v[:, A_DV * hd:A_DV * (hd + 1)].T.astype(BF16)
        vt_ref[hd, A_DV:VT_ROWS, :] = ones

    o = 2 * A_QK + A_V
    bqkv_ref[...] = _mm(h, w_ref[:, o:o + B_CONV_DIM])
    o += B_CONV_DIM
    bz_ref[...] = _mm(h, w_ref[:, o:o + B_V])

    y = _mm(h, ws_ref[...])
    lane = _iota(y.shape, 1)
    g = gp_ref[0:1, :] * _softplus(y + gp_ref[1:2, :])
    gb_ref[...] = jnp.where(lane < B_HEADS, g, jnp.where(lane < 2 * B_HEADS, _sigmoid(y), 0.0))


def _inproj_even(x, g, w, ws, grp, qg, kg, gp, layer, li, tm):
    m, d = x.shape
    wmain = IN_EVEN_MAIN
    row = lambda n: pl.BlockSpec((tm, n), lambda i: (i, 0))
    out_shapes = [
        jax.ShapeDtypeStruct((m, A_QK), BF16),
        jax.ShapeDtypeStruct((m * A_HEADS, 2 * A_DK), F32),
        jax.ShapeDtypeStruct((m, A_QK), BF16),
        jax.ShapeDtypeStruct((m * A_HEADS, A_DV), F32),
        jax.ShapeDtypeStruct((A_HEADS, m // tm, VT_ROWS, tm), BF16),
        jax.ShapeDtypeStruct((m, B_CONV_DIM), F32),
        jax.ShapeDtypeStruct((m, B_V), F32),
        jax.ShapeDtypeStruct((m, LANES), F32),
    ]
    out_specs = [row(s.shape[1]) for s in out_shapes]
    out_specs[1] = pl.BlockSpec((tm * A_HEADS, 2 * A_DK), lambda i: (i, 0))
    out_specs[3] = pl.BlockSpec((tm * A_HEADS, A_DV), lambda i: (i, 0))
    out_specs[4] = pl.BlockSpec((A_HEADS, None, VT_ROWS, tm), lambda i: (0, i, 0, 0))
    return pl.pallas_call(
        _inproj_even_kernel,
        grid=(m // tm,),
        in_specs=[
            row(d),
            pl.BlockSpec((None, 1, d), lambda i: (layer, 0, 0)),
            pl.BlockSpec((None, d, wmain), lambda i: (li, 0, 0)),
            pl.BlockSpec((None, d, LANES), lambda i: (li, 0, 0)),
            pl.BlockSpec((A_QK, A_QK), lambda i: (0, 0)),
            pl.BlockSpec((None, 1, A_QK), lambda i: (li, 0, 0)),
            pl.BlockSpec((None, 1, A_QK), lambda i: (li, 0, 0)),
            pl.BlockSpec((None, 2, LANES), lambda i: (li, 0, 0)),
        ],
        out_specs=out_specs,
        out_shape=out_shapes,
        compiler_params=_cp("parallel"),
        name="inproj_even",
    )(x, g, w, ws, grp, qg, kg, gp)


def _bias_kernel(rb_ref, bp_ref, bs_ref, *, tq, tk, past, page, tnew):
    h = pl.program_id(0)
    far = rb_ref[N_BUCKETS - 1, h]
    exact = N_BUCKETS // 2

    def table(rel):
        n = jnp.maximum(rel, 0)
        nf = jnp.maximum(n, 1).astype(F32)
        large = exact + (jnp.log(nf / exact) / math.log(MAX_DIST / exact) * (N_BUCKETS - exact)).astype(jnp.int32)
        bucket = jnp.where(n < exact, n, jnp.minimum(large, N_BUCKETS - 1))
        val = jnp.zeros(rel.shape, F32)
        for b in range(N_BUCKETS):
            val = jnp.where(bucket == b, rb_ref[b, h], val)
        return jnp.where(rel >= 0, (val - far) * LOG2E, NEG)

    rows = SUBLANES
    for n in range(tq // tk + 1):
        def fill(c, carry, n=n):
            r0 = pl.multiple_of(c * rows, rows)
            key_p = (n - 1) * tk + r0 + _iota((rows, tq), 0)
            bp_ref[n, pl.ds(r0, rows), :] = table(_iota((rows, tq), 1) - key_p)
            return carry

        lax.fori_loop(0, tk // rows, fill, 0)

    ncol = page * A_HEADS + LANES
    t_idx = _iota((2 * tnew, ncol), 0) & (tnew - 1)
    col = _iota((2 * tnew, ncol), 1)
    key = past - page + (col >> 2)
    rel_new = jnp.where(key - past < tnew, t_idx - (key - past), -1)
    rel = jnp.where(key < past, past + t_idx - key, rel_new)
    bs_ref[...] = table(jnp.where((col & (A_HEADS - 1)) == h, rel, -1))


def _bias_tiles(rel_bias, tq, tk, past, page, tnew):
    assert tnew & (tnew - 1) == 0 and 2 * tnew == SUBLANES and A_HEADS == 4 and tq % tk == 0
    ncol = page * A_HEADS + LANES
    ntile = tq // tk + 1
    return pl.pallas_call(
        functools.partial(_bias_kernel, tq=tq, tk=tk, past=past, page=page, tnew=tnew),
        grid=(A_HEADS,),
        in_specs=[pl.BlockSpec(memory_space=pltpu.SMEM)],
        out_specs=[
            pl.BlockSpec((None, ntile, tk, tq), lambda h: (h, 0, 0, 0)),
            pl.BlockSpec((None, 2 * tnew, ncol), lambda h: (h, 0, 0)),
        ],
        out_shape=[
            jax.ShapeDtypeStruct((A_HEADS, ntile, tk, tq), F32),
            jax.ShapeDtypeStruct((A_HEADS, 2 * tnew, ncol), F32),
        ],
        compiler_params=_cp("parallel"),
        name="t5_bias",
    )(rel_bias)


def _far_bucket_saturated(dist):
    exact = N_BUCKETS // 2
    v = math.log(dist / exact) / math.log(MAX_DIST / exact) * (N_BUCKETS - exact)
    return dist >= exact and exact + v >= N_BUCKETS - 1 + 1e-3


def _attn_prompt_kernel(sc_ref, q_ref, k_ref, vt_ref, bias_ref, sg_ref, o_ref, *, tq, tk):
    head = pl.program_id(1)
    i = pl.program_id(2)
    q = q_ref[...]
    lane = _iota(q.shape, 1)
    zero = jnp.zeros_like(q)
    qbd = jnp.concatenate([jnp.where(lane < A_DK, q, zero), jnp.where(lane >= A_DK, q, zero)], axis=0)
    ratio = tq // tk

    def biased(j, bias):
        start = pl.multiple_of(j * tk, tk)
        s = _mm(k_ref[pl.ds(start, tk), :], qbd, NT)
        return s if bias is None else s + jnp.concatenate([bias, bias], axis=1)

    def key_blocks(step, carry):
        first = i * ratio
        nfar = jnp.maximum(first - 1, 0)
        per_trip = 4

        def trip(j, c):
            for u in range(per_trip):
                c = step(per_trip * j + u, c, None)
            return c

        def overlapping(c):
            for n in range(ratio):
                c = step(first + n, c, bias_ref[n + 1])
            return c

        carry = lax.fori_loop(0, nfar // per_trip, trip, carry)
        carry = lax.fori_loop(nfar - nfar % per_trip, nfar, lambda j, c: step(j, c, None), carry)
        return lax.cond(i >= 1, lambda c: overlapping(step(first - 1, c, bias_ref[0])), overlapping, carry)

    def fixed_shift(_):
        shift = sc_ref[3 + head]

        def step(j, acc, bias):
            return acc + _mm(vt_ref[j], jnp.exp2(biased(j, bias) - shift).astype(BF16))

        return key_blocks(step, jnp.zeros((VT_ROWS, 2 * tq), F32))

    def running_max(_):
        def step(j, c, bias):
            m, acc = c
            s = biased(j, bias)
            m_new = jnp.maximum(m, jnp.max(s, axis=0, keepdims=True))
            p = jnp.exp2(s - m_new).astype(BF16)
            return m_new, jnp.exp2(m - m_new) * acc + _mm(vt_ref[j], p)

        return key_blocks(step, (jnp.full((1, 2 * tq), NEG, F32), jnp.zeros((VT_ROWS, 2 * tq), F32)))[1]

    acc = lax.cond(sc_ref[2] > 0.5, fixed_shift, running_max, 0)

    o = acc[0:A_DV] / acc[A_DV:A_DV + 1]
    o = o[:, 0:tq] - sc_ref[0] * o[:, tq:2 * tq]
    o = o * lax.rsqrt(jnp.mean(o * o, axis=0, keepdims=True) + EPS)
    o_ref[...] = (o.T * (sg_ref[...] * sc_ref[1])).astype(BF16)


def _attn_prompt(sc, q, k, vt, bias_p, sg, li, tq):
    bsz, t, _ = q.shape
    tk = vt.shape[3]
    assert tq % tk == 0 and t % tq == 0 and bias_p.shape[1:] == (tq // tk + 1, tk, tq)
    return pl.pallas_call(
        functools.partial(_attn_prompt_kernel, tq=tq, tk=tk),
        grid=(bsz, A_HEADS, t // tq),
        in_specs=[
            pl.BlockSpec(memory_space=pltpu.SMEM),
            pl.BlockSpec((None, tq, 2 * A_DK), lambda b, h, i: (b, i, h)),
            pl.BlockSpec((None, t, 2 * A_DK), lambda b, h, i: (b, 0, h)),
            pl.BlockSpec((None, t // tk, VT_ROWS, tk), lambda b, h, i: (h, b, 0, 0)),
            pl.BlockSpec((None, tq // tk + 1, tk, tq), lambda b, h, i: (h, 0, 0, 0)),
            pl.BlockSpec((None, 1, A_DV), lambda b, h, i: (li, 0, 0)),
        ],
        out_specs=pl.BlockSpec((None, tq, A_DV), lambda b, h, i: (b, i, h)),
        out_shape=jax.ShapeDtypeStruct((bsz, t, A_V), BF16),
        compiler_params=_cp("parallel", "parallel", "arbitrary"),
        name="attn_prompt",
    )(sc, q, k, vt, bias_p, sg)


def _attn_sample_kernel(pt_ref, sc_ref, qbd_ref, bias_ref, biasn_ref, kn_ref, vn_ref, sg_ref, *rest, npg, tnew):
    k_refs = rest[:npg]
    v_refs = rest[npg:2 * npg]
    o_ref, m_ref, l_ref, acc_ref, mask_ref = rest[2 * npg:]
    pg = pl.program_id(1)
    last = pg == pl.num_programs(1) - 1
    qbd = qbd_ref[...]

    @pl.when(pg == 0)
    def _():
        m_ref[...] = jnp.full(m_ref.shape, NEG, F32)
        l_ref[...] = jnp.zeros(l_ref.shape, F32)
        acc_ref[...] = jnp.zeros(acc_ref.shape, F32)
        row_head = _iota(mask_ref.shape, 0) >> ((2 * tnew).bit_length() - 1)
        col_head = _iota(mask_ref.shape, 1) & (A_HEADS - 1)
        mask_ref[...] = jnp.where(row_head == col_head, 0.0, NEG)

    def update(s, vs):
        m_old = m_ref[...]
        m_new = jnp.maximum(m_old, jnp.max(s, axis=1, keepdims=True))
        alpha = jnp.exp2(m_old - m_new)
        p = jnp.exp2(s - m_new)
        l_ref[...] = alpha * l_ref[...] + jnp.sum(p, axis=1, keepdims=True)
        pb = p.astype(BF16)
        w = pb.shape[1] // len(vs)
        pv = _mm(pb[:, 0:w], vs[0])
        for g in range(1, len(vs)):
            pv = pv + _mm(pb[:, g * w:(g + 1) * w], vs[g])
        acc_ref[...] = alpha * acc_ref[...] + pv
        m_ref[...] = m_new

    mask = mask_ref[...]
    final_page = jnp.where(last, bias_ref[...], mask)
    s = jnp.concatenate([_mm(qbd, kr[...].astype(BF16), NT) + (mask if g < npg - 1 else final_page)
                         for g, kr in enumerate(k_refs)], axis=1)
    update(s, [vr[...].astype(BF16) for vr in v_refs])

    @pl.when(last)
    def _():
        update(_mm(qbd, kn_ref[...], NT) + biasn_ref[...], [vn_ref[...]])
        lam = sc_ref[0]
        rows = 2 * tnew
        for h in range(A_HEADS):
            a = acc_ref[rows * h:rows * (h + 1), :] / l_ref[rows * h:rows * (h + 1), :]
            o = a - lam * pltpu.roll(a, tnew, axis=0)
            o_ref[:, A_DV * h:A_DV * (h + 1)] = _rmsnorm(o, sg_ref[...]) * sc_ref[1]


def _attn_sample(page_table, sc, qbd, bias_s, knew, vnew, sg, cache_k, cache_v, li, tnew):
    bsz, nrow, _ = qbd.shape
    n_pages = page_table.shape[1]
    npg = PAGES_PER_STEP
    assert n_pages % npg == 0
    prow = cache_k.shape[2]
    assert bias_s.shape == (nrow, prow + LANES)

    def kvspec(g):
        return pl.BlockSpec((None, None, prow, 2 * A_DK), lambda b, pg, pt: (li, pt[b, pg * npg + g], 0, 0))

    grid_spec = pltpu.PrefetchScalarGridSpec(
        num_scalar_prefetch=1,
        grid=(bsz, n_pages // npg),
        in_specs=[
            pl.BlockSpec(memory_space=pltpu.SMEM),
            pl.BlockSpec((None, nrow, 2 * A_DK), lambda b, pg, pt: (b, 0, 0)),
            pl.BlockSpec((nrow, prow), lambda b, pg, pt: (0, 0)),
            pl.BlockSpec((nrow, LANES), lambda b, pg, pt: (0, prow // LANES)),
            pl.BlockSpec((None, LANES, 2 * A_DK), lambda b, pg, pt: (b, 0, 0)),
            pl.BlockSpec((None, LANES, A_DV), lambda b, pg, pt: (b, 0, 0)),
            pl.BlockSpec((None, 1, A_DV), lambda b, pg, pt: (li, 0, 0)),
        ] + [kvspec(g) for g in range(npg)] + [kvspec(g) for g in range(npg)],
        out_specs=pl.BlockSpec((None, 2 * tnew, A_V), lambda b, pg, pt: (b, 0, 0)),
        scratch_shapes=[pltpu.VMEM((nrow, 1), F32), pltpu.VMEM((nrow, 1), F32), pltpu.VMEM((nrow, A_DV), F32),
                        pltpu.VMEM((nrow, prow), F32)],
    )
    return pl.pallas_call(
        functools.partial(_attn_sample_kernel, npg=npg, tnew=tnew),
        grid_spec=grid_spec,
        out_shape=jax.ShapeDtypeStruct((bsz, 2 * tnew, A_V), F32),
        compiler_params=_cp("parallel", "arbitrary"),
        name="attn_sample",
    )(page_table, sc, qbd, bias_s, bias_s, knew, vnew, sg, *([cache_k] * npg), *([cache_v] * npg))


def _l2norm(x):
    return x * lax.rsqrt(jnp.sum(x * x, axis=-1, keepdims=True) + EPS)


def _unit_lower_inverses(mats, c):
    n = mats[0].shape[0]
    nblk = n // c
    shift = c.bit_length() - 1
    same = (_iota((n, n), 0) >> shift) == (_iota((n, n), 1) >> shift)
    eye_w = ((_iota((c, n), 1) & (c - 1)) == _iota((c, n), 0)).astype(F32)

    def wide(sq):
        out = sq[0:c]
        for r in range(1, nblk):
            out = out + sq[r * c:(r + 1) * c]
        return out

    def square(w):
        return jnp.where(same, jnp.concatenate([w] * nblk, axis=0), jnp.zeros((), w.dtype))

    def mm3_split(lhs_split, rhs_split):
        (lh, ll), (rh, rl) = lhs_split, rhs_split
        return _mm(lh, rh) + (_mm(lh, rl) + _mm(ll, rh))

    qw = [_split2(wide(a)) for a in mats]
    qsq = [(square(h), square(l)) for h, l in qw]
    ps = [eye_w - wide(a) for a in mats]
    span = 2
    while span < c:
        qw = [_split2(mm3_split(q, s)) for q, s in zip(qw, qsq)]
        qsq = [(square(h), square(l)) for h, l in qw]
        ps = [p + mm3_split(_split2(p), s) for p, s in zip(ps, qsq)]
        span *= 2
    return [square(p) for p in ps]


def _cumsum_rows(x, n):
    row = _iota(x.shape, 0)
    k = 1
    while k < n:
        x = x + jnp.where(row >= k, pltpu.roll(x, k, axis=0), 0.0)
        k *= 2
    return x


def _delta_prep_kernel(x_ref, halo_ref, c0_ref, gb_ref, cw_ref, u_ref, wk_ref, qe_ref, kd_ref, qk_ref, xbuf,
                       *, c, nck):
    ci = pl.program_id(1)
    ch = B_HEADS * c
    rows = nck * c
    xbuf[0:SUBLANES, :] = jnp.where(ci == 0, c0_ref[...], halo_ref[...])
    xbuf[SUBLANES:SUBLANES + rows, :] = x_ref[...]
    first = SUBLANES - (CONV_W - 1)
    conv = xbuf[pl.ds(first, rows), :] * cw_ref[0:1, :]
    for j in range(1, CONV_W):
        conv = conv + xbuf[pl.ds(first + j, rows), :] * cw_ref[j:j + 1, :]
    conv = _silu(conv)

    row = _iota((ch, ch), 0)
    col = _iota((ch, ch), 1)
    shift = c.bit_length() - 1
    same = (row >> shift) == (col >> shift)
    incl = same & (row >= col)
    strict = same & (row > col)

    def stack_lane(a, off):
        return jnp.concatenate([a[:, off + h:off + h + 1] for h in range(B_HEADS)], axis=0)

    parts = []
    for n in range(nck):
        rs = slice(n * c, (n + 1) * c)

        def stack(off, width):
            return jnp.concatenate([conv[rs, off + width * h:off + width * (h + 1)] for h in range(B_HEADS)], axis=0)

        q = _l2norm(stack(0, B_DK)) * (B_DK ** -0.5)
        k = _l2norm(stack(B_QK, B_DK))
        v = stack(2 * B_QK, B_DV)
        gb = gb_ref[rs, :]
        gcum = _cumsum_rows(gb, c)
        gcol = stack_lane(gcum, 0)
        beta = stack_lane(gb, B_HEADS)
        glast = stack_lane(jnp.broadcast_to(gcum[c - 1:c, :], gcum.shape), 0)
        grow = jnp.broadcast_to(gcol, (ch, LANES)).T[0:1, :]
        dec = jnp.where(incl, jnp.exp(jnp.where(incl, gcol - grow, 0.0)), 0.0)
        kb = k.astype(BF16)
        a = jnp.where(strict, _mm(kb, kb, NT) * dec * beta, 0.0)
        egc = jnp.exp(gcol)
        qe_ref[n] = (q * egc).astype(BF16)
        kd_ref[n] = (k * jnp.exp(glast - gcol)).astype(BF16)
        qk_ref[n] = (_mm(q.astype(BF16), kb, NT) * dec).astype(BF16)
        parts.append((a, jnp.concatenate([v * beta, k * (beta * egc)], axis=1)))

    t_invs = _unit_lower_inverses([a for a, _ in parts], c)
    for n in range(nck):
        uw = _mm3(t_invs[n], parts[n][1])
        u_ref[n] = uw[:, 0:B_DV]
        wk_ref[n] = uw[:, B_DV:B_DV + B_DK].astype(BF16)


def _delta_scan_kernel(u_ref, wk_ref, qe_ref, kd_ref, qk_ref, gb_ref, z_ref, s0_ref, bn_ref, ob_ref, sout_ref,
                       s_scr, *, c, nb):
    ci = pl.program_id(1)

    @pl.when(ci == 0)
    def _():
        s_scr[...] = s0_ref[...]

    hs = [slice(c * h, c * (h + 1)) for h in range(B_HEADS)]
    for b in range(nb):
        decay = jnp.exp(jnp.sum(gb_ref[b], axis=0, keepdims=True))
        s = [s_scr[b, h] for h in range(B_HEADS)]
        s_bf = [x.astype(BF16) for x in s]
        v_new = u_ref[b] - jnp.concatenate([_mm(wk_ref[b, hs[h], :], s_bf[h]) for h in range(B_HEADS)], axis=0)
        vn_bf = v_new.astype(BF16)
        o = (jnp.concatenate([_mm(qe_ref[b, hs[h], :], s_bf[h]) for h in range(B_HEADS)], axis=0)
             + _mm(qk_ref[b], vn_bf))
        for h in range(B_HEADS):
            s_scr[b, h] = s[h] * decay[:, h:h + 1] + _mm(kd_ref[b, hs[h], :], vn_bf[hs[h]], TN)
            sl = slice(B_DV * h, B_DV * (h + 1))
            ob_ref[b, :, sl] = (_rmsnorm(o[hs[h]], bn_ref[...]) * _silu(z_ref[b, :, sl])).astype(BF16)

    @pl.when(ci == pl.num_programs(1) - 1)
    def _():
        sout_ref[...] = s_scr[...]


def _delta(bqkv, bz, gb, conv0, s0, cw, bn, li, c):
    bsz, t, _ = bqkv.shape
    assert t % c == 0 and c % SUBLANES == 0 and c & (c - 1) == 0
    nch = t // c
    ch = B_HEADS * c
    nck = min(DELTA_PREP_CHUNKS, nch)
    assert nch % nck == 0
    hb = nck * c // SUBLANES
    tok = lambda n: pl.BlockSpec((None, c, n), lambda b, ci: (b, ci, 0))
    step_tok = lambda n: pl.BlockSpec((None, nck * c, n), lambda b, ci: (b, ci, 0))
    step_chunks = lambda n: pl.BlockSpec((None, nck, ch, n), lambda b, ci: (b, ci, 0, 0))
    chunk_shape = lambda n, dt: jax.ShapeDtypeStruct((bsz, nch, ch, n), dt)
    u, wk, qe, kd, qk = pl.pallas_call(
        functools.partial(_delta_prep_kernel, c=c, nck=nck),
        grid=(bsz, nch // nck),
        in_specs=[
            step_tok(B_CONV_DIM),
            pl.BlockSpec((None, SUBLANES, B_CONV_DIM), lambda b, ci: (b, jnp.maximum(ci * hb - 1, 0), 0)),
            pl.BlockSpec((None, SUBLANES, B_CONV_DIM), lambda b, ci: (b, 0, 0)),
            step_tok(LANES),
            pl.BlockSpec((None, CONV_W, B_CONV_DIM), lambda b, ci: (li, 0, 0)),
        ],
        out_specs=[step_chunks(B_DV), step_chunks(B_DK), step_chunks(B_DK), step_chunks(B_DK), step_chunks(ch)],
        out_shape=[chunk_shape(B_DV, F32), chunk_shape(B_DK, BF16), chunk_shape(B_DK, BF16),
                   chunk_shape(B_DK, BF16), chunk_shape(ch, BF16)],
        scratch_shapes=[pltpu.VMEM((SUBLANES + nck * c, B_CONV_DIM), F32)],
        compiler_params=_cp("parallel", "parallel"),
        name="delta_prep",
    )(bqkv, bqkv, conv0, gb, cw)

    nb = 2
    assert bsz % nb == 0
    grp_chunk = lambda n: pl.BlockSpec((nb, None, ch, n), lambda b, ci: (b, ci, 0, 0))
    grp_tok = lambda n: pl.BlockSpec((nb, c, n), lambda b, ci: (b, ci, 0))
    state = pl.BlockSpec((nb, B_HEADS, B_DK, B_DV), lambda b, ci: (b, 0, 0, 0))
    return pl.pallas_call(
        functools.partial(_delta_scan_kernel, c=c, nb=nb),
        grid=(bsz // nb, nch),
        in_specs=[grp_chunk(B_DV), grp_chunk(B_DK), grp_chunk(B_DK), grp_chunk(B_DK), grp_chunk(ch),
                  grp_tok(LANES), grp_tok(B_V), state,
                  pl.BlockSpec((None, 1, B_DV), lambda b, ci: (li, 0, 0))],
        out_specs=[grp_tok(B_V), state],
        out_shape=[jax.ShapeDtypeStruct((bsz, t, B_V), BF16),
                   jax.ShapeDtypeStruct((bsz, B_HEADS, B_DK, B_DV), F32)],
        scratch_shapes=[pltpu.VMEM((nb, B_HEADS, B_DK, B_DV), F32)],
        compiler_params=_cp("parallel", "arbitrary"),
        name="delta_scan",
    )(u, wk, qe, kd, qk, gb, bz, s0, bn)


def _inproj_odd_kernel(x_ref, g_ref, w_ref, ws_ref, w2_ref, gbias_ref, q_ref, k_ref, v_ref, r_ref, gk_ref):
    h = _rmsnorm(x_ref[...], g_ref[...]).astype(BF16)
    q_ref[...] = _mm(h, w_ref[:, 0:C_QK])
    k_ref[...] = _mm(h, w_ref[:, C_QK:2 * C_QK])
    v_ref[...] = _mm(h, w_ref[:, 2 * C_QK:2 * C_QK + C_V]).astype(BF16)
    r_ref[...] = _mm(h, w_ref[:, 2 * C_QK + C_V:2 * C_QK + 2 * C_V])
    glr = _mm(h, ws_ref[...]).astype(BF16)
    y = _mm(glr, w2_ref[...]) + gbias_ref[...]
    gk_ref[...] = -_softplus(-y) * (1.0 / GLA_TAU)


def _inproj_odd(x, g, w, ws, w2, gbias, layer, li, tm):
    m, d = x.shape
    wmain = 2 * C_QK + 2 * C_V
    row = lambda n: pl.BlockSpec((tm, n), lambda i: (i, 0))
    out_shapes = [
        jax.ShapeDtypeStruct((m, C_QK), F32),
        jax.ShapeDtypeStruct((m, C_QK), F32),
        jax.ShapeDtypeStruct((m, C_V), BF16),
        jax.ShapeDtypeStruct((m, C_V), F32),
        jax.ShapeDtypeStruct((m, C_QK), F32),
    ]
    return pl.pallas_call(
        _inproj_odd_kernel,
        grid=(m // tm,),
        in_specs=[
            row(d),
            pl.BlockSpec((None, 1, d), lambda i: (layer, 0, 0)),
            pl.BlockSpec((None, d, wmain), lambda i: (li, 0, 0)),
            pl.BlockSpec((None, d, LANES), lambda i: (li, 0, 0)),
            pl.BlockSpec((None, LANES, C_QK), lambda i: (li, 0, 0)),
            pl.BlockSpec((None, 1, C_QK), lambda i: (li, 0, 0)),
        ],
        out_specs=[row(s.shape[1]) for s in out_shapes],
        out_shape=out_shapes,
        compiler_params=_cp("parallel"),
        name="inproj_odd",
    )(x, g, w, ws, w2, gbias)


def _gla_kernel(q_ref, k_ref, v_ref, r_ref, gk_ref, s0_ref, cn_ref, o_ref, sout_ref, s_scr, *, c, sub):
    ci = pl.program_id(1)

    @pl.when(ci == 0)
    def _():
        s_scr[...] = s0_ref[...]

    row = _iota((c, c), 0)
    col = _iota((c, c), 1)
    lower_bf = (row >= col).astype(BF16)
    in_sub = row & (sub - 1)
    sub_start = row - in_sub
    b_all = _mm_exact_lhs(lower_bf, gk_ref[...])

    for h in range(C_HEADS):
        sl = slice(C_DK * h, C_DK * (h + 1))
        vl = slice(C_DV * h, C_DV * (h + 1))
        q = q_ref[:, sl] * (C_DK ** -0.5)
        k = k_ref[:, sl]
        b = b_all[:, sl]
        v = v_ref[:, vl]
        s = s_scr[h]

        o = _mm((q * jnp.exp(b)).astype(BF16), s.astype(BF16))

        blocks = [jnp.zeros((sub, c), F32)]
        for r in range(1, c // sub):
            ref = b[sub * r:sub * r + 1, :]
            qt = q[sub * r:sub * (r + 1)] * jnp.exp(b[sub * r:sub * (r + 1)] - ref)
            kt = k * jnp.exp(jnp.minimum(ref - b, 0.0))
            blocks.append(_mm(qt.astype(BF16), kt.astype(BF16), NT))
        a = jnp.where(col < sub_start, jnp.concatenate(blocks, axis=0), 0.0)
        b2 = b * LOG2E
        for d in range(sub):
            kd = k if d == 0 else pltpu.roll(k, d, axis=0)
            bd = b2 if d == 0 else pltpu.roll(b2, d, axis=0)
            ad = jnp.sum(q * kd * jnp.exp2(b2 - bd), axis=1, keepdims=True)
            a = jnp.where((col == row - d) & (in_sub >= d), ad, a)
        o = o + _mm(a.astype(BF16), v)

        blast = b[c - 1:c, :]
        kdec = (k * jnp.exp(blast - b)).astype(BF16)
        s_scr[h] = s * _row_to_col(jnp.exp(blast), C_DK) + _mm(kdec, v, TN)

        o_ref[:, vl] = (_rmsnorm(o, cn_ref[...]) * _silu(r_ref[:, vl])).astype(BF16)

    @pl.when(ci == pl.num_programs(1) - 1)
    def _():
        sout_ref[...] = s_scr[...]


def _gla(q, k, v, r, gk, s0, cn, li, c):
    bsz, t, _ = q.shape
    assert t % c == 0
    sub = min(GLA_SUB, c)
    tok = lambda n: pl.BlockSpec((None, c, n), lambda b, ci: (b, ci, 0))
    state = pl.BlockSpec((None, C_HEADS, C_DK, C_DV), lambda b, ci: (b, 0, 0, 0))
    return pl.pallas_call(
        functools.partial(_gla_kernel, c=c, sub=sub),
        grid=(bsz, t // c),
        in_specs=[tok(C_QK), tok(C_QK), tok(C_V), tok(C_V), tok(C_QK), state,
                  pl.BlockSpec((None, 1, C_DV), lambda b, ci: (li, 0, 0))],
        out_specs=[tok(C_V), state],
        out_shape=[jax.ShapeDtypeStruct((bsz, t, C_V), BF16),
                   jax.ShapeDtypeStruct((bsz, C_HEADS, C_DK, C_DV), F32)],
        scratch_shapes=[pltpu.VMEM((C_HEADS, C_DK, C_DV), F32)],
        compiler_params=_cp("parallel", "arbitrary"),
        name="gla",
    )(q, k, v, r, gk, s0, cn)


def _pad_rows(a, rows):
    return jnp.pad(a, ((0, 0), (0, rows - a.shape[1]), (0, 0)))


def _pad_lanes(a):
    return jnp.pad(a, [(0, 0)] * (a.ndim - 1) + [(0, LANES - a.shape[-1])])


def kernel(x_prompt, x_sample, cache_k, cache_v, state_delta, state_conv, state_gla, page_table, rel_bias, norm_ffn1, norm_mix, norm_ffn2, ffn_in, ffn_out, w_in_even, a_qk_norm, a_lambda, a_subln, b_conv, b_a_log, b_dt_bias, b_norm, w_out_even, w_in_odd, c_gate_w2, c_gate_b, c_norm, w_out_odd):
    bp, tp, d = x_prompt.shape
    bs, ts, _ = x_sample.shape
    depth = norm_ffn1.shape[0]
    n_att = w_in_even.shape[0]
    n_pool, page = cache_k.shape[1], cache_k.shape[2]
    past = page_table.shape[1] * page
    tq = ATT_Q_TILE
    cs = SUBLANES
    assert page == LANES and tp % tq == 0 and ts <= cs and ts >= CONV_W - 1
    assert _far_bucket_saturated(ATT_K_TILE + 1) and _far_bucket_saturated(page + 1)

    ffn_in_bf = ffn_in.astype(BF16)
    ffn_out_bf = ffn_out.astype(BF16)
    w_even_bf = w_in_even.astype(BF16)
    w_even_small = _pad_lanes(w_in_even[:, :, IN_EVEN_MAIN:]).astype(BF16)
    w_out_even_bf = w_out_even.astype(BF16)
    w_odd_bf = w_in_odd.astype(BF16)
    w_odd_small = _pad_lanes(w_in_odd[:, :, IN_ODD_MAIN:]).astype(BF16)
    w_out_odd_bf = w_out_odd.astype(BF16)
    gate_w2 = jnp.pad(c_gate_w2, ((0, 0), (0, LANES - GLA_LOWRANK), (0, 0))).astype(BF16)
    gate_b = c_gate_b[:, None, :]
    n1 = norm_ffn1[:, None, :]
    nm = norm_mix[:, None, :]
    n2 = norm_ffn2[:, None, :]
    grp = jnp.kron(jnp.eye(2 * A_HEADS, dtype=F32), jnp.ones((A_DK, A_DK), F32)).astype(BF16)
    qgain = jnp.tile(a_qk_norm[:, 0:1, :], (1, 1, 2 * A_HEADS))
    kgain = jnp.tile(a_qk_norm[:, 1:2, :], (1, 1, 2 * A_HEADS))
    gparams = jnp.stack([_pad_lanes(-jnp.exp(b_a_log)), _pad_lanes(b_dt_bias)], axis=1)
    subln = a_subln[:, None, :]
    bnorm = b_norm[:, None, :]
    cnorm = c_norm[:, None, :]
    cache_k2 = cache_k.reshape(n_att, n_pool, page * A_HEADS, 2 * A_DK)
    cache_v2 = cache_v.reshape(n_att, n_pool, page * A_HEADS, A_DV)

    bias_l2 = (rel_bias - rel_bias[N_BUCKETS - 1:N_BUCKETS]) * LOG2E
    bias_p, bias_s = _bias_tiles(rel_bias, tq, ATT_K_TILE, past, page, ts)
    bias_s = bias_s.reshape(A_HEADS * 2 * ts, page * A_HEADS + LANES)

    xp = x_prompt.reshape(bp * tp, d)
    xs = x_sample.reshape(bs * ts, d)
    tm_p = ROWS_DENSE
    tm_s = bs * ts
    k_rows = [[], []]
    v_rows = [[], []]
    deltas = [[], []]
    convs = [[], []]
    glas = [[], []]

    for i in range(depth):
        li = i // 2
        xp = _ffn(xp, n1, ffn_in_bf, ffn_out_bf, i, 0, tm_p)
        xs = _ffn(xs, n1, ffn_in_bf, ffn_out_bf, i, 0, tm_s)
        if i % 2 == 0:
            lam_init = 0.8 - 0.6 * math.exp(-0.3 * i)
            lq1, lk1, lq2, lk2 = a_lambda[li]
            lam = jnp.exp(jnp.sum(lq1 * lk1)) - jnp.exp(jnp.sum(lq2 * lk2)) + lam_init
            smax = (1.02 * A_DK * A_DK ** -0.5 * LOG2E) * jnp.max(jnp.abs(a_qk_norm[li, 0])) * jnp.max(
                jnp.abs(a_qk_norm[li, 1]))
            shift = smax + jnp.max(bias_l2, axis=0)
            fast = jnp.all(2.0 * smax + jnp.max(bias_l2, axis=0) - jnp.min(bias_l2, axis=0) <= SAFE_SPAN)
            sc = jnp.concatenate([jnp.stack([lam, jnp.asarray(1.0 - lam_init, F32), fast.astype(F32)]),
                                  shift]).astype(F32)

            q, k32, kbf, v32, vt, bqkv, bz, gb = _inproj_even(
                xp, nm, w_even_bf, w_even_small, grp, qgain, kgain, gparams, i, li, ROWS_INPROJ)
            oa = _attn_prompt(sc, q.reshape(bp, tp, A_QK), kbf.reshape(bp, tp, A_QK), vt, bias_p, subln, li, tq)
            bqkv3 = bqkv.reshape(bp, tp, B_CONV_DIM)
            ob, s_new = _delta(bqkv3, bz.reshape(bp, tp, B_V), gb.reshape(bp, tp, LANES),
                               jnp.zeros((bp, SUBLANES, B_CONV_DIM), F32),
                               jnp.zeros((bp, B_HEADS, B_DK, B_DV), F32), b_conv, bnorm, li, SEQ_CHUNK)
            mix_p, w_mix = (oa.reshape(bp * tp, A_V), ob.reshape(bp * tp, B_V)), w_out_even_bf
            k_rows[0].append(k32.reshape(bp, tp, A_HEADS, 2 * A_DK))
            v_rows[0].append(v32.reshape(bp, tp, A_HEADS, A_DV))
            deltas[0].append(s_new)
            convs[0].append(bqkv3[:, tp - (CONV_W - 1):, :])

            q, k32, kbf, v32, _, bqkv, bz, gb = _inproj_even(
                xs, nm, w_even_bf, w_even_small, grp, qgain, kgain, gparams, i, li, tm_s)
            q4 = q.reshape(bs, ts, A_HEADS, 2, A_DK)
            qbd = jnp.einsum('bthcd,ce->bhcted', q4, jnp.eye(2, dtype=BF16)).reshape(bs, A_HEADS * 2 * ts, 2 * A_DK)
            knew = _pad_rows(kbf.reshape(bs, ts * A_HEADS, 2 * A_DK), LANES)
            vnew = _pad_rows(v32.reshape(bs, ts * A_HEADS, A_DV).astype(BF16), LANES)
            oa = _attn_sample(page_table, sc, qbd, bias_s, knew, vnew, subln, cache_k2, cache_v2, li, ts)
            oa = oa[:, :ts, :].astype(BF16)
            bqkv3 = bqkv.reshape(bs, ts, B_CONV_DIM)
            conv0 = jnp.pad(state_conv[li], ((0, 0), (SUBLANES - (CONV_W - 1), 0), (0, 0)))
            ob, s_new = _delta(_pad_rows(bqkv3, cs), _pad_rows(bz.reshape(bs, ts, B_V), cs),
                               _pad_rows(gb.reshape(bs, ts, LANES), cs), conv0, state_delta[li],
                               b_conv, bnorm, li, cs)
            mix_s = (oa.reshape(bs * ts, A_V), ob[:, :ts, :].reshape(bs * ts, B_V))
            k_rows[1].append(k32.reshape(bs, ts, A_HEADS, 2 * A_DK))
            v_rows[1].append(v32.reshape(bs, ts, A_HEADS, A_DV))
            deltas[1].append(s_new)
            convs[1].append(jnp.concatenate([state_conv[li], bqkv3], axis=1)[:, ts:, :])
        else:
            q, k, v, r, gk = _inproj_odd(xp, nm, w_odd_bf, w_odd_small, gate_w2, gate_b, i, li, ROWS_INPROJ)
            o, s_new = _gla(q.reshape(bp, tp, C_QK), k.reshape(bp, tp, C_QK), v.reshape(bp, tp, C_V),
                            r.reshape(bp, tp, C_V), gk.reshape(bp, tp, C_QK),
                            jnp.zeros((bp, C_HEADS, C_DK, C_DV), F32), cnorm, li, SEQ_CHUNK)
            mix_p, w_mix = (o.reshape(bp * tp, C_V),), w_out_odd_bf
            glas[0].append(s_new)

            q, k, v, r, gk = _inproj_odd(xs, nm, w_odd_bf, w_odd_small, gate_w2, gate_b, i, li, tm_s)
            pad = lambda a, n: _pad_rows(a.reshape(bs, ts, n), cs)
            o, s_new = _gla(pad(q, C_QK), pad(k, C_QK), pad(v, C_V), pad(r, C_V), pad(gk, C_QK),
                            state_gla[li], cnorm, li, cs)
            mix_s = (o[:, :ts, :].reshape(bs * ts, C_V),)
            glas[1].append(s_new)
        xp = _ffn(xp, n2, ffn_in_bf, ffn_out_bf, i, 1, tm_p, mix_p, w_mix, li)
        xs = _ffn(xs, n2, ffn_in_bf, ffn_out_bf, i, 1, tm_s, mix_s, w_mix, li)

    return (xp.reshape(bp, tp, d), xs.reshape(bs, ts, d),
            jnp.stack(k_rows[0]), jnp.stack(v_rows[0]), jnp.stack(deltas[0]), jnp.stack(convs[0]),
            jnp.stack(glas[0]),
            jnp.stack(k_rows[1]), jnp.stack(v_rows[1]), jnp.stack(deltas[1]), jnp.stack(convs[1]),
            jnp.stack(glas[1]))
```

```python
import functools
import math

import jax
import jax.numpy as jnp
from jax import lax
from jax.experimental import pallas as pl
from jax.experimental.pallas import tpu as pltpu

F32 = jnp.float32
BF16 = jnp.bfloat16

A_HEADS, A_DK, A_DV = 4, 64, 128
B_HEADS, B_DK, B_DV = 4, 128, 128
CONV_W = 4
C_HEADS, C_DK, C_DV = 4, 128, 256
GLA_LOWRANK, GLA_TAU = 16, 16.0
N_BUCKETS, MAX_DIST = 32, 128
EPS = 1e-6
NEG = -1e30
LOG2E = 1.4426950408889634

A_QK = A_HEADS * 2 * A_DK
A_V = A_HEADS * A_DV
B_QK = B_HEADS * B_DK
B_V = B_HEADS * B_DV
B_CONV_DIM = 2 * B_QK + B_V
C_QK = C_HEADS * C_DK
C_V = C_HEADS * C_DV
IN_EVEN_MAIN = 2 * A_QK + A_V + B_CONV_DIM + B_V
IN_ODD_MAIN = 2 * C_QK + 2 * C_V

LANES = 128
SUBLANES = 8
PACKED_SUBLANES = 16
VMEM_LIMIT = 56 * 2 ** 20

ROWS_DENSE = 512
ROWS_INPROJ = 256
ATT_Q_TILE = 1024
ATT_K_TILE = ROWS_INPROJ
VT_ROWS = A_DV + PACKED_SUBLANES
SAFE_SPAN = 96.0
SEQ_CHUNK = 64
DELTA_PREP_CHUNKS = 4
DELTA_SCAN_CHUNKS = 4
GLA_SUB = 16
PAGES_PER_STEP = 16

NT = (((1,), (1,)), ((), ()))
TN = (((0,), (0,)), ((), ()))


def _cp(*sem):
    return pltpu.CompilerParams(dimension_semantics=sem, vmem_limit_bytes=VMEM_LIMIT)


def _mm(a, b, dn=None):
    if dn is None:
        return jnp.dot(a, b, preferred_element_type=F32)
    return lax.dot_general(a, b, dn, preferred_element_type=F32)


def _split2(x):
    hi = x.astype(BF16)
    lo = (x - hi.astype(F32)).astype(BF16)
    return hi, lo


def _split3(x):
    hi = x.astype(BF16)
    r = x - hi.astype(F32)
    mid = r.astype(BF16)
    lo = (r - mid.astype(F32)).astype(BF16)
    return hi, mid, lo


def _mm3(a, b):
    ah, al = _split2(a)
    bh, bl = _split2(b)
    return _mm(ah, bh) + (_mm(ah, bl) + _mm(al, bh))


def _mm_exact_lhs(a_bf, x):
    hi, mid, lo = _split3(x)
    return _mm(a_bf, hi) + (_mm(a_bf, mid) + _mm(a_bf, lo))


def _sigmoid(x):
    return 1.0 / (1.0 + jnp.exp(-x))


def _silu(x):
    return x * _sigmoid(x)


def _softplus(x):
    return jnp.maximum(x, 0.0) + jnp.log(1.0 + jnp.exp(-jnp.abs(x)))


def _rmsnorm(xf, g):
    return xf * lax.rsqrt(jnp.mean(xf * xf, axis=-1, keepdims=True) + EPS) * g


def _iota(shape, dim):
    return lax.broadcasted_iota(jnp.int32, shape, dim)


def _row_to_col(row, n):
    eye = _iota((n, n), 0) == _iota((n, n), 1)
    return jnp.sum(jnp.where(eye, jnp.broadcast_to(row, (n, n)), 0.0), axis=1, keepdims=True)


def _ffn_kernel(x_ref, g_ref, wg_ref, wu_ref, wo_ref, *rest):
    *mix_refs, o_ref, h_ref = rest
    j = pl.program_id(1)

    @pl.when(j == 0)
    def _():
        xf = x_ref[...]
        if mix_refs:
            wm_ref = mix_refs[-1]
            off = 0
            for m_ref in mix_refs[:-1]:
                n = m_ref.shape[1]
                xf = xf + _mm(m_ref[...], wm_ref[off:off + n, :])
                off += n
        h_ref[...] = _rmsnorm(xf, g_ref[...]).astype(BF16)
        o_ref[...] = xf

    h = h_ref[...]
    gate = _mm(h, wg_ref[...])
    up = _mm(h, wu_ref[...])
    a = (_silu(gate) * up).astype(BF16)
    o_ref[...] += 0.5 * _mm(a, wo_ref[...])


def _ffn(x, g, w_in, w_out, layer, which, tm, mix=(), w_mix=None, li=0):
    m, d = x.shape
    f = w_out.shape[2]
    nf = 2
    tf = f // nf
    assert f % nf == 0 and tf % LANES == 0 and m % tm == 0
    mix_specs = [pl.BlockSpec((tm, o.shape[1]), lambda i, j: (i, 0)) for o in mix]
    if mix:
        assert sum(o.shape[1] for o in mix) == w_mix.shape[1]
        mix_specs.append(pl.BlockSpec((None, w_mix.shape[1], d), lambda i, j: (li, 0, 0)))
    return pl.pallas_call(
        _ffn_kernel,
        grid=(m // tm, nf),
        in_specs=[
            pl.BlockSpec((tm, d), lambda i, j: (i, 0)),
            pl.BlockSpec((None, 1, d), lambda i, j: (layer, 0, 0)),
            pl.BlockSpec((None, None, d, tf), lambda i, j: (layer, which, 0, j)),
            pl.BlockSpec((None, None, d, tf), lambda i, j: (layer, which, 0, j + nf)),
            pl.BlockSpec((None, None, tf, d), lambda i, j: (layer, which, j, 0)),
        ] + mix_specs,
        out_specs=pl.BlockSpec((tm, d), lambda i, j: (i, 0)),
        out_shape=jax.ShapeDtypeStruct((m, d), F32),
        scratch_shapes=[pltpu.VMEM((tm, d), BF16)],
        compiler_params=_cp("parallel", "arbitrary"),
        name="ffn",
    )(x, g, w_in, w_in, w_out, *mix, *([w_mix] if mix else []))


def _group_rms(x, grp_bf, gain, width):
    hi, lo = _split2(x * x)
    ms = (_mm(hi, grp_bf) + _mm(lo, grp_bf)) * (1.0 / width)
    return x * lax.rsqrt(ms + EPS) * gain


def _inproj_even_kernel(x_ref, g_ref, w_ref, ws_ref, grp_ref, qg_ref, kg_ref, gp_ref,
                        q_ref, k32_ref, kbf_ref, v32_ref, vt_ref, bqkv_ref, bz_ref, gb_ref):
    h = _rmsnorm(x_ref[...], g_ref[...]).astype(BF16)
    grp = grp_ref[...]

    aq = _mm(h, w_ref[:, 0:A_QK])
    q = _group_rms(aq, grp, qg_ref[...], A_DK)
    q_ref[...] = (q * (A_DK ** -0.5 * LOG2E)).astype(BF16)

    ak = _mm(h, w_ref[:, A_QK:2 * A_QK])
    k = _group_rms(ak, grp, kg_ref[...], A_DK)
    kbf_ref[...] = k.astype(BF16)

    v = _mm(h, w_ref[:, 2 * A_QK:2 * A_QK + A_V])
    tm = v.shape[0]
    ones = jnp.ones((VT_ROWS - A_DV, tm), BF16)
    for hd in range(A_HEADS):
        k32_ref[pl.ds(hd, tm, stride=A_HEADS), :] = k[:, 2 * A_DK * hd:2 * A_DK * (hd + 1)]
        v32_ref[pl.ds(hd, tm, stride=A_HEADS), :] = v[:, A_DV * hd:A_DV * (hd + 1)]
        vt_ref[hd, 0:A_DV, :] = v[:, A_DV * hd:A_DV * (hd + 1)].T.astype(BF16)
        vt_ref[hd, A_DV:VT_ROWS, :] = ones

    o = 2 * A_QK + A_V
    bqkv_ref[...] = _mm(h, w_ref[:, o:o + B_CONV_DIM])
    o += B_CONV_DIM
    bz_ref[...] = _mm(h, w_ref[:, o:o + B_V])

    y = _mm(h, ws_ref[...])
    lane = _iota(y.shape, 1)
    g = gp_ref[0:1, :] * _softplus(y + gp_ref[1:2, :])
    gb_ref[...] = jnp.where(lane < B_HEADS, g, jnp.where(lane < 2 * B_HEADS, _sigmoid(y), 0.0))


def _inproj_even(x, g, w, ws, grp, qg, kg, gp, layer, li, tm):
    m, d = x.shape
    wmain = IN_EVEN_MAIN
    row = lambda n: pl.BlockSpec((tm, n), lambda i: (i, 0))
    out_shapes = [
        jax.ShapeDtypeStruct((m, A_QK), BF16),
        jax.ShapeDtypeStruct((m * A_HEADS, 2 * A_DK), F32),
        jax.ShapeDtypeStruct((m, A_QK), BF16),
        jax.ShapeDtypeStruct((m * A_HEADS, A_DV), F32),
        jax.ShapeDtypeStruct((A_HEADS, m // tm, VT_ROWS, tm), BF16),
        jax.ShapeDtypeStruct((m, B_CONV_DIM), F32),
        jax.ShapeDtypeStruct((m, B_V), F32),
        jax.ShapeDtypeStruct((m, LANES), F32),
    ]
    out_specs = [row(s.shape[1]) for s in out_shapes]
    out_specs[1] = pl.BlockSpec((tm * A_HEADS, 2 * A_DK), lambda i: (i, 0))
    out_specs[3] = pl.BlockSpec((tm * A_HEADS, A_DV), lambda i: (i, 0))
    out_specs[4] = pl.BlockSpec((A_HEADS, None, VT_ROWS, tm), lambda i: (0, i, 0, 0))
    return pl.pallas_call(
        _inproj_even_kernel,
        grid=(m // tm,),
        in_specs=[
            row(d),
            pl.BlockSpec((None, 1, d), lambda i: (layer, 0, 0)),
            pl.BlockSpec((None, d, wmain), lambda i: (li, 0, 0)),
            pl.BlockSpec((None, d, LANES), lambda i: (li, 0, 0)),
            pl.BlockSpec((A_QK, A_QK), lambda i: (0, 0)),
            pl.BlockSpec((None, 1, A_QK), lambda i: (li, 0, 0)),
            pl.BlockSpec((None, 1, A_QK), lambda i: (li, 0, 0)),
            pl.BlockSpec((None, 2, LANES), lambda i: (li, 0, 0)),
        ],
        out_specs=out_specs,
        out_shape=out_shapes,
        compiler_params=_cp("parallel"),
        name="inproj_even",
    )(x, g, w, ws, grp, qg, kg, gp)


def _bias_kernel(rb_ref, bp_ref, bs_ref, *, tq, tk, past, page, tnew):
    h = pl.program_id(0)
    far = rb_ref[N_BUCKETS - 1, h]
    exact = N_BUCKETS // 2

    def table(rel):
        n = jnp.maximum(rel, 0)
        nf = jnp.maximum(n, 1).astype(F32)
        large = exact + (jnp.log(nf / exact) / math.log(MAX_DIST / exact) * (N_BUCKETS - exact)).astype(jnp.int32)
        bucket = jnp.where(n < exact, n, jnp.minimum(large, N_BUCKETS - 1))
        val = jnp.zeros(rel.shape, F32)
        for b in range(N_BUCKETS):
            val = jnp.where(bucket == b, rb_ref[b, h], val)
        return jnp.where(rel >= 0, (val - far) * LOG2E, NEG)

    rows = SUBLANES
    for n in range(tq // tk + 1):
        def fill(c, carry, n=n):
            r0 = pl.multiple_of(c * rows, rows)
            key_p = (n - 1) * tk + r0 + _iota((rows, tq), 0)
            bp_ref[n, pl.ds(r0, rows), :] = table(_iota((rows, tq), 1) - key_p)
            return carry

        lax.fori_loop(0, tk // rows, fill, 0)

    ncol = page * A_HEADS + LANES
    t_idx = _iota((2 * tnew, ncol), 0) & (tnew - 1)
    col = _iota((2 * tnew, ncol), 1)
    key = past - page + (col >> 2)
    rel_new = jnp.where(key - past < tnew, t_idx - (key - past), -1)
    rel = jnp.where(key < past, past + t_idx - key, rel_new)
    bs_ref[...] = table(jnp.where((col & (A_HEADS - 1)) == h, rel, -1))


def _bias_tiles(rel_bias, tq, tk, past, page, tnew):
    assert tnew & (tnew - 1) == 0 and 2 * tnew == SUBLANES and A_HEADS == 4 and tq % tk == 0
    ncol = page * A_HEADS + LANES
    ntile = tq // tk + 1
    return pl.pallas_call(
        functools.partial(_bias_kernel, tq=tq, tk=tk, past=past, page=page, tnew=tnew),
        grid=(A_HEADS,),
        in_specs=[pl.BlockSpec(memory_space=pltpu.SMEM)],
        out_specs=[
            pl.BlockSpec((None, ntile, tk, tq), lambda h: (h, 0, 0, 0)),
            pl.BlockSpec((None, 2 * tnew, ncol), lambda h: (h, 0, 0)),
        ],
        out_shape=[
            jax.ShapeDtypeStruct((A_HEADS, ntile, tk, tq), F32),
            jax.ShapeDtypeStruct((A_HEADS, 2 * tnew, ncol), F32),
        ],
        compiler_params=_cp("parallel"),
        name="t5_bias",
    )(rel_bias)


def _far_bucket_saturated(dist):
    exact = N_BUCKETS // 2
    v = math.log(dist / exact) / math.log(MAX_DIST / exact) * (N_BUCKETS - exact)
    return dist >= exact and exact + v >= N_BUCKETS - 1 + 1e-3


def _attn_prompt_kernel(sc_ref, q_ref, k_ref, vt_ref, bias_ref, sg_ref, o_ref, *, tq, tk):
    head = pl.program_id(1)
    i = pl.program_id(2)
    q = q_ref[...]
    lane = _iota(q.shape, 1)
    zero = jnp.zeros_like(q)
    qbd = jnp.concatenate([jnp.where(lane < A_DK, q, zero), jnp.where(lane >= A_DK, q, zero)], axis=0)
    ratio = tq // tk

    def biased(j, bias):
        start = pl.multiple_of(j * tk, tk)
        s = _mm(k_ref[pl.ds(start, tk), :], qbd, NT)
        return s if bias is None else s + jnp.concatenate([bias, bias], axis=1)

    def key_blocks(step, carry):
        first = i * ratio
        nfar = jnp.maximum(first - 1, 0)
        per_trip = 4

        def trip(j, c):
            for u in range(per_trip):
                c = step(per_trip * j + u, c, None)
            return c

        def overlapping(c):
            for n in range(ratio):
                c = step(first + n, c, bias_ref[n + 1])
            return c

        carry = lax.fori_loop(0, nfar // per_trip, trip, carry)
        carry = lax.fori_loop(nfar - nfar % per_trip, nfar, lambda j, c: step(j, c, None), carry)
        return lax.cond(i >= 1, lambda c: overlapping(step(first - 1, c, bias_ref[0])), overlapping, carry)

    def fixed_shift(_):
        shift = sc_ref[3 + head]

        def step(j, acc, bias):
            return acc + _mm(vt_ref[j], jnp.exp2(biased(j, bias) - shift).astype(BF16))

        return key_blocks(step, jnp.zeros((VT_ROWS, 2 * tq), F32))

    def running_max(_):
        def step(j, c, bias):
            m, acc = c
            s = biased(j, bias)
            m_new = jnp.maximum(m, jnp.max(s, axis=0, keepdims=True))
            p = jnp.exp2(s - m_new).astype(BF16)
            return m_new, jnp.exp2(m - m_new) * acc + _mm(vt_ref[j], p)

        return key_blocks(step, (jnp.full((1, 2 * tq), NEG, F32), jnp.zeros((VT_ROWS, 2 * tq), F32)))[1]

    acc = lax.cond(sc_ref[2] > 0.5, fixed_shift, running_max, 0)

    o = acc[0:A_DV] / acc[A_DV:A_DV + 1]
    o = o[:, 0:tq] - sc_ref[0] * o[:, tq:2 * tq]
    o = o * lax.rsqrt(jnp.mean(o * o, axis=0, keepdims=True) + EPS)
    o_ref[...] = (o.T * (sg_ref[...] * sc_ref[1])).astype(BF16)


def _attn_prompt(sc, q, k, vt, bias_p, sg, li, tq):
    bsz, t, _ = q.shape
    tk = vt.shape[3]
    assert tq % tk == 0 and t % tq == 0 and bias_p.shape[1:] == (tq // tk + 1, tk, tq)
    return pl.pallas_call(
        functools.partial(_attn_prompt_kernel, tq=tq, tk=tk),
        grid=(bsz, A_HEADS, t // tq),
        in_specs=[
            pl.BlockSpec(memory_space=pltpu.SMEM),
            pl.BlockSpec((None, tq, 2 * A_DK), lambda b, h, i: (b, i, h)),
            pl.BlockSpec((None, t, 2 * A_DK), lambda b, h, i: (b, 0, h)),
            pl.BlockSpec((None, t // tk, VT_ROWS, tk), lambda b, h, i: (h, b, 0, 0)),
            pl.BlockSpec((None, tq // tk + 1, tk, tq), lambda b, h, i: (h, 0, 0, 0)),
            pl.BlockSpec((None, 1, A_DV), lambda b, h, i: (li, 0, 0)),
        ],
        out_specs=pl.BlockSpec((None, tq, A_DV), lambda b, h, i: (b, i, h)),
        out_shape=jax.ShapeDtypeStruct((bsz, t, A_V), BF16),
        compiler_params=_cp("parallel", "parallel", "arbitrary"),
        name="attn_prompt",
    )(sc, q, k, vt, bias_p, sg)


def _attn_sample_kernel(pt_ref, sc_ref, qbd_ref, bias_ref, biasn_ref, kn_ref, vn_ref, sg_ref, *rest, npg, tnew):
    k_refs = rest[:npg]
    v_refs = rest[npg:2 * npg]
    o_ref, m_ref, l_ref, acc_ref, mask_ref = rest[2 * npg:]
    pg = pl.program_id(1)
    last = pg == pl.num_programs(1) - 1
    qbd = qbd_ref[...]

    @pl.when(pg == 0)
    def _():
        m_ref[...] = jnp.full(m_ref.shape, NEG, F32)
        l_ref[...] = jnp.zeros(l_ref.shape, F32)
        acc_ref[...] = jnp.zeros(acc_ref.shape, F32)
        row_head = _iota(mask_ref.shape, 0) >> ((2 * tnew).bit_length() - 1)
        col_head = _iota(mask_ref.shape, 1) & (A_HEADS - 1)
        mask_ref[...] = jnp.where(row_head == col_head, 0.0, NEG)

    def update(s, vs):
        m_old = m_ref[...]
        m_new = jnp.maximum(m_old, jnp.max(s, axis=1, keepdims=True))
        alpha = jnp.exp2(m_old - m_new)
        p = jnp.exp2(s - m_new)
        l_ref[...] = alpha * l_ref[...] + jnp.sum(p, axis=1, keepdims=True)
        pb = p.astype(BF16)
        w = pb.shape[1] // len(vs)
        pv = _mm(pb[:, 0:w], vs[0])
        for g in range(1, len(vs)):
            pv = pv + _mm(pb[:, g * w:(g + 1) * w], vs[g])
        acc_ref[...] = alpha * acc_ref[...] + pv
        m_ref[...] = m_new

    mask = mask_ref[...]
    final_page = jnp.where(last, bias_ref[...], mask)
    s = jnp.concatenate([_mm(qbd, kr[...].astype(BF16), NT) + (mask if g < npg - 1 else final_page)
                         for g, kr in enumerate(k_refs)], axis=1)
    update(s, [vr[...].astype(BF16) for vr in v_refs])

    @pl.when(last)
    def _():
        update(_mm(qbd, kn_ref[...], NT) + biasn_ref[...], [vn_ref[...]])
        lam = sc_ref[0]
        rows = 2 * tnew
        for h in range(A_HEADS):
            a = acc_ref[rows * h:rows * (h + 1), :] / l_ref[rows * h:rows * (h + 1), :]
            o = a - lam * pltpu.roll(a, tnew, axis=0)
            o_ref[:, A_DV * h:A_DV * (h + 1)] = _rmsnorm(o, sg_ref[...]) * sc_ref[1]


def _attn_sample(page_table, sc, qbd, bias_s, knew, vnew, sg, cache_k, cache_v, li, tnew):
    bsz, nrow, _ = qbd.shape
    n_pages = page_table.shape[1]
    npg = PAGES_PER_STEP
    assert n_pages % npg == 0
    prow = cache_k.shape[2]
    assert bias_s.shape == (nrow, prow + LANES)

    def kvspec(g):
        return pl.BlockSpec((None, None, prow, 2 * A_DK), lambda b, pg, pt: (li, pt[b, pg * npg + g], 0, 0))

    grid_spec = pltpu.PrefetchScalarGridSpec(
        num_scalar_prefetch=1,
        grid=(bsz, n_pages // npg),
        in_specs=[
            pl.BlockSpec(memory_space=pltpu.SMEM),
            pl.BlockSpec((None, nrow, 2 * A_DK), lambda b, pg, pt: (b, 0, 0)),
            pl.BlockSpec((nrow, prow), lambda b, pg, pt: (0, 0)),
            pl.BlockSpec((nrow, LANES), lambda b, pg, pt: (0, prow // LANES)),
            pl.BlockSpec((None, LANES, 2 * A_DK), lambda b, pg, pt: (b, 0, 0)),
            pl.BlockSpec((None, LANES, A_DV), lambda b, pg, pt: (b, 0, 0)),
            pl.BlockSpec((None, 1, A_DV), lambda b, pg, pt: (li, 0, 0)),
        ] + [kvspec(g) for g in range(npg)] + [kvspec(g) for g in range(npg)],
        out_specs=pl.BlockSpec((None, 2 * tnew, A_V), lambda b, pg, pt: (b, 0, 0)),
        scratch_shapes=[pltpu.VMEM((nrow, 1), F32), pltpu.VMEM((nrow, 1), F32), pltpu.VMEM((nrow, A_DV), F32),
                        pltpu.VMEM((nrow, prow), F32)],
    )
    return pl.pallas_call(
        functools.partial(_attn_sample_kernel, npg=npg, tnew=tnew),
        grid_spec=grid_spec,
        out_shape=jax.ShapeDtypeStruct((bsz, 2 * tnew, A_V), F32),
        compiler_params=_cp("parallel", "arbitrary"),
        name="attn_sample",
    )(page_table, sc, qbd, bias_s, bias_s, knew, vnew, sg, *([cache_k] * npg), *([cache_v] * npg))


def _l2norm(x):
    return x * lax.rsqrt(jnp.sum(x * x, axis=-1, keepdims=True) + EPS)


def _unit_lower_inverses(mats, c):
    n = mats[0].shape[0]
    nblk = n // c
    shift = c.bit_length() - 1
    same = (_iota((n, n), 0) >> shift) == (_iota((n, n), 1) >> shift)
    eye_w = ((_iota((c, n), 1) & (c - 1)) == _iota((c, n), 0)).astype(F32)

    def wide(sq):
        out = sq[0:c]
        for r in range(1, nblk):
            out = out + sq[r * c:(r + 1) * c]
        return out

    def square(w):
        return jnp.where(same, jnp.concatenate([w] * nblk, axis=0), jnp.zeros((), w.dtype))

    def mm3_split(lhs_split, rhs_split):
        (lh, ll), (rh, rl) = lhs_split, rhs_split
        return _mm(lh, rh) + (_mm(lh, rl) + _mm(ll, rh))

    qw = [_split2(wide(a)) for a in mats]
    qsq = [(square(h), square(l)) for h, l in qw]
    ps = [eye_w - wide(a) for a in mats]
    span = 2
    while span < c:
        qw = [_split2(mm3_split(q, s)) for q, s in zip(qw, qsq)]
        qsq = [(square(h), square(l)) for h, l in qw]
        ps = [p + mm3_split(_split2(p), s) for p, s in zip(ps, qsq)]
        span *= 2
    return [square(p) for p in ps]


def _cumsum_rows(x, n):
    row = _iota(x.shape, 0)
    k = 1
    while k < n:
        x = x + jnp.where(row >= k, pltpu.roll(x, k, axis=0), 0.0)
        k *= 2
    return x


def _delta_prep_kernel(x_ref, halo_ref, c0_ref, gb_ref, cw_ref, u_ref, wk_ref, qe_ref, kd_ref, qk_ref, xbuf,
                       *, c, nck):
    ci = pl.program_id(1)
    ch = B_HEADS * c
    rows = nck * c
    xbuf[0:SUBLANES, :] = jnp.where(ci == 0, c0_ref[...], halo_ref[...])
    xbuf[SUBLANES:SUBLANES + rows, :] = x_ref[...]
    first = SUBLANES - (CONV_W - 1)
    conv = xbuf[pl.ds(first, rows), :] * cw_ref[0:1, :]
    for j in range(1, CONV_W):
        conv = conv + xbuf[pl.ds(first + j, rows), :] * cw_ref[j:j + 1, :]
    conv = _silu(conv)

    row = _iota((ch, ch), 0)
    col = _iota((ch, ch), 1)
    shift = c.bit_length() - 1
    same = (row >> shift) == (col >> shift)
    incl = same & (row >= col)
    strict = same & (row > col)

    def stack_lane(a, off):
        return jnp.concatenate([a[:, off + h:off + h + 1] for h in range(B_HEADS)], axis=0)

    parts = []
    for n in range(nck):
        rs = slice(n * c, (n + 1) * c)

        def stack(off, width):
            return jnp.concatenate([conv[rs, off + width * h:off + width * (h + 1)] for h in range(B_HEADS)], axis=0)

        q = _l2norm(stack(0, B_DK)) * (B_DK ** -0.5)
        k = _l2norm(stack(B_QK, B_DK))
        v = stack(2 * B_QK, B_DV)
        gb = gb_ref[rs, :]
        gcum = _cumsum_rows(gb, c)
        gcol = stack_lane(gcum, 0)
        beta = stack_lane(gb, B_HEADS)
        glast = stack_lane(jnp.broadcast_to(gcum[c - 1:c, :], gcum.shape), 0)
        grow = jnp.broadcast_to(gcol, (ch, LANES)).T[0:1, :]
        dec = jnp.where(incl, jnp.exp(jnp.where(incl, gcol - grow, 0.0)), 0.0)
        kb = k.astype(BF16)
        a = jnp.where(strict, _mm(kb, kb, NT) * dec * beta, 0.0)
        egc = jnp.exp(gcol)
        qe_ref[n] = (q * egc).astype(BF16)
        kd_ref[n] = (k * jnp.exp(glast - gcol)).astype(BF16)
        qk_ref[n] = (_mm(q.astype(BF16), kb, NT) * dec).astype(BF16)
        parts.append((a, jnp.concatenate([v * beta, k * (beta * egc)], axis=1)))

    t_invs = _unit_lower_inverses([a for a, _ in parts], c)
    for n in range(nck):
        uw = _mm3(t_invs[n], parts[n][1])
        u_ref[n] = uw[:, 0:B_DV]
        wk_ref[n] = uw[:, B_DV:B_DV + B_DK].astype(BF16)


def _delta_scan_kernel(u_ref, wk_ref, qe_ref, kd_ref, qk_ref, gb_ref, z_ref, s0_ref, bn_ref, ob_ref, sout_ref,
                       s_scr, *, c, nb, cps):
    ci = pl.program_id(1)

    @pl.when(ci == 0)
    def _():
        s_scr[...] = s0_ref[...]

    hs = [slice(c * h, c * (h + 1)) for h in range(B_HEADS)]
    for n in range(cps):
        rows = slice(c * n, c * (n + 1))
        for b in range(nb):
            decay = jnp.exp(jnp.sum(gb_ref[b, rows, :], axis=0, keepdims=True))
            s = [s_scr[b, h] for h in range(B_HEADS)]
            s_bf = [x.astype(BF16) for x in s]
            v_new = u_ref[b, n] - jnp.concatenate(
                [_mm(wk_ref[b, n, hs[h], :], s_bf[h]) for h in range(B_HEADS)], axis=0)
            vn_bf = v_new.astype(BF16)
            o = (jnp.concatenate([_mm(qe_ref[b, n, hs[h], :], s_bf[h]) for h in range(B_HEADS)], axis=0)
                 + _mm(qk_ref[b, n], vn_bf))
            for h in range(B_HEADS):
                s_scr[b, h] = s[h] * decay[:, h:h + 1] + _mm(kd_ref[b, n, hs[h], :], vn_bf[hs[h]], TN)
                sl = slice(B_DV * h, B_DV * (h + 1))
                ob_ref[b, rows, sl] = (_rmsnorm(o[hs[h]], bn_ref[...]) * _silu(z_ref[b, rows, sl])).astype(BF16)

    @pl.when(ci == pl.num_programs(1) - 1)
    def _():
        sout_ref[...] = s_scr[...]


def _delta(bqkv, bz, gb, conv0, s0, cw, bn, li, c):
    bsz, t, _ = bqkv.shape
    assert t % c == 0 and c % SUBLANES == 0 and c & (c - 1) == 0
    nch = t // c
    ch = B_HEADS * c
    nck = min(DELTA_PREP_CHUNKS, nch)
    assert nch % nck == 0
    hb = nck * c // SUBLANES
    tok = lambda n: pl.BlockSpec((None, c, n), lambda b, ci: (b, ci, 0))
    step_tok = lambda n: pl.BlockSpec((None, nck * c, n), lambda b, ci: (b, ci, 0))
    step_chunks = lambda n: pl.BlockSpec((None, nck, ch, n), lambda b, ci: (b, ci, 0, 0))
    chunk_shape = lambda n, dt: jax.ShapeDtypeStruct((bsz, nch, ch, n), dt)
    u, wk, qe, kd, qk = pl.pallas_call(
        functools.partial(_delta_prep_kernel, c=c, nck=nck),
        grid=(bsz, nch // nck),
        in_specs=[
            step_tok(B_CONV_DIM),
            pl.BlockSpec((None, SUBLANES, B_CONV_DIM), lambda b, ci: (b, jnp.maximum(ci * hb - 1, 0), 0)),
            pl.BlockSpec((None, SUBLANES, B_CONV_DIM), lambda b, ci: (b, 0, 0)),
            step_tok(LANES),
            pl.BlockSpec((None, CONV_W, B_CONV_DIM), lambda b, ci: (li, 0, 0)),
        ],
        out_specs=[step_chunks(B_DV), step_chunks(B_DK), step_chunks(B_DK), step_chunks(B_DK), step_chunks(ch)],
        out_shape=[chunk_shape(B_DV, F32), chunk_shape(B_DK, BF16), chunk_shape(B_DK, BF16),
                   chunk_shape(B_DK, BF16), chunk_shape(ch, BF16)],
        scratch_shapes=[pltpu.VMEM((SUBLANES + nck * c, B_CONV_DIM), F32)],
        compiler_params=_cp("parallel", "parallel"),
        name="delta_prep",
    )(bqkv, bqkv, conv0, gb, cw)

    nb = 2
    cps = DELTA_SCAN_CHUNKS if nch % DELTA_SCAN_CHUNKS == 0 else 1
    assert bsz % nb == 0
    grp_chunk = lambda n: pl.BlockSpec((nb, cps, ch, n), lambda b, ci: (b, ci, 0, 0))
    grp_tok = lambda n: pl.BlockSpec((nb, cps * c, n), lambda b, ci: (b, ci, 0))
    state = pl.BlockSpec((nb, B_HEADS, B_DK, B_DV), lambda b, ci: (b, 0, 0, 0))
    return pl.pallas_call(
        functools.partial(_delta_scan_kernel, c=c, nb=nb, cps=cps),
        grid=(bsz // nb, nch // cps),
        in_specs=[grp_chunk(B_DV), grp_chunk(B_DK), grp_chunk(B_DK), grp_chunk(B_DK), grp_chunk(ch),
                  grp_tok(LANES), grp_tok(B_V), state,
                  pl.BlockSpec((None, 1, B_DV), lambda b, ci: (li, 0, 0))],
        out_specs=[grp_tok(B_V), state],
        out_shape=[jax.ShapeDtypeStruct((bsz, t, B_V), BF16),
                   jax.ShapeDtypeStruct((bsz, B_HEADS, B_DK, B_DV), F32)],
        scratch_shapes=[pltpu.VMEM((nb, B_HEADS, B_DK, B_DV), F32)],
        compiler_params=_cp("parallel", "arbitrary"),
        name="delta_scan",
    )(u, wk, qe, kd, qk, gb, bz, s0, bn)


def _inproj_odd_kernel(x_ref, g_ref, w_ref, ws_ref, w2_ref, gbias_ref, q_ref, k_ref, v_ref, r_ref, gk_ref):
    h = _rmsnorm(x_ref[...], g_ref[...]).astype(BF16)
    q_ref[...] = _mm(h, w_ref[:, 0:C_QK])
    k_ref[...] = _mm(h, w_ref[:, C_QK:2 * C_QK])
    v_ref[...] = _mm(h, w_ref[:, 2 * C_QK:2 * C_QK + C_V]).astype(BF16)
    r_ref[...] = _mm(h, w_ref[:, 2 * C_QK + C_V:2 * C_QK + 2 * C_V])
    glr = _mm(h, ws_ref[...]).astype(BF16)
    y = _mm(glr, w2_ref[...]) + gbias_ref[...]
    gk_ref[...] = -_softplus(-y) * (1.0 / GLA_TAU)


def _inproj_odd(x, g, w, ws, w2, gbias, layer, li, tm):
    m, d = x.shape
    wmain = 2 * C_QK + 2 * C_V
    row = lambda n: pl.BlockSpec((tm, n), lambda i: (i, 0))
    out_shapes = [
        jax.ShapeDtypeStruct((m, C_QK), F32),
        jax.ShapeDtypeStruct((m, C_QK), F32),
        jax.ShapeDtypeStruct((m, C_V), BF16),
        jax.ShapeDtypeStruct((m, C_V), F32),
        jax.ShapeDtypeStruct((m, C_QK), F32),
    ]
    return pl.pallas_call(
        _inproj_odd_kernel,
        grid=(m // tm,),
        in_specs=[
            row(d),
            pl.BlockSpec((None, 1, d), lambda i: (layer, 0, 0)),
            pl.BlockSpec((None, d, wmain), lambda i: (li, 0, 0)),
            pl.BlockSpec((None, d, LANES), lambda i: (li, 0, 0)),
            pl.BlockSpec((None, LANES, C_QK), lambda i: (li, 0, 0)),
            pl.BlockSpec((None, 1, C_QK), lambda i: (li, 0, 0)),
        ],
        out_specs=[row(s.shape[1]) for s in out_shapes],
        out_shape=out_shapes,
        compiler_params=_cp("parallel"),
        name="inproj_odd",
    )(x, g, w, ws, w2, gbias)


def _gla_kernel(q_ref, k_ref, v_ref, r_ref, gk_ref, s0_ref, cn_ref, o_ref, sout_ref, s_scr, *, c, sub, nb):
    ci = pl.program_id(1)

    @pl.when(ci == 0)
    def _():
        s_scr[...] = s0_ref[...]

    for b in range(nb):
        _gla_chunk(q_ref.at[b], k_ref.at[b], v_ref.at[b], r_ref.at[b], gk_ref.at[b], cn_ref, o_ref.at[b],
                   s_scr.at[b], c, sub)

    @pl.when(ci == pl.num_programs(1) - 1)
    def _():
        sout_ref[...] = s_scr[...]


def _gla_chunk(q_ref, k_ref, v_ref, r_ref, gk_ref, cn_ref, o_ref, s_scr, c, sub):
    row = _iota((c, c), 0)
    col = _iota((c, c), 1)
    lower_bf = (row >= col).astype(BF16)
    in_sub = row & (sub - 1)
    sub_start = row - in_sub
    b_all = _mm_exact_lhs(lower_bf, gk_ref[...])

    for h in range(C_HEADS):
        sl = slice(C_DK * h, C_DK * (h + 1))
        vl = slice(C_DV * h, C_DV * (h + 1))
        q = q_ref[:, sl] * (C_DK ** -0.5)
        k = k_ref[:, sl]
        b = b_all[:, sl]
        v = v_ref[:, vl]
        s = s_scr[h]

        o = _mm((q * jnp.exp(b)).astype(BF16), s.astype(BF16))

        blocks = [jnp.zeros((sub, c), F32)]
        for r in range(1, c // sub):
            ref = b[sub * r:sub * r + 1, :]
            qt = q[sub * r:sub * (r + 1)] * jnp.exp(b[sub * r:sub * (r + 1)] - ref)
            kt = k * jnp.exp(jnp.minimum(ref - b, 0.0))
            blocks.append(_mm(qt.astype(BF16), kt.astype(BF16), NT))
        a = jnp.where(col < sub_start, jnp.concatenate(blocks, axis=0), 0.0)
        b2 = b * LOG2E
        for d in range(sub):
            kd = k if d == 0 else pltpu.roll(k, d, axis=0)
            bd = b2 if d == 0 else pltpu.roll(b2, d, axis=0)
            ad = jnp.sum(q * kd * jnp.exp2(b2 - bd), axis=1, keepdims=True)
            a = jnp.where((col == row - d) & (in_sub >= d), ad, a)
        o = o + _mm(a.astype(BF16), v)

        blast = b[c - 1:c, :]
        kdec = (k * jnp.exp(blast - b)).astype(BF16)
        s_scr[h] = s * _row_to_col(jnp.exp(blast), C_DK) + _mm(kdec, v, TN)

        o_ref[:, vl] = (_rmsnorm(o, cn_ref[...]) * _silu(r_ref[:, vl])).astype(BF16)


def _gla(q, k, v, r, gk, s0, cn, li, c):
    bsz, t, _ = q.shape
    nb = 2
    assert t % c == 0 and bsz % nb == 0
    sub = min(GLA_SUB, c)
    tok = lambda n: pl.BlockSpec((nb, c, n), lambda b, ci: (b, ci, 0))
    state = pl.BlockSpec((nb, C_HEADS, C_DK, C_DV), lambda b, ci: (b, 0, 0, 0))
    return pl.pallas_call(
        functools.partial(_gla_kernel, c=c, sub=sub, nb=nb),
        grid=(bsz // nb, t // c),
        in_specs=[tok(C_QK), tok(C_QK), tok(C_V), tok(C_V), tok(C_QK), state,
                  pl.BlockSpec((None, 1, C_DV), lambda b, ci: (li, 0, 0))],
        out_specs=[tok(C_V), state],
        out_shape=[jax.ShapeDtypeStruct((bsz, t, C_V), BF16),
                   jax.ShapeDtypeStruct((bsz, C_HEADS, C_DK, C_DV), F32)],
        scratch_shapes=[pltpu.VMEM((nb, C_HEADS, C_DK, C_DV), F32)],
        compiler_params=_cp("parallel", "arbitrary"),
        name="gla",
    )(q, k, v, r, gk, s0, cn)


def _pad_rows(a, rows):
    return jnp.pad(a, ((0, 0), (0, rows - a.shape[1]), (0, 0)))


def _pad_lanes(a):
    return jnp.pad(a, [(0, 0)] * (a.ndim - 1) + [(0, LANES - a.shape[-1])])


def kernel(x_prompt, x_sample, cache_k, cache_v, state_delta, state_conv, state_gla, page_table, rel_bias, norm_ffn1, norm_mix, norm_ffn2, ffn_in, ffn_out, w_in_even, a_qk_norm, a_lambda, a_subln, b_conv, b_a_log, b_dt_bias, b_norm, w_out_even, w_in_odd, c_gate_w2, c_gate_b, c_norm, w_out_odd):
    bp, tp, d = x_prompt.shape
    bs, ts, _ = x_sample.shape
    depth = norm_ffn1.shape[0]
    n_att = w_in_even.shape[0]
    n_pool, page = cache_k.shape[1], cache_k.shape[2]
    past = page_table.shape[1] * page
    tq = ATT_Q_TILE
    cs = SUBLANES
    assert page == LANES and tp % tq == 0 and ts <= cs and ts >= CONV_W - 1
    assert _far_bucket_saturated(ATT_K_TILE + 1) and _far_bucket_saturated(page + 1)

    ffn_in_bf = ffn_in.astype(BF16)
    ffn_out_bf = ffn_out.astype(BF16)
    w_even_bf = w_in_even.astype(BF16)
    w_even_small = _pad_lanes(w_in_even[:, :, IN_EVEN_MAIN:]).astype(BF16)
    w_out_even_bf = w_out_even.astype(BF16)
    w_odd_bf = w_in_odd.astype(BF16)
    w_odd_small = _pad_lanes(w_in_odd[:, :, IN_ODD_MAIN:]).astype(BF16)
    w_out_odd_bf = w_out_odd.astype(BF16)
    gate_w2 = jnp.pad(c_gate_w2, ((0, 0), (0, LANES - GLA_LOWRANK), (0, 0))).astype(BF16)
    gate_b = c_gate_b[:, None, :]
    n1 = norm_ffn1[:, None, :]
    nm = norm_mix[:, None, :]
    n2 = norm_ffn2[:, None, :]
    grp = jnp.kron(jnp.eye(2 * A_HEADS, dtype=F32), jnp.ones((A_DK, A_DK), F32)).astype(BF16)
    qgain = jnp.tile(a_qk_norm[:, 0:1, :], (1, 1, 2 * A_HEADS))
    kgain = jnp.tile(a_qk_norm[:, 1:2, :], (1, 1, 2 * A_HEADS))
    gparams = jnp.stack([_pad_lanes(-jnp.exp(b_a_log)), _pad_lanes(b_dt_bias)], axis=1)
    subln = a_subln[:, None, :]
    bnorm = b_norm[:, None, :]
    cnorm = c_norm[:, None, :]
    cache_k2 = cache_k.reshape(n_att, n_pool, page * A_HEADS, 2 * A_DK)
    cache_v2 = cache_v.reshape(n_att, n_pool, page * A_HEADS, A_DV)

    bias_l2 = (rel_bias - rel_bias[N_BUCKETS - 1:N_BUCKETS]) * LOG2E
    bias_p, bias_s = _bias_tiles(rel_bias, tq, ATT_K_TILE, past, page, ts)
    bias_s = bias_s.reshape(A_HEADS * 2 * ts, page * A_HEADS + LANES)

    xp = x_prompt.reshape(bp * tp, d)
    xs = x_sample.reshape(bs * ts, d)
    tm_p = ROWS_DENSE
    tm_s = bs * ts
    k_rows = [[], []]
    v_rows = [[], []]
    deltas = [[], []]
    convs = [[], []]
    glas = [[], []]

    for i in range(depth):
        li = i // 2
        xp = _ffn(xp, n1, ffn_in_bf, ffn_out_bf, i, 0, tm_p)
        xs = _ffn(xs, n1, ffn_in_bf, ffn_out_bf, i, 0, tm_s)
        if i % 2 == 0:
            lam_init = 0.8 - 0.6 * math.exp(-0.3 * i)
            lq1, lk1, lq2, lk2 = a_lambda[li]
            lam = jnp.exp(jnp.sum(lq1 * lk1)) - jnp.exp(jnp.sum(lq2 * lk2)) + lam_init
            smax = (1.02 * A_DK * A_DK ** -0.5 * LOG2E) * jnp.max(jnp.abs(a_qk_norm[li, 0])) * jnp.max(
                jnp.abs(a_qk_norm[li, 1]))
            shift = smax + jnp.max(bias_l2, axis=0)
            fast = jnp.all(2.0 * smax + jnp.max(bias_l2, axis=0) - jnp.min(bias_l2, axis=0) <= SAFE_SPAN)
            sc = jnp.concatenate([jnp.stack([lam, jnp.asarray(1.0 - lam_init, F32), fast.astype(F32)]),
                                  shift]).astype(F32)

            q, k32, kbf, v32, vt, bqkv, bz, gb = _inproj_even(
                xp, nm, w_even_bf, w_even_small, grp, qgain, kgain, gparams, i, li, ROWS_INPROJ)
            oa = _attn_prompt(sc, q.reshape(bp, tp, A_QK), kbf.reshape(bp, tp, A_QK), vt, bias_p, subln, li, tq)
            bqkv3 = bqkv.reshape(bp, tp, B_CONV_DIM)
            ob, s_new = _delta(bqkv3, bz.reshape(bp, tp, B_V), gb.reshape(bp, tp, LANES),
                               jnp.zeros((bp, SUBLANES, B_CONV_DIM), F32),
                               jnp.zeros((bp, B_HEADS, B_DK, B_DV), F32), b_conv, bnorm, li, SEQ_CHUNK)
            mix_p, w_mix = (oa.reshape(bp * tp, A_V), ob.reshape(bp * tp, B_V)), w_out_even_bf
            k_rows[0].append(k32.reshape(bp, tp, A_HEADS, 2 * A_DK))
            v_rows[0].append(v32.reshape(bp, tp, A_HEADS, A_DV))
            deltas[0].append(s_new)
            convs[0].append(bqkv3[:, tp - (CONV_W - 1):, :])

            q, k32, kbf, v32, _, bqkv, bz, gb = _inproj_even(
                xs, nm, w_even_bf, w_even_small, grp, qgain, kgain, gparams, i, li, tm_s)
            q4 = q.reshape(bs, ts, A_HEADS, 2, A_DK)
            qbd = jnp.einsum('bthcd,ce->bhcted', q4, jnp.eye(2, dtype=BF16)).reshape(bs, A_HEADS * 2 * ts, 2 * A_DK)
            knew = _pad_rows(kbf.reshape(bs, ts * A_HEADS, 2 * A_DK), LANES)
            vnew = _pad_rows(v32.reshape(bs, ts * A_HEADS, A_DV).astype(BF16), LANES)
            oa = _attn_sample(page_table, sc, qbd, bias_s, knew, vnew, subln, cache_k2, cache_v2, li, ts)
            oa = oa[:, :ts, :].astype(BF16)
            bqkv3 = bqkv.reshape(bs, ts, B_CONV_DIM)
            conv0 = jnp.pad(state_conv[li], ((0, 0), (SUBLANES - (CONV_W - 1), 0), (0, 0)))
            ob, s_new = _delta(_pad_rows(bqkv3, cs), _pad_rows(bz.reshape(bs, ts, B_V), cs),
                               _pad_rows(gb.reshape(bs, ts, LANES), cs), conv0, state_delta[li],
                               b_conv, bnorm, li, cs)
            mix_s = (oa.reshape(bs * ts, A_V), ob[:, :ts, :].reshape(bs * ts, B_V))
            k_rows[1].append(k32.reshape(bs, ts, A_HEADS, 2 * A_DK))
            v_rows[1].append(v32.reshape(bs, ts, A_HEADS, A_DV))
            deltas[1].append(s_new)
            convs[1].append(jnp.concatenate([state_conv[li], bqkv3], axis=1)[:, ts:, :])
        else:
            q, k, v, r, gk = _inproj_odd(xp, nm, w_odd_bf, w_odd_small, gate_w2, gate_b, i, li, ROWS_INPROJ)
            o, s_new = _gla(q.reshape(bp, tp, C_QK), k.reshape(bp, tp, C_QK), v.reshape(bp, tp, C_V),
                            r.reshape(bp, tp, C_V), gk.reshape(bp, tp, C_QK),
                            jnp.zeros((bp, C_HEADS, C_DK, C_DV), F32), cnorm, li, SEQ_CHUNK)
            mix_p, w_mix = (o.reshape(bp * tp, C_V),), w_out_odd_bf
            glas[0].append(s_new)

            q, k, v, r, gk = _inproj_odd(xs, nm, w_odd_bf, w_odd_small, gate_w2, gate_b, i, li, tm_s)
            pad = lambda a, n: _pad_rows(a.reshape(bs, ts, n), cs)
            o, s_new = _gla(pad(q, C_QK), pad(k, C_QK), pad(v, C_V), pad(r, C_V), pad(gk, C_QK),
                            state_gla[li], cnorm, li, cs)
            mix_s = (o[:, :ts, :].reshape(bs * ts, C_V),)
            glas[1].append(s_new)
        xp = _ffn(xp, n2, ffn_in_bf, ffn_out_bf, i, 1, tm_p, mix_p, w_mix, li)
        xs = _ffn(xs, n2, ffn_in_bf, ffn_out_bf, i, 1, tm_s, mix_s, w_mix, li)

    return (xp.reshape(bp, tp, d), xs.reshape(bs, ts, d),
            jnp.stack(k_rows[0]), jnp.stack(v_rows[0]), jnp.stack(deltas[0]), jnp.stack(convs[0]),
            jnp.stack(glas[0]),
            jnp.stack(k_rows[1]), jnp.stack(v_rows[1]), jnp.stack(deltas[1]), jnp.stack(convs[1]),
            jnp.stack(glas[1]))
```

```python
import functools
import math

import jax
import jax.numpy as jnp
from jax import lax
from jax.experimental import pallas as pl
from jax.experimental.pallas import tpu as pltpu

F32 = jnp.float32
BF16 = jnp.bfloat16

A_HEADS, A_DK, A_DV = 4, 64, 128
B_HEADS, B_DK, B_DV = 4, 128, 128
CONV_W = 4
C_HEADS, C_DK, C_DV = 4, 128, 256
GLA_LOWRANK, GLA_TAU = 16, 16.0
N_BUCKETS, MAX_DIST = 32, 128
EPS = 1e-6
NEG = -1e30
LOG2E = 1.4426950408889634

A_QK = A_HEADS * 2 * A_DK
A_V = A_HEADS * A_DV
B_QK = B_HEADS * B_DK
B_V = B_HEADS * B_DV
B_CONV_DIM = 2 * B_QK + B_V
C_QK = C_HEADS * C_DK
C_V = C_HEADS * C_DV
IN_EVEN_MAIN = 2 * A_QK + A_V + B_CONV_DIM + B_V
IN_ODD_MAIN = 2 * C_QK + 2 * C_V

LANES = 128
SUBLANES = 8
PACKED_SUBLANES = 16
VMEM_LIMIT = 56 * 2 ** 20

ROWS_DENSE = 512
ROWS_INPROJ = 256
ATT_Q_TILE = 1024
ATT_K_TILE = ROWS_INPROJ
VT_ROWS = A_DV + PACKED_SUBLANES
SAFE_SPAN = 96.0
SEQ_CHUNK = 64
DELTA_PREP_CHUNKS = 4
DELTA_SCAN_CHUNKS = 4
GLA_SUB = 16
PAGES_PER_STEP = 16

NT = (((1,), (1,)), ((), ()))
TN = (((0,), (0,)), ((), ()))


def _cp(*sem):
    return pltpu.CompilerParams(dimension_semantics=sem, vmem_limit_bytes=VMEM_LIMIT)


def _mm(a, b, dn=None):
    if dn is None:
        return jnp.dot(a, b, preferred_element_type=F32)
    return lax.dot_general(a, b, dn, preferred_element_type=F32)


def _split2(x):
    hi = x.astype(BF16)
    lo = (x - hi.astype(F32)).astype(BF16)
    return hi, lo


def _split3(x):
    hi = x.astype(BF16)
    r = x - hi.astype(F32)
    mid = r.astype(BF16)
    lo = (r - mid.astype(F32)).astype(BF16)
    return hi, mid, lo


def _mm3(a, b):
    ah, al = _split2(a)
    bh, bl = _split2(b)
    return _mm(ah, bh) + (_mm(ah, bl) + _mm(al, bh))


def _mm_exact_lhs(a_bf, x):
    hi, mid, lo = _split3(x)
    return _mm(a_bf, hi) + (_mm(a_bf, mid) + _mm(a_bf, lo))


def _sigmoid(x):
    return 1.0 / (1.0 + jnp.exp(-x))


def _silu(x):
    return x * _sigmoid(x)


def _softplus(x):
    return jnp.maximum(x, 0.0) + jnp.log(1.0 + jnp.exp(-jnp.abs(x)))


def _rmsnorm(xf, g):
    return xf * lax.rsqrt(jnp.mean(xf * xf, axis=-1, keepdims=True) + EPS) * g


def _iota(shape, dim):
    return lax.broadcasted_iota(jnp.int32, shape, dim)


def _row_to_col(row, n):
    eye = _iota((n, n), 0) == _iota((n, n), 1)
    return jnp.sum(jnp.where(eye, jnp.broadcast_to(row, (n, n)), 0.0), axis=1, keepdims=True)


def _ffn_kernel(x_ref, g_ref, wg_ref, wu_ref, wo_ref, *rest):
    *mix_refs, o_ref, h_ref = rest
    j = pl.program_id(1)

    @pl.when(j == 0)
    def _():
        xf = x_ref[...]
        if mix_refs:
            wm_ref = mix_refs[-1]
            off = 0
            for m_ref in mix_refs[:-1]:
                n = m_ref.shape[1]
                xf = xf + _mm(m_ref[...], wm_ref[off:off + n, :])
                off += n
        h_ref[...] = _rmsnorm(xf, g_ref[...]).astype(BF16)
        o_ref[...] = xf

    h = h_ref[...]
    gate = _mm(h, wg_ref[...])
    up = _mm(h, wu_ref[...])
    a = (_silu(gate) * up).astype(BF16)
    o_ref[...] += 0.5 * _mm(a, wo_ref[...])


def _ffn(x, g, w_in, w_out, layer, which, tm, mix=(), w_mix=None, li=0):
    m, d = x.shape
    f = w_out.shape[2]
    nf = 2
    tf = f // nf
    assert f % nf == 0 and tf % LANES == 0 and m % tm == 0
    mix_specs = [pl.BlockSpec((tm, o.shape[1]), lambda i, j: (i, 0)) for o in mix]
    if mix:
        assert sum(o.shape[1] for o in mix) == w_mix.shape[1]
        mix_specs.append(pl.BlockSpec((None, w_mix.shape[1], d), lambda i, j: (li, 0, 0)))
    return pl.pallas_call(
        _ffn_kernel,
        grid=(m // tm, nf),
        in_specs=[
            pl.BlockSpec((tm, d), lambda i, j: (i, 0)),
            pl.BlockSpec((None, 1, d), lambda i, j: (layer, 0, 0)),
            pl.BlockSpec((None, None, d, tf), lambda i, j: (layer, which, 0, j)),
            pl.BlockSpec((None, None, d, tf), lambda i, j: (layer, which, 0, j + nf)),
            pl.BlockSpec((None, None, tf, d), lambda i, j: (layer, which, j, 0)),
        ] + mix_specs,
        out_specs=pl.BlockSpec((tm, d), lambda i, j: (i, 0)),
        out_shape=jax.ShapeDtypeStruct((m, d), F32),
        scratch_shapes=[pltpu.VMEM((tm, d), BF16)],
        compiler_params=_cp("parallel", "arbitrary"),
        name="ffn",
    )(x, g, w_in, w_in, w_out, *mix, *([w_mix] if mix else []))


def _group_rms(x, grp_bf, gain, width):
    hi, lo = _split2(x * x)
    ms = (_mm(hi, grp_bf) + _mm(lo, grp_bf)) * (1.0 / width)
    return x * lax.rsqrt(ms + EPS) * gain


def _inproj_even_kernel(x_ref, g_ref, w_ref, ws_ref, grp_ref, qg_ref, kg_ref, gp_ref,
                        q_ref, k32_ref, kbf_ref, v32_ref, vt_ref, bqkv_ref, bz_ref, gb_ref):
    h = _rmsnorm(x_ref[...], g_ref[...]).astype(BF16)
    grp = grp_ref[...]

    aq = _mm(h, w_ref[:, 0:A_QK])
    q = _group_rms(aq, grp, qg_ref[...], A_DK)
    q_ref[...] = (q * (A_DK ** -0.5 * LOG2E)).astype(BF16)

    ak = _mm(h, w_ref[:, A_QK:2 * A_QK])
    k = _group_rms(ak, grp, kg_ref[...], A_DK)
    kbf_ref[...] = k.astype(BF16)

    v = _mm(h, w_ref[:, 2 * A_QK:2 * A_QK + A_V])
    tm = v.shape[0]
    ones = jnp.ones((VT_ROWS - A_DV, tm), BF16)
    for hd in range(A_HEADS):
        k32_ref[pl.ds(hd, tm, stride=A_HEADS), :] = k[:, 2 * A_DK * hd:2 * A_DK * (hd + 1)]
        v32_ref[pl.ds(hd, tm, stride=A_HEADS), :] = v[:, A_DV * hd:A_DV * (hd + 1)]
        vt_ref[hd, 0:A_DV, :] = v[:, A_DV * hd:A_DV * (hd + 1)].T.astype(BF16)
        vt_ref[hd, A_DV:VT_ROWS, :] = ones

    o = 2 * A_QK + A_V
    bqkv_ref[...] = _mm(h, w_ref[:, o:o + B_CONV_DIM])
    o += B_CONV_DIM
    bz_ref[...] = _mm(h, w_ref[:, o:o + B_V])

    y = _mm(h, ws_ref[...])
    lane = _iota(y.shape, 1)
    g = gp_ref[0:1, :] * _softplus(y + gp_ref[1:2, :])
    gb_ref[...] = jnp.where(lane < B_HEADS, g, jnp.where(lane < 2 * B_HEADS, _sigmoid(y), 0.0))


def _inproj_even(x, g, w, ws, grp, qg, kg, gp, layer, li, tm):
    m, d = x.shape
    wmain = IN_EVEN_MAIN
    row = lambda n: pl.BlockSpec((tm, n), lambda i: (i, 0))
    out_shapes = [
        jax.ShapeDtypeStruct((m, A_QK), BF16),
        jax.ShapeDtypeStruct((m * A_HEADS, 2 * A_DK), F32),
        jax.ShapeDtypeStruct((m, A_QK), BF16),
        jax.ShapeDtypeStruct((m * A_HEADS, A_DV), F32),
        jax.ShapeDtypeStruct((A_HEADS, m // tm, VT_ROWS, tm), BF16),
        jax.ShapeDtypeStruct((m, B_CONV_DIM), F32),
        jax.ShapeDtypeStruct((m, B_V), F32),
        jax.ShapeDtypeStruct((m, LANES), F32),
    ]
    out_specs = [row(s.shape[1]) for s in out_shapes]
    out_specs[1] = pl.BlockSpec((tm * A_HEADS, 2 * A_DK), lambda i: (i, 0))
    out_specs[3] = pl.BlockSpec((tm * A_HEADS, A_DV), lambda i: (i, 0))
    out_specs[4] = pl.BlockSpec((A_HEADS, None, VT_ROWS, tm), lambda i: (0, i, 0, 0))
    return pl.pallas_call(
        _inproj_even_kernel,
        grid=(m // tm,),
        in_specs=[
            row(d),
            pl.BlockSpec((None, 1, d), lambda i: (layer, 0, 0)),
            pl.BlockSpec((None, d, wmain), lambda i: (li, 0, 0)),
            pl.BlockSpec((None, d, LANES), lambda i: (li, 0, 0)),
            pl.BlockSpec((A_QK, A_QK), lambda i: (0, 0)),
            pl.BlockSpec((None, 1, A_QK), lambda i: (li, 0, 0)),
            pl.BlockSpec((None, 1, A_QK), lambda i: (li, 0, 0)),
            pl.BlockSpec((None, 2, LANES), lambda i: (li, 0, 0)),
        ],
        out_specs=out_specs,
        out_shape=out_shapes,
        compiler_params=_cp("parallel"),
        name="inproj_even",
    )(x, g, w, ws, grp, qg, kg, gp)


def _bias_kernel(rb_ref, bp_ref, bs_ref, *, tq, tk, past, page, tnew):
    h = pl.program_id(0)
    far = rb_ref[N_BUCKETS - 1, h]
    exact = N_BUCKETS // 2

    def table(rel):
        n = jnp.maximum(rel, 0)
        nf = jnp.maximum(n, 1).astype(F32)
        large = exact + (jnp.log(nf / exact) / math.log(MAX_DIST / exact) * (N_BUCKETS - exact)).astype(jnp.int32)
        bucket = jnp.where(n < exact, n, jnp.minimum(large, N_BUCKETS - 1))
        val = jnp.zeros(rel.shape, F32)
        for b in range(N_BUCKETS):
            val = jnp.where(bucket == b, rb_ref[b, h], val)
        return jnp.where(rel >= 0, (val - far) * LOG2E, NEG)

    rows = SUBLANES

    def fill(c, carry):
        r0 = pl.multiple_of(c * rows, rows)
        key_p = r0 + _iota((rows, tq), 0)
        qry_p = _iota((rows, tq), 1)
        bp_ref[0, pl.ds(r0, rows), :] = table(qry_p - (key_p - tk))
        t1 = table(qry_p - key_p)
        bp_ref[1, pl.ds(r0, rows), :] = t1
        for n in range(2, tq // tk + 1):
            off = (n - 1) * tk
            bp_ref[n, pl.ds(r0, rows), 0:off] = jnp.full((rows, off), NEG, F32)
            bp_ref[n, pl.ds(r0, rows), off:tq] = t1[:, 0:tq - off]
        return carry

    lax.fori_loop(0, tk // rows, fill, 0)

    ncol = page * A_HEADS + LANES
    t_idx = _iota((2 * tnew, ncol), 0) & (tnew - 1)
    col = _iota((2 * tnew, ncol), 1)
    key = past - page + (col >> 2)
    rel_new = jnp.where(key - past < tnew, t_idx - (key - past), -1)
    rel = jnp.where(key < past, past + t_idx - key, rel_new)
    bs_ref[...] = table(jnp.where((col & (A_HEADS - 1)) == h, rel, -1))


def _bias_tiles(rel_bias, tq, tk, past, page, tnew):
    assert tnew & (tnew - 1) == 0 and 2 * tnew == SUBLANES and A_HEADS == 4 and tq % tk == 0
    ncol = page * A_HEADS + LANES
    ntile = tq // tk + 1
    return pl.pallas_call(
        functools.partial(_bias_kernel, tq=tq, tk=tk, past=past, page=page, tnew=tnew),
        grid=(A_HEADS,),
        in_specs=[pl.BlockSpec(memory_space=pltpu.SMEM)],
        out_specs=[
            pl.BlockSpec((None, ntile, tk, tq), lambda h: (h, 0, 0, 0)),
            pl.BlockSpec((None, 2 * tnew, ncol), lambda h: (h, 0, 0)),
        ],
        out_shape=[
            jax.ShapeDtypeStruct((A_HEADS, ntile, tk, tq), F32),
            jax.ShapeDtypeStruct((A_HEADS, 2 * tnew, ncol), F32),
        ],
        compiler_params=_cp("parallel"),
        name="t5_bias",
    )(rel_bias)


def _far_bucket_saturated(dist):
    exact = N_BUCKETS // 2
    v = math.log(dist / exact) / math.log(MAX_DIST / exact) * (N_BUCKETS - exact)
    return dist >= exact and exact + v >= N_BUCKETS - 1 + 1e-3


def _attn_prompt_kernel(sc_ref, q_ref, k_ref, vt_ref, bias_ref, sg_ref, o_ref, *, tq, tk):
    head = pl.program_id(1)
    i = pl.program_id(2)
    q = q_ref[...]
    lane = _iota(q.shape, 1)
    zero = jnp.zeros_like(q)
    qbd = jnp.concatenate([jnp.where(lane < A_DK, q, zero), jnp.where(lane >= A_DK, q, zero)], axis=0)
    ratio = tq // tk

    def biased(j, bias):
        start = pl.multiple_of(j * tk, tk)
        s = _mm(k_ref[pl.ds(start, tk), :], qbd, NT)
        return s if bias is None else s + jnp.concatenate([bias, bias], axis=1)

    def key_blocks(step, carry):
        first = i * ratio

        def trip(j, c):
            for u in range(ratio):
                c = step(ratio * j + u, c, None)
            return c

        def overlapping(c):
            for n in range(ratio):
                c = step(first + n, c, bias_ref[n + 1])
            return c

        def previous_tile(c):
            for u in range(ratio - 1):
                c = step(first - ratio + u, c, None)
            return overlapping(step(first - 1, c, bias_ref[0]))

        carry = lax.fori_loop(0, jnp.maximum(i - 1, 0), trip, carry)
        return lax.cond(i >= 1, previous_tile, overlapping, carry)

    def fixed_shift(_):
        shift = sc_ref[3 + head]

        def step(j, acc, bias):
            return acc + _mm(vt_ref[j], jnp.exp2(biased(j, bias) - shift).astype(BF16))

        return key_blocks(step, jnp.zeros((VT_ROWS, 2 * tq), F32))

    def running_max(_):
        def step(j, c, bias):
            m, acc = c
            s = biased(j, bias)
            m_new = jnp.maximum(m, jnp.max(s, axis=0, keepdims=True))
            p = jnp.exp2(s - m_new).astype(BF16)
            return m_new, jnp.exp2(m - m_new) * acc + _mm(vt_ref[j], p)

        return key_blocks(step, (jnp.full((1, 2 * tq), NEG, F32), jnp.zeros((VT_ROWS, 2 * tq), F32)))[1]

    acc = lax.cond(sc_ref[2] > 0.5, fixed_shift, running_max, 0)

    o = acc[0:A_DV] / acc[A_DV:A_DV + 1]
    o = o[:, 0:tq] - sc_ref[0] * o[:, tq:2 * tq]
    o = o * lax.rsqrt(jnp.mean(o * o, axis=0, keepdims=True) + EPS)
    o_ref[...] = (o.T * (sg_ref[...] * sc_ref[1])).astype(BF16)


def _attn_prompt(sc, q, k, vt, bias_p, sg, li, tq):
    bsz, t, _ = q.shape
    tk = vt.shape[3]
    assert tq % tk == 0 and t % tq == 0 and bias_p.shape[1:] == (tq // tk + 1, tk, tq)
    return pl.pallas_call(
        functools.partial(_attn_prompt_kernel, tq=tq, tk=tk),
        grid=(bsz, A_HEADS, t // tq),
        in_specs=[
            pl.BlockSpec(memory_space=pltpu.SMEM),
            pl.BlockSpec((None, tq, 2 * A_DK), lambda b, h, i: (b, i, h)),
            pl.BlockSpec((None, t, 2 * A_DK), lambda b, h, i: (b, 0, h)),
            pl.BlockSpec((None, t // tk, VT_ROWS, tk), lambda b, h, i: (h, b, 0, 0)),
            pl.BlockSpec((None, tq // tk + 1, tk, tq), lambda b, h, i: (h, 0, 0, 0)),
            pl.BlockSpec((None, 1, A_DV), lambda b, h, i: (li, 0, 0)),
        ],
        out_specs=pl.BlockSpec((None, tq, A_DV), lambda b, h, i: (b, i, h)),
        out_shape=jax.ShapeDtypeStruct((bsz, t, A_V), BF16),
        compiler_params=_cp("parallel", "parallel", "arbitrary"),
        name="attn_prompt",
    )(sc, q, k, vt, bias_p, sg)


def _attn_sample_kernel(pt_ref, sc_ref, qbd_ref, bias_ref, biasn_ref, kn_ref, vn_ref, sg_ref, *rest, npg, tnew):
    k_refs = rest[:npg]
    v_refs = rest[npg:2 * npg]
    o_ref, m_ref, l_ref, acc_ref, mask_ref = rest[2 * npg:]
    pg = pl.program_id(1)
    last = pg == pl.num_programs(1) - 1
    qbd = qbd_ref[...]

    @pl.when(pg == 0)
    def _():
        m_ref[...] = jnp.full(m_ref.shape, NEG, F32)
        l_ref[...] = jnp.zeros(l_ref.shape, F32)
        acc_ref[...] = jnp.zeros(acc_ref.shape, F32)
        row_head = _iota(mask_ref.shape, 0) >> ((2 * tnew).bit_length() - 1)
        col_head = _iota(mask_ref.shape, 1) & (A_HEADS - 1)
        mask_ref[...] = jnp.where(row_head == col_head, 0.0, NEG)

    def update(s, vs):
        m_old = m_ref[...]
        m_new = jnp.maximum(m_old, jnp.max(s, axis=1, keepdims=True))
        alpha = jnp.exp2(m_old - m_new)
        p = jnp.exp2(s - m_new)
        l_ref[...] = alpha * l_ref[...] + jnp.sum(p, axis=1, keepdims=True)
        pb = p.astype(BF16)
        w = pb.shape[1] // len(vs)
        pv = _mm(pb[:, 0:w], vs[0])
        for g in range(1, len(vs)):
            pv = pv + _mm(pb[:, g * w:(g + 1) * w], vs[g])
        acc_ref[...] = alpha * acc_ref[...] + pv
        m_ref[...] = m_new

    mask = mask_ref[...]
    final_page = jnp.where(last, bias_ref[...], mask)
    s = jnp.concatenate([_mm(qbd, kr[...].astype(BF16), NT) + (mask if g < npg - 1 else final_page)
                         for g, kr in enumerate(k_refs)], axis=1)
    update(s, [vr[...].astype(BF16) for vr in v_refs])

    @pl.when(last)
    def _():
        update(_mm(qbd, kn_ref[...], NT) + biasn_ref[...], [vn_ref[...]])
        lam = sc_ref[0]
        rows = 2 * tnew
        for h in range(A_HEADS):
            a = acc_ref[rows * h:rows * (h + 1), :] / l_ref[rows * h:rows * (h + 1), :]
            o = a - lam * pltpu.roll(a, tnew, axis=0)
            o_ref[:, A_DV * h:A_DV * (h + 1)] = _rmsnorm(o, sg_ref[...]) * sc_ref[1]


def _attn_sample(page_table, sc, qbd, bias_s, knew, vnew, sg, cache_k, cache_v, li, tnew):
    bsz, nrow, _ = qbd.shape
    n_pages = page_table.shape[1]
    npg = PAGES_PER_STEP
    assert n_pages % npg == 0
    prow = cache_k.shape[2]
    assert bias_s.shape == (nrow, prow + LANES)

    def kvspec(g):
        return pl.BlockSpec((None, None, prow, 2 * A_DK), lambda b, pg, pt: (li, pt[b, pg * npg + g], 0, 0))

    grid_spec = pltpu.PrefetchScalarGridSpec(
        num_scalar_prefetch=1,
        grid=(bsz, n_pages // npg),
        in_specs=[
            pl.BlockSpec(memory_space=pltpu.SMEM),
            pl.BlockSpec((None, nrow, 2 * A_DK), lambda b, pg, pt: (b, 0, 0)),
            pl.BlockSpec((nrow, prow), lambda b, pg, pt: (0, 0)),
            pl.BlockSpec((nrow, LANES), lambda b, pg, pt: (0, prow // LANES)),
            pl.BlockSpec((None, LANES, 2 * A_DK), lambda b, pg, pt: (b, 0, 0)),
            pl.BlockSpec((None, LANES, A_DV), lambda b, pg, pt: (b, 0, 0)),
            pl.BlockSpec((None, 1, A_DV), lambda b, pg, pt: (li, 0, 0)),
        ] + [kvspec(g) for g in range(npg)] + [kvspec(g) for g in range(npg)],
        out_specs=pl.BlockSpec((None, 2 * tnew, A_V), lambda b, pg, pt: (b, 0, 0)),
        scratch_shapes=[pltpu.VMEM((nrow, 1), F32), pltpu.VMEM((nrow, 1), F32), pltpu.VMEM((nrow, A_DV), F32),
                        pltpu.VMEM((nrow, prow), F32)],
    )
    return pl.pallas_call(
        functools.partial(_attn_sample_kernel, npg=npg, tnew=tnew),
        grid_spec=grid_spec,
        out_shape=jax.ShapeDtypeStruct((bsz, 2 * tnew, A_V), F32),
        compiler_params=_cp("parallel", "arbitrary"),
        name="attn_sample",
    )(page_table, sc, qbd, bias_s, bias_s, knew, vnew, sg, *([cache_k] * npg), *([cache_v] * npg))


def _l2norm(x):
    return x * lax.rsqrt(jnp.sum(x * x, axis=-1, keepdims=True) + EPS)


def _unit_lower_inverses(mats, c):
    n = mats[0].shape[0]
    nblk = n // c
    shift = c.bit_length() - 1
    same = (_iota((n, n), 0) >> shift) == (_iota((n, n), 1) >> shift)
    eye_w = ((_iota((c, n), 1) & (c - 1)) == _iota((c, n), 0)).astype(F32)

    def wide(sq):
        out = sq[0:c]
        for r in range(1, nblk):
            out = out + sq[r * c:(r + 1) * c]
        return out

    def square(w):
        return jnp.where(same, jnp.concatenate([w] * nblk, axis=0), jnp.zeros((), w.dtype))

    def mm3_split(lhs_split, rhs_split):
        (lh, ll), (rh, rl) = lhs_split, rhs_split
        return _mm(lh, rh) + (_mm(lh, rl) + _mm(ll, rh))

    qw = [_split2(wide(a)) for a in mats]
    qsq = [(square(h), square(l)) for h, l in qw]
    ps = [eye_w - wide(a) for a in mats]
    span = 2
    while span < c:
        qw = [_split2(mm3_split(q, s)) for q, s in zip(qw, qsq)]
        qsq = [(square(h), square(l)) for h, l in qw]
        ps = [p + mm3_split(_split2(p), s) for p, s in zip(ps, qsq)]
        span *= 2
    return [square(p) for p in ps]


def _cumsum_rows(x, n):
    row = _iota(x.shape, 0)
    k = 1
    while k < n:
        x = x + jnp.where(row >= k, pltpu.roll(x, k, axis=0), 0.0)
        k *= 2
    return x


def _delta_prep_kernel(x_ref, halo_ref, c0_ref, gb_ref, cw_ref, u_ref, wk_ref, qe_ref, kd_ref, qk_ref, xbuf,
                       *, c, nck):
    ci = pl.program_id(1)
    ch = B_HEADS * c
    rows = nck * c
    xbuf[0:SUBLANES, :] = jnp.where(ci == 0, c0_ref[...], halo_ref[...])
    xbuf[SUBLANES:SUBLANES + rows, :] = x_ref[...]
    first = SUBLANES - (CONV_W - 1)
    conv = xbuf[pl.ds(first, rows), :] * cw_ref[0:1, :]
    for j in range(1, CONV_W):
        conv = conv + xbuf[pl.ds(first + j, rows), :] * cw_ref[j:j + 1, :]
    conv = _silu(conv)

    row = _iota((ch, ch), 0)
    col = _iota((ch, ch), 1)
    shift = c.bit_length() - 1
    same = (row >> shift) == (col >> shift)
    incl = same & (row >= col)
    strict = same & (row > col)

    def stack_lane(a, off):
        return jnp.concatenate([a[:, off + h:off + h + 1] for h in range(B_HEADS)], axis=0)

    parts = []
    for n in range(nck):
        rs = slice(n * c, (n + 1) * c)

        def stack(off, width):
            return jnp.concatenate([conv[rs, off + width * h:off + width * (h + 1)] for h in range(B_HEADS)], axis=0)

        q = _l2norm(stack(0, B_DK)) * (B_DK ** -0.5)
        k = _l2norm(stack(B_QK, B_DK))
        v = stack(2 * B_QK, B_DV)
        gb = gb_ref[rs, :]
        gcum = _cumsum_rows(gb, c)
        gcol = stack_lane(gcum, 0)
        beta = stack_lane(gb, B_HEADS)
        glast = stack_lane(jnp.broadcast_to(gcum[c - 1:c, :], gcum.shape), 0)
        grow = jnp.broadcast_to(gcol, (ch, LANES)).T[0:1, :]
        dec = jnp.where(incl, jnp.exp(jnp.where(incl, gcol - grow, 0.0)), 0.0)
        kb = k.astype(BF16)
        a = jnp.where(strict, _mm(kb, kb, NT) * dec * beta, 0.0)
        egc = jnp.exp(gcol)
        qe_ref[n] = (q * egc).astype(BF16)
        kd_ref[n] = (k * jnp.exp(glast - gcol)).astype(BF16)
        qk_ref[n] = (_mm(q.astype(BF16), kb, NT) * dec).astype(BF16)
        parts.append((a, jnp.concatenate([v * beta, k * (beta * egc)], axis=1)))

    t_invs = _unit_lower_inverses([a for a, _ in parts], c)
    for n in range(nck):
        uw = _mm3(t_invs[n], parts[n][1])
        u_ref[n] = uw[:, 0:B_DV]
        wk_ref[n] = uw[:, B_DV:B_DV + B_DK].astype(BF16)


def _delta_scan_kernel(u_ref, wk_ref, qe_ref, kd_ref, qk_ref, gb_ref, z_ref, s0_ref, bn_ref, ob_ref, sout_ref,
                       s_scr, *, c, nb, cps):
    ci = pl.program_id(1)

    @pl.when(ci == 0)
    def _():
        s_scr[...] = s0_ref[...]

    hs = [slice(c * h, c * (h + 1)) for h in range(B_HEADS)]
    for n in range(cps):
        rows = slice(c * n, c * (n + 1))
        for b in range(nb):
            decay = jnp.exp(jnp.sum(gb_ref[b, rows, :], axis=0, keepdims=True))
            s = [s_scr[b, h] for h in range(B_HEADS)]
            s_bf = [x.astype(BF16) for x in s]
            v_new = u_ref[b, n] - jnp.concatenate(
                [_mm(wk_ref[b, n, hs[h], :], s_bf[h]) for h in range(B_HEADS)], axis=0)
            vn_bf = v_new.astype(BF16)
            o = (jnp.concatenate([_mm(qe_ref[b, n, hs[h], :], s_bf[h]) for h in range(B_HEADS)], axis=0)
                 + _mm(qk_ref[b, n], vn_bf))
            for h in range(B_HEADS):
                s_scr[b, h] = s[h] * decay[:, h:h + 1] + _mm(kd_ref[b, n, hs[h], :], vn_bf[hs[h]], TN)
                sl = slice(B_DV * h, B_DV * (h + 1))
                ob_ref[b, rows, sl] = (_rmsnorm(o[hs[h]], bn_ref[...]) * _silu(z_ref[b, rows, sl])).astype(BF16)

    @pl.when(ci == pl.num_programs(1) - 1)
    def _():
        sout_ref[...] = s_scr[...]


def _delta(bqkv, bz, gb, conv0, s0, cw, bn, li, c):
    bsz, t, _ = bqkv.shape
    assert t % c == 0 and c % SUBLANES == 0 and c & (c - 1) == 0
    nch = t // c
    ch = B_HEADS * c
    nck = min(DELTA_PREP_CHUNKS, nch)
    assert nch % nck == 0
    hb = nck * c // SUBLANES
    tok = lambda n: pl.BlockSpec((None, c, n), lambda b, ci: (b, ci, 0))
    step_tok = lambda n: pl.BlockSpec((None, nck * c, n), lambda b, ci: (b, ci, 0))
    step_chunks = lambda n: pl.BlockSpec((None, nck, ch, n), lambda b, ci: (b, ci, 0, 0))
    chunk_shape = lambda n, dt: jax.ShapeDtypeStruct((bsz, nch, ch, n), dt)
    u, wk, qe, kd, qk = pl.pallas_call(
        functools.partial(_delta_prep_kernel, c=c, nck=nck),
        grid=(bsz, nch // nck),
        in_specs=[
            step_tok(B_CONV_DIM),
            pl.BlockSpec((None, SUBLANES, B_CONV_DIM), lambda b, ci: (b, jnp.maximum(ci * hb - 1, 0), 0)),
            pl.BlockSpec((None, SUBLANES, B_CONV_DIM), lambda b, ci: (b, 0, 0)),
            step_tok(LANES),
            pl.BlockSpec((None, CONV_W, B_CONV_DIM), lambda b, ci: (li, 0, 0)),
        ],
        out_specs=[step_chunks(B_DV), step_chunks(B_DK), step_chunks(B_DK), step_chunks(B_DK), step_chunks(ch)],
        out_shape=[chunk_shape(B_DV, F32), chunk_shape(B_DK, BF16), chunk_shape(B_DK, BF16),
                   chunk_shape(B_DK, BF16), chunk_shape(ch, BF16)],
        scratch_shapes=[pltpu.VMEM((SUBLANES + nck * c, B_CONV_DIM), F32)],
        compiler_params=_cp("parallel", "parallel"),
        name="delta_prep",
    )(bqkv, bqkv, conv0, gb, cw)

    nb = 2
    cps = DELTA_SCAN_CHUNKS if nch % DELTA_SCAN_CHUNKS == 0 else 1
    assert bsz % nb == 0
    grp_chunk = lambda n: pl.BlockSpec((nb, cps, ch, n), lambda b, ci: (b, ci, 0, 0))
    grp_tok = lambda n: pl.BlockSpec((nb, cps * c, n), lambda b, ci: (b, ci, 0))
    state = pl.BlockSpec((nb, B_HEADS, B_DK, B_DV), lambda b, ci: (b, 0, 0, 0))
    return pl.pallas_call(
        functools.partial(_delta_scan_kernel, c=c, nb=nb, cps=cps),
        grid=(bsz // nb, nch // cps),
        in_specs=[grp_chunk(B_DV), grp_chunk(B_DK), grp_chunk(B_DK), grp_chunk(B_DK), grp_chunk(ch),
                  grp_tok(LANES), grp_tok(B_V), state,
                  pl.BlockSpec((None, 1, B_DV), lambda b, ci: (li, 0, 0))],
        out_specs=[grp_tok(B_V), state],
        out_shape=[jax.ShapeDtypeStruct((bsz, t, B_V), BF16),
                   jax.ShapeDtypeStruct((bsz, B_HEADS, B_DK, B_DV), F32)],
        scratch_shapes=[pltpu.VMEM((nb, B_HEADS, B_DK, B_DV), F32)],
        compiler_params=_cp("parallel", "arbitrary"),
        name="delta_scan",
    )(u, wk, qe, kd, qk, gb, bz, s0, bn)


def _inproj_odd_kernel(x_ref, g_ref, w_ref, ws_ref, w2_ref, gbias_ref, q_ref, k_ref, v_ref, r_ref, gk_ref):
    h = _rmsnorm(x_ref[...], g_ref[...]).astype(BF16)
    q_ref[...] = _mm(h, w_ref[:, 0:C_QK])
    k_ref[...] = _mm(h, w_ref[:, C_QK:2 * C_QK])
    v_ref[...] = _mm(h, w_ref[:, 2 * C_QK:2 * C_QK + C_V]).astype(BF16)
    r_ref[...] = _mm(h, w_ref[:, 2 * C_QK + C_V:2 * C_QK + 2 * C_V])
    glr = _mm(h, ws_ref[...]).astype(BF16)
    y = _mm(glr, w2_ref[...]) + gbias_ref[...]
    gk_ref[...] = -_softplus(-y) * (1.0 / GLA_TAU)


def _inproj_odd(x, g, w, ws, w2, gbias, layer, li, tm):
    m, d = x.shape
    wmain = 2 * C_QK + 2 * C_V
    row = lambda n: pl.BlockSpec((tm, n), lambda i: (i, 0))
    out_shapes = [
        jax.ShapeDtypeStruct((m, C_QK), F32),
        jax.ShapeDtypeStruct((m, C_QK), F32),
        jax.ShapeDtypeStruct((m, C_V), BF16),
        jax.ShapeDtypeStruct((m, C_V), F32),
        jax.ShapeDtypeStruct((m, C_QK), F32),
    ]
    return pl.pallas_call(
        _inproj_odd_kernel,
        grid=(m // tm,),
        in_specs=[
            row(d),
            pl.BlockSpec((None, 1, d), lambda i: (layer, 0, 0)),
            pl.BlockSpec((None, d, wmain), lambda i: (li, 0, 0)),
            pl.BlockSpec((None, d, LANES), lambda i: (li, 0, 0)),
            pl.BlockSpec((None, LANES, C_QK), lambda i: (li, 0, 0)),
            pl.BlockSpec((None, 1, C_QK), lambda i: (li, 0, 0)),
        ],
        out_specs=[row(s.shape[1]) for s in out_shapes],
        out_shape=out_shapes,
        compiler_params=_cp("parallel"),
        name="inproj_odd",
    )(x, g, w, ws, w2, gbias)


def _gla_kernel(q_ref, k_ref, v_ref, r_ref, gk_ref, s0_ref, cn_ref, o_ref, sout_ref, s_scr, *, c, sub, nb):
    ci = pl.program_id(1)

    @pl.when(ci == 0)
    def _():
        s_scr[...] = s0_ref[...]

    for b in range(nb):
        _gla_chunk(q_ref.at[b], k_ref.at[b], v_ref.at[b], r_ref.at[b], gk_ref.at[b], cn_ref, o_ref.at[b],
                   s_scr.at[b], c, sub)

    @pl.when(ci == pl.num_programs(1) - 1)
    def _():
        sout_ref[...] = s_scr[...]


def _gla_chunk(q_ref, k_ref, v_ref, r_ref, gk_ref, cn_ref, o_ref, s_scr, c, sub):
    row = _iota((c, c), 0)
    col = _iota((c, c), 1)
    lower_bf = (row >= col).astype(BF16)
    in_sub = row & (sub - 1)
    sub_start = row - in_sub
    b_all = _mm_exact_lhs(lower_bf, gk_ref[...])

    for h in range(C_HEADS):
        sl = slice(C_DK * h, C_DK * (h + 1))
        vl = slice(C_DV * h, C_DV * (h + 1))
        q = q_ref[:, sl] * (C_DK ** -0.5)
        k = k_ref[:, sl]
        b = b_all[:, sl]
        v = v_ref[:, vl]
        s = s_scr[h]

        o = _mm((q * jnp.exp(b)).astype(BF16), s.astype(BF16))

        blocks = [jnp.zeros((sub, c), F32)]
        for r in range(1, c // sub):
            ref = b[sub * r:sub * r + 1, :]
            qt = q[sub * r:sub * (r + 1)] * jnp.exp(b[sub * r:sub * (r + 1)] - ref)
            kt = k * jnp.exp(jnp.minimum(ref - b, 0.0))
            blocks.append(_mm(qt.astype(BF16), kt.astype(BF16), NT))
        a = jnp.where(col < sub_start, jnp.concatenate(blocks, axis=0), 0.0)
        b2 = b * LOG2E
        for d in range(sub):
            kd = k if d == 0 else pltpu.roll(k, d, axis=0)
            bd = b2 if d == 0 else pltpu.roll(b2, d, axis=0)
            ad = jnp.sum(q * kd * jnp.exp2(b2 - bd), axis=1, keepdims=True)
            a = jnp.where((col == row - d) & (in_sub >= d), ad, a)
        o = o + _mm(a.astype(BF16), v)

        blast = b[c - 1:c, :]
        kdec = (k * jnp.exp(blast - b)).astype(BF16)
        s_scr[h] = s * _row_to_col(jnp.exp(blast), C_DK) + _mm(kdec, v, TN)

        o_ref[:, vl] = (_rmsnorm(o, cn_ref[...]) * _silu(r_ref[:, vl])).astype(BF16)


def _gla(q, k, v, r, gk, s0, cn, li, c):
    bsz, t, _ = q.shape
    nb = 2
    assert t % c == 0 and bsz % nb == 0
    sub = min(GLA_SUB, c)
    tok = lambda n: pl.BlockSpec((nb, c, n), lambda b, ci: (b, ci, 0))
    state = pl.BlockSpec((nb, C_HEADS, C_DK, C_DV), lambda b, ci: (b, 0, 0, 0))
    return pl.pallas_call(
        functools.partial(_gla_kernel, c=c, sub=sub, nb=nb),
        grid=(bsz // nb, t // c),
        in_specs=[tok(C_QK), tok(C_QK), tok(C_V), tok(C_V), tok(C_QK), state,
                  pl.BlockSpec((None, 1, C_DV), lambda b, ci: (li, 0, 0))],
        out_specs=[tok(C_V), state],
        out_shape=[jax.ShapeDtypeStruct((bsz, t, C_V), BF16),
                   jax.ShapeDtypeStruct((bsz, C_HEADS, C_DK, C_DV), F32)],
        scratch_shapes=[pltpu.VMEM((nb, C_HEADS, C_DK, C_DV), F32)],
        compiler_params=_cp("parallel", "arbitrary"),
        name="gla",
    )(q, k, v, r, gk, s0, cn)


def _pad_rows(a, rows):
    return jnp.pad(a, ((0, 0), (0, rows - a.shape[1]), (0, 0)))


def _pad_lanes(a):
    return jnp.pad(a, [(0, 0)] * (a.ndim - 1) + [(0, LANES - a.shape[-1])])


def kernel(x_prompt, x_sample, cache_k, cache_v, state_delta, state_conv, state_gla, page_table, rel_bias, norm_ffn1, norm_mix, norm_ffn2, ffn_in, ffn_out, w_in_even, a_qk_norm, a_lambda, a_subln, b_conv, b_a_log, b_dt_bias, b_norm, w_out_even, w_in_odd, c_gate_w2, c_gate_b, c_norm, w_out_odd):
    bp, tp, d = x_prompt.shape
    bs, ts, _ = x_sample.shape
    depth = norm_ffn1.shape[0]
    n_att = w_in_even.shape[0]
    n_pool, page = cache_k.shape[1], cache_k.shape[2]
    past = page_table.shape[1] * page
    tq = ATT_Q_TILE
    cs = SUBLANES
    assert page == LANES and tp % tq == 0 and ts <= cs and ts >= CONV_W - 1
    assert _far_bucket_saturated(ATT_K_TILE + 1) and _far_bucket_saturated(page + 1)

    ffn_in_bf = ffn_in.astype(BF16)
    ffn_out_bf = ffn_out.astype(BF16)
    w_even_bf = w_in_even.astype(BF16)
    w_even_small = _pad_lanes(w_in_even[:, :, IN_EVEN_MAIN:]).astype(BF16)
    w_out_even_bf = w_out_even.astype(BF16)
    w_odd_bf = w_in_odd.astype(BF16)
    w_odd_small = _pad_lanes(w_in_odd[:, :, IN_ODD_MAIN:]).astype(BF16)
    w_out_odd_bf = w_out_odd.astype(BF16)
    gate_w2 = jnp.pad(c_gate_w2, ((0, 0), (0, LANES - GLA_LOWRANK), (0, 0))).astype(BF16)
    gate_b = c_gate_b[:, None, :]
    n1 = norm_ffn1[:, None, :]
    nm = norm_mix[:, None, :]
    n2 = norm_ffn2[:, None, :]
    grp = jnp.kron(jnp.eye(2 * A_HEADS, dtype=F32), jnp.ones((A_DK, A_DK), F32)).astype(BF16)
    qgain = jnp.tile(a_qk_norm[:, 0:1, :], (1, 1, 2 * A_HEADS))
    kgain = jnp.tile(a_qk_norm[:, 1:2, :], (1, 1, 2 * A_HEADS))
    gparams = jnp.stack([_pad_lanes(-jnp.exp(b_a_log)), _pad_lanes(b_dt_bias)], axis=1)
    subln = a_subln[:, None, :]
    bnorm = b_norm[:, None, :]
    cnorm = c_norm[:, None, :]
    cache_k2 = cache_k.reshape(n_att, n_pool, page * A_HEADS, 2 * A_DK)
    cache_v2 = cache_v.reshape(n_att, n_pool, page * A_HEADS, A_DV)

    bias_l2 = (rel_bias - rel_bias[N_BUCKETS - 1:N_BUCKETS]) * LOG2E
    bias_p, bias_s = _bias_tiles(rel_bias, tq, ATT_K_TILE, past, page, ts)
    bias_s = bias_s.reshape(A_HEADS * 2 * ts, page * A_HEADS + LANES)

    xp = x_prompt.reshape(bp * tp, d)
    xs = x_sample.reshape(bs * ts, d)
    tm_p = ROWS_DENSE
    tm_s = bs * ts
    k_rows = [[], []]
    v_rows = [[], []]
    deltas = [[], []]
    convs = [[], []]
    glas = [[], []]

    for i in range(depth):
        li = i // 2
        xp = _ffn(xp, n1, ffn_in_bf, ffn_out_bf, i, 0, tm_p)
        xs = _ffn(xs, n1, ffn_in_bf, ffn_out_bf, i, 0, tm_s)
        if i % 2 == 0:
            lam_init = 0.8 - 0.6 * math.exp(-0.3 * i)
            lq1, lk1, lq2, lk2 = a_lambda[li]
            lam = jnp.exp(jnp.sum(lq1 * lk1)) - jnp.exp(jnp.sum(lq2 * lk2)) + lam_init
            smax = (1.02 * A_DK * A_DK ** -0.5 * LOG2E) * jnp.max(jnp.abs(a_qk_norm[li, 0])) * jnp.max(
                jnp.abs(a_qk_norm[li, 1]))
            shift = smax + jnp.max(bias_l2, axis=0)
            fast = jnp.all(2.0 * smax + jnp.max(bias_l2, axis=0) - jnp.min(bias_l2, axis=0) <= SAFE_SPAN)
            sc = jnp.concatenate([jnp.stack([lam, jnp.asarray(1.0 - lam_init, F32), fast.astype(F32)]),
                                  shift]).astype(F32)

            q, k32, kbf, v32, vt, bqkv, bz, gb = _inproj_even(
                xp, nm, w_even_bf, w_even_small, grp, qgain, kgain, gparams, i, li, ROWS_INPROJ)
            oa = _attn_prompt(sc, q.reshape(bp, tp, A_QK), kbf.reshape(bp, tp, A_QK), vt, bias_p, subln, li, tq)
            bqkv3 = bqkv.reshape(bp, tp, B_CONV_DIM)
            ob, s_new = _delta(bqkv3, bz.reshape(bp, tp, B_V), gb.reshape(bp, tp, LANES),
                               jnp.zeros((bp, SUBLANES, B_CONV_DIM), F32),
                               jnp.zeros((bp, B_HEADS, B_DK, B_DV), F32), b_conv, bnorm, li, SEQ_CHUNK)
            mix_p, w_mix = (oa.reshape(bp * tp, A_V), ob.reshape(bp * tp, B_V)), w_out_even_bf
            k_rows[0].append(k32.reshape(bp, tp, A_HEADS, 2 * A_DK))
            v_rows[0].append(v32.reshape(bp, tp, A_HEADS, A_DV))
            deltas[0].append(s_new)
            convs[0].append(bqkv3[:, tp - (CONV_W - 1):, :])

            q, k32, kbf, v32, _, bqkv, bz, gb = _inproj_even(
                xs, nm, w_even_bf, w_even_small, grp, qgain, kgain, gparams, i, li, tm_s)
            q4 = q.reshape(bs, ts, A_HEADS, 2, A_DK)
            qbd = jnp.einsum('bthcd,ce->bhcted', q4, jnp.eye(2, dtype=BF16)).reshape(bs, A_HEADS * 2 * ts, 2 * A_DK)
            knew = _pad_rows(kbf.reshape(bs, ts * A_HEADS, 2 * A_DK), LANES)
            vnew = _pad_rows(v32.reshape(bs, ts * A_HEADS, A_DV).astype(BF16), LANES)
            oa = _attn_sample(page_table, sc, qbd, bias_s, knew, vnew, subln, cache_k2, cache_v2, li, ts)
            oa = oa[:, :ts, :].astype(BF16)
            bqkv3 = bqkv.reshape(bs, ts, B_CONV_DIM)
            conv0 = jnp.pad(state_conv[li], ((0, 0), (SUBLANES - (CONV_W - 1), 0), (0, 0)))
            ob, s_new = _delta(_pad_rows(bqkv3, cs), _pad_rows(bz.reshape(bs, ts, B_V), cs),
                               _pad_rows(gb.reshape(bs, ts, LANES), cs), conv0, state_delta[li],
                               b_conv, bnorm, li, cs)
            mix_s = (oa.reshape(bs * ts, A_V), ob[:, :ts, :].reshape(bs * ts, B_V))
            k_rows[1].append(k32.reshape(bs, ts, A_HEADS, 2 * A_DK))
            v_rows[1].append(v32.reshape(bs, ts, A_HEADS, A_DV))
            deltas[1].append(s_new)
            convs[1].append(jnp.concatenate([state_conv[li], bqkv3], axis=1)[:, ts:, :])
        else:
            q, k, v, r, gk = _inproj_odd(xp, nm, w_odd_bf, w_odd_small, gate_w2, gate_b, i, li, ROWS_INPROJ)
            o, s_new = _gla(q.reshape(bp, tp, C_QK), k.reshape(bp, tp, C_QK), v.reshape(bp, tp, C_V),
                            r.reshape(bp, tp, C_V), gk.reshape(bp, tp, C_QK),
                            jnp.zeros((bp, C_HEADS, C_DK, C_DV), F32), cnorm, li, SEQ_CHUNK)
            mix_p, w_mix = (o.reshape(bp * tp, C_V),), w_out_odd_bf
            glas[0].append(s_new)

            q, k, v, r, gk = _inproj_odd(xs, nm, w_odd_bf, w_odd_small, gate_w2, gate_b, i, li, tm_s)
            pad = lambda a, n: _pad_rows(a.reshape(bs, ts, n), cs)
            o, s_new = _gla(pad(q, C_QK), pad(k, C_QK), pad(v, C_V), pad(r, C_V), pad(gk, C_QK),
                            state_gla[li], cnorm, li, cs)
            mix_s = (o[:, :ts, :].reshape(bs * ts, C_V),)
            glas[1].append(s_new)
        xp = _ffn(xp, n2, ffn_in_bf, ffn_out_bf, i, 1, tm_p, mix_p, w_mix, li)
        xs = _ffn(xs, n2, ffn_in_bf, ffn_out_bf, i, 1, tm_s, mix_s, w_mix, li)

    return (xp.reshape(bp, tp, d), xs.reshape(bs, ts, d),
            jnp.stack(k_rows[0]), jnp.stack(v_rows[0]), jnp.stack(deltas[0]), jnp.stack(convs[0]),
            jnp.stack(glas[0]),
            jnp.stack(k_rows[1]), jnp.stack(v_rows[1]), jnp.stack(deltas[1]), jnp.stack(convs[1]),
            jnp.stack(glas[1]))
```

```python
import functools
import math

import jax
import jax.numpy as jnp
from jax import lax
from jax.experimental import pallas as pl
from jax.experimental.pallas import tpu as pltpu

F32 = jnp.float32
BF16 = jnp.bfloat16

A_HEADS, A_DK, A_DV = 4, 64, 128
B_HEADS, B_DK, B_DV = 4, 128, 128
CONV_W = 4
C_HEADS, C_DK, C_DV = 4, 128, 256
GLA_LOWRANK, GLA_TAU = 16, 16.0
N_BUCKETS, MAX_DIST = 32, 128
EPS = 1e-6
NEG = -1e30
LOG2E = 1.4426950408889634

A_QK = A_HEADS * 2 * A_DK
A_V = A_HEADS * A_DV
B_QK = B_HEADS * B_DK
B_V = B_HEADS * B_DV
B_CONV_DIM = 2 * B_QK + B_V
C_QK = C_HEADS * C_DK
C_V = C_HEADS * C_DV
IN_EVEN_MAIN = 2 * A_QK + A_V + B_CONV_DIM + B_V
IN_ODD_MAIN = 2 * C_QK + 2 * C_V

LANES = 128
SUBLANES = 8
PACKED_SUBLANES = 16
VMEM_LIMIT = 56 * 2 ** 20

ROWS_DENSE = 512
ROWS_INPROJ = 256
ATT_Q_TILE = 1024
ATT_K_TILE = ROWS_INPROJ
VT_ROWS = A_DV + PACKED_SUBLANES
SAFE_SPAN = 96.0
SEQ_CHUNK = 64
DELTA_PREP_CHUNKS = 8
DELTA_SCAN_CHUNKS = 4
GLA_SUB = 16
PAGES_PER_STEP = 32

NT = (((1,), (1,)), ((), ()))
TN = (((0,), (0,)), ((), ()))


def _cp(*sem):
    return pltpu.CompilerParams(dimension_semantics=sem, vmem_limit_bytes=VMEM_LIMIT)


def _mm(a, b, dn=None):
    if dn is None:
        return jnp.dot(a, b, preferred_element_type=F32)
    return lax.dot_general(a, b, dn, preferred_element_type=F32)


def _split2(x):
    hi = x.astype(BF16)
    lo = (x - hi.astype(F32)).astype(BF16)
    return hi, lo


def _split3(x):
    hi = x.astype(BF16)
    r = x - hi.astype(F32)
    mid = r.astype(BF16)
    lo = (r - mid.astype(F32)).astype(BF16)
    return hi, mid, lo


def _mm3(a, b):
    ah, al = _split2(a)
    bh, bl = _split2(b)
    return _mm(ah, bh) + (_mm(ah, bl) + _mm(al, bh))


def _mm_exact_lhs(a_bf, x):
    hi, mid, lo = _split3(x)
    return _mm(a_bf, hi) + (_mm(a_bf, mid) + _mm(a_bf, lo))


def _sigmoid(x):
    return 1.0 / (1.0 + jnp.exp(-x))


def _silu(x):
    return x * _sigmoid(x)


def _softplus(x):
    return jnp.maximum(x, 0.0) + jnp.log(1.0 + jnp.exp(-jnp.abs(x)))


def _rmsnorm(xf, g):
    return xf * lax.rsqrt(jnp.mean(xf * xf, axis=-1, keepdims=True) + EPS) * g


def _iota(shape, dim):
    return lax.broadcasted_iota(jnp.int32, shape, dim)


def _row_to_col(row, n):
    eye = _iota((n, n), 0) == _iota((n, n), 1)
    return jnp.sum(jnp.where(eye, jnp.broadcast_to(row, (n, n)), 0.0), axis=1, keepdims=True)


def _ffn_kernel(x_ref, g_ref, wg_ref, wu_ref, wo_ref, *rest):
    *mix_refs, o_ref, h_ref = rest
    j = pl.program_id(1)

    @pl.when(j == 0)
    def _():
        xf = x_ref[...]
        if mix_refs:
            wm_ref = mix_refs[-1]
            off = 0
            for m_ref in mix_refs[:-1]:
                n = m_ref.shape[1]
                xf = xf + _mm(m_ref[...], wm_ref[off:off + n, :])
                off += n
        h_ref[...] = _rmsnorm(xf, g_ref[...]).astype(BF16)
        o_ref[...] = xf

    h = h_ref[...]
    gate = _mm(h, wg_ref[...])
    up = _mm(h, wu_ref[...])
    a = (_silu(gate) * up).astype(BF16)
    o_ref[...] += 0.5 * _mm(a, wo_ref[...])


def _ffn(x, g, w_in, w_out, layer, which, tm, mix=(), w_mix=None, li=0):
    m, d = x.shape
    f = w_out.shape[2]
    nf = 2
    tf = f // nf
    assert f % nf == 0 and tf % LANES == 0 and m % tm == 0
    mix_specs = [pl.BlockSpec((tm, o.shape[1]), lambda i, j: (i, 0)) for o in mix]
    if mix:
        assert sum(o.shape[1] for o in mix) == w_mix.shape[1]
        mix_specs.append(pl.BlockSpec((None, w_mix.shape[1], d), lambda i, j: (li, 0, 0)))
    return pl.pallas_call(
        _ffn_kernel,
        grid=(m // tm, nf),
        in_specs=[
            pl.BlockSpec((tm, d), lambda i, j: (i, 0)),
            pl.BlockSpec((None, 1, d), lambda i, j: (layer, 0, 0)),
            pl.BlockSpec((None, None, d, tf), lambda i, j: (layer, which, 0, j)),
            pl.BlockSpec((None, None, d, tf), lambda i, j: (layer, which, 0, j + nf)),
            pl.BlockSpec((None, None, tf, d), lambda i, j: (layer, which, j, 0)),
        ] + mix_specs,
        out_specs=pl.BlockSpec((tm, d), lambda i, j: (i, 0)),
        out_shape=jax.ShapeDtypeStruct((m, d), F32),
        scratch_shapes=[pltpu.VMEM((tm, d), BF16)],
        compiler_params=_cp("parallel", "arbitrary"),
        name="ffn",
    )(x, g, w_in, w_in, w_out, *mix, *([w_mix] if mix else []))


def _group_rms(x, grp_bf, gain, width):
    hi, lo = _split2(x * x)
    ms = (_mm(hi, grp_bf) + _mm(lo, grp_bf)) * (1.0 / width)
    return x * lax.rsqrt(ms + EPS) * gain


def _inproj_even_kernel(x_ref, g_ref, w_ref, ws_ref, grp_ref, qg_ref, kg_ref, gp_ref,
                        q_ref, k32_ref, kbf_ref, v32_ref, vt_ref, bqkv_ref, bz_ref, gb_ref):
    h = _rmsnorm(x_ref[...], g_ref[...]).astype(BF16)
    grp = grp_ref[...]

    aq = _mm(h, w_ref[:, 0:A_QK])
    q = _group_rms(aq, grp, qg_ref[...], A_DK)
    q_ref[...] = (q * (A_DK ** -0.5 * LOG2E)).astype(BF16)

    ak = _mm(h, w_ref[:, A_QK:2 * A_QK])
    k = _group_rms(ak, grp, kg_ref[...], A_DK)
    kbf_ref[...] = k.astype(BF16)

    v = _mm(h, w_ref[:, 2 * A_QK:2 * A_QK + A_V])
    tm = v.shape[0]
    ones = jnp.ones((VT_ROWS - A_DV, tm), BF16)
    for hd in range(A_HEADS):
        k32_ref[pl.ds(hd, tm, stride=A_HEADS), :] = k[:, 2 * A_DK * hd:2 * A_DK * (hd + 1)]
        v32_ref[pl.ds(hd, tm, stride=A_HEADS), :] = v[:, A_DV * hd:A_DV * (hd + 1)]
        vt_ref[hd, 0:A_DV, :] = v[:, A_DV * hd:A_DV * (hd + 1)].T.astype(BF16)
        vt_ref[hd, A_DV:VT_ROWS, :] = ones

    o = 2 * A_QK + A_V
    bqkv_ref[...] = _mm(h, w_ref[:, o:o + B_CONV_DIM])
    o += B_CONV_DIM
    bz_ref[...] = _mm(h, w_ref[:, o:o + B_V])

    y = _mm(h, ws_ref[...])
    lane = _iota(y.shape, 1)
    g = gp_ref[0:1, :] * _softplus(y + gp_ref[1:2, :])
    gb_ref[...] = jnp.where(lane < B_HEADS, g, jnp.where(lane < 2 * B_HEADS, _sigmoid(y), 0.0))


def _inproj_even(x, g, w, ws, grp, qg, kg, gp, layer, li, tm):
    m, d = x.shape
    wmain = IN_EVEN_MAIN
    row = lambda n: pl.BlockSpec((tm, n), lambda i: (i, 0))
    out_shapes = [
        jax.ShapeDtypeStruct((m, A_QK), BF16),
        jax.ShapeDtypeStruct((m * A_HEADS, 2 * A_DK), F32),
        jax.ShapeDtypeStruct((m, A_QK), BF16),
        jax.ShapeDtypeStruct((m * A_HEADS, A_DV), F32),
        jax.ShapeDtypeStruct((A_HEADS, m // tm, VT_ROWS, tm), BF16),
        jax.ShapeDtypeStruct((m, B_CONV_DIM), F32),
        jax.ShapeDtypeStruct((m, B_V), F32),
        jax.ShapeDtypeStruct((m, LANES), F32),
    ]
    out_specs = [row(s.shape[1]) for s in out_shapes]
    out_specs[1] = pl.BlockSpec((tm * A_HEADS, 2 * A_DK), lambda i: (i, 0))
    out_specs[3] = pl.BlockSpec((tm * A_HEADS, A_DV), lambda i: (i, 0))
    out_specs[4] = pl.BlockSpec((A_HEADS, None, VT_ROWS, tm), lambda i: (0, i, 0, 0))
    return pl.pallas_call(
        _inproj_even_kernel,
        grid=(m // tm,),
        in_specs=[
            row(d),
            pl.BlockSpec((None, 1, d), lambda i: (layer, 0, 0)),
            pl.BlockSpec((None, d, wmain), lambda i: (li, 0, 0)),
            pl.BlockSpec((None, d, LANES), lambda i: (li, 0, 0)),
            pl.BlockSpec((A_QK, A_QK), lambda i: (0, 0)),
            pl.BlockSpec((None, 1, A_QK), lambda i: (li, 0, 0)),
            pl.BlockSpec((None, 1, A_QK), lambda i: (li, 0, 0)),
            pl.BlockSpec((None, 2, LANES), lambda i: (li, 0, 0)),
        ],
        out_specs=out_specs,
        out_shape=out_shapes,
        compiler_params=_cp("parallel"),
        name="inproj_even",
    )(x, g, w, ws, grp, qg, kg, gp)


def _bias_kernel(rb_ref, bp_ref, bs_ref, *, tq, tk, past, page, tnew):
    h = pl.program_id(0)
    far = rb_ref[N_BUCKETS - 1, h]
    exact = N_BUCKETS // 2

    def table(rel):
        n = jnp.maximum(rel, 0)
        nf = jnp.maximum(n, 1).astype(F32)
        large = exact + (jnp.log(nf / exact) / math.log(MAX_DIST / exact) * (N_BUCKETS - exact)).astype(jnp.int32)
        bucket = jnp.where(n < exact, n, jnp.minimum(large, N_BUCKETS - 1))
        val = jnp.zeros(rel.shape, F32)
        for b in range(N_BUCKETS):
            val = jnp.where(bucket == b, rb_ref[b, h], val)
        return jnp.where(rel >= 0, (val - far) * LOG2E, NEG)

    rows = SUBLANES

    def fill(c, carry):
        r0 = pl.multiple_of(c * rows, rows)
        key_p = r0 + _iota((rows, tq), 0)
        qry_p = _iota((rows, tq), 1)
        bp_ref[0, pl.ds(r0, rows), :] = table(qry_p - (key_p - tk))
        t1 = table(qry_p - key_p)
        bp_ref[1, pl.ds(r0, rows), :] = t1
        for n in range(2, tq // tk + 1):
            off = (n - 1) * tk
            bp_ref[n, pl.ds(r0, rows), 0:off] = jnp.full((rows, off), NEG, F32)
            bp_ref[n, pl.ds(r0, rows), off:tq] = t1[:, 0:tq - off]
        return carry

    lax.fori_loop(0, tk // rows, fill, 0)

    ncol = page * A_HEADS + LANES
    t_idx = _iota((2 * tnew, ncol), 0) & (tnew - 1)
    col = _iota((2 * tnew, ncol), 1)
    key = past - page + (col >> 2)
    rel_new = jnp.where(key - past < tnew, t_idx - (key - past), -1)
    rel = jnp.where(key < past, past + t_idx - key, rel_new)
    bs_ref[...] = table(jnp.where((col & (A_HEADS - 1)) == h, rel, -1))


def _bias_tiles(rel_bias, tq, tk, past, page, tnew):
    assert tnew & (tnew - 1) == 0 and 2 * tnew == SUBLANES and A_HEADS == 4 and tq % tk == 0
    ncol = page * A_HEADS + LANES
    ntile = tq // tk + 1
    return pl.pallas_call(
        functools.partial(_bias_kernel, tq=tq, tk=tk, past=past, page=page, tnew=tnew),
        grid=(A_HEADS,),
        in_specs=[pl.BlockSpec(memory_space=pltpu.SMEM)],
        out_specs=[
            pl.BlockSpec((None, ntile, tk, tq), lambda h: (h, 0, 0, 0)),
            pl.BlockSpec((None, 2 * tnew, ncol), lambda h: (h, 0, 0)),
        ],
        out_shape=[
            jax.ShapeDtypeStruct((A_HEADS, ntile, tk, tq), F32),
            jax.ShapeDtypeStruct((A_HEADS, 2 * tnew, ncol), F32),
        ],
        compiler_params=_cp("parallel"),
        name="t5_bias",
    )(rel_bias)


def _far_bucket_saturated(dist):
    exact = N_BUCKETS // 2
    v = math.log(dist / exact) / math.log(MAX_DIST / exact) * (N_BUCKETS - exact)
    return dist >= exact and exact + v >= N_BUCKETS - 1 + 1e-3


def _attn_prompt_kernel(sc_ref, q_ref, k_ref, vt_ref, bias_ref, sg_ref, o_ref, *, tq, tk):
    head = pl.program_id(1)
    i = pl.program_id(2)
    q = q_ref[...]
    lane = _iota(q.shape, 1)
    zero = jnp.zeros_like(q)
    qbd = jnp.concatenate([jnp.where(lane < A_DK, q, zero), jnp.where(lane >= A_DK, q, zero)], axis=0)
    ratio = tq // tk

    def biased(j, bias):
        start = pl.multiple_of(j * tk, tk)
        s = _mm(k_ref[pl.ds(start, tk), :], qbd, NT)
        return s if bias is None else s + jnp.concatenate([bias, bias], axis=1)

    def key_blocks(step, carry):
        first = i * ratio

        def trip(j, c):
            for u in range(ratio):
                c = step(ratio * j + u, c, None)
            return c

        def overlapping(c):
            for n in range(ratio):
                c = step(first + n, c, bias_ref[n + 1])
            return c

        def previous_tile(c):
            for u in range(ratio - 1):
                c = step(first - ratio + u, c, None)
            return overlapping(step(first - 1, c, bias_ref[0]))

        carry = lax.fori_loop(0, jnp.maximum(i - 1, 0), trip, carry)
        return lax.cond(i >= 1, previous_tile, overlapping, carry)

    def fixed_shift(_):
        shift = sc_ref[3 + head]

        def step(j, acc, bias):
            return acc + _mm(vt_ref[j], jnp.exp2(biased(j, bias) - shift).astype(BF16))

        return key_blocks(step, jnp.zeros((VT_ROWS, 2 * tq), F32))

    def running_max(_):
        def step(j, c, bias):
            m, acc = c
            s = biased(j, bias)
            m_new = jnp.maximum(m, jnp.max(s, axis=0, keepdims=True))
            p = jnp.exp2(s - m_new).astype(BF16)
            return m_new, jnp.exp2(m - m_new) * acc + _mm(vt_ref[j], p)

        return key_blocks(step, (jnp.full((1, 2 * tq), NEG, F32), jnp.zeros((VT_ROWS, 2 * tq), F32)))[1]

    acc = lax.cond(sc_ref[2] > 0.5, fixed_shift, running_max, 0)

    o = acc[0:A_DV] / acc[A_DV:A_DV + 1]
    o = o[:, 0:tq] - sc_ref[0] * o[:, tq:2 * tq]
    o = o * lax.rsqrt(jnp.mean(o * o, axis=0, keepdims=True) + EPS)
    o_ref[...] = (o.T * (sg_ref[...] * sc_ref[1])).astype(BF16)


def _attn_prompt(sc, q, k, vt, bias_p, sg, li, tq):
    bsz, t, _ = q.shape
    tk = vt.shape[3]
    assert tq % tk == 0 and t % tq == 0 and bias_p.shape[1:] == (tq // tk + 1, tk, tq)
    return pl.pallas_call(
        functools.partial(_attn_prompt_kernel, tq=tq, tk=tk),
        grid=(bsz, A_HEADS, t // tq),
        in_specs=[
            pl.BlockSpec(memory_space=pltpu.SMEM),
            pl.BlockSpec((None, tq, 2 * A_DK), lambda b, h, i: (b, i, h)),
            pl.BlockSpec((None, t, 2 * A_DK), lambda b, h, i: (b, 0, h)),
            pl.BlockSpec((None, t // tk, VT_ROWS, tk), lambda b, h, i: (h, b, 0, 0)),
            pl.BlockSpec((None, tq // tk + 1, tk, tq), lambda b, h, i: (h, 0, 0, 0)),
            pl.BlockSpec((None, 1, A_DV), lambda b, h, i: (li, 0, 0)),
        ],
        out_specs=pl.BlockSpec((None, tq, A_DV), lambda b, h, i: (b, i, h)),
        out_shape=jax.ShapeDtypeStruct((bsz, t, A_V), BF16),
        compiler_params=_cp("parallel", "parallel", "arbitrary"),
        name="attn_prompt",
    )(sc, q, k, vt, bias_p, sg)


def _attn_sample_kernel(pt_ref, sc_ref, qbd_ref, bias_ref, biasn_ref, kn_ref, vn_ref, sg_ref, *rest, npg, tnew):
    k_refs = rest[:npg]
    v_refs = rest[npg:2 * npg]
    o_ref, m_ref, l_ref, acc_ref, mask_ref = rest[2 * npg:]
    pg = pl.program_id(1)
    last = pg == pl.num_programs(1) - 1
    qbd = qbd_ref[...]

    @pl.when(pg == 0)
    def _():
        m_ref[...] = jnp.full(m_ref.shape, NEG, F32)
        l_ref[...] = jnp.zeros(l_ref.shape, F32)
        acc_ref[...] = jnp.zeros(acc_ref.shape, F32)
        row_head = _iota(mask_ref.shape, 0) >> ((2 * tnew).bit_length() - 1)
        col_head = _iota(mask_ref.shape, 1) & (A_HEADS - 1)
        mask_ref[...] = jnp.where(row_head == col_head, 0.0, NEG)

    def update(s, vs):
        m_old = m_ref[...]
        m_new = jnp.maximum(m_old, jnp.max(s, axis=1, keepdims=True))
        alpha = jnp.exp2(m_old - m_new)
        p = jnp.exp2(s - m_new)
        l_ref[...] = alpha * l_ref[...] + jnp.sum(p, axis=1, keepdims=True)
        pb = p.astype(BF16)
        w = pb.shape[1] // len(vs)
        pv = _mm(pb[:, 0:w], vs[0])
        for g in range(1, len(vs)):
            pv = pv + _mm(pb[:, g * w:(g + 1) * w], vs[g])
        acc_ref[...] = alpha * acc_ref[...] + pv
        m_ref[...] = m_new

    mask = mask_ref[...]
    final_page = jnp.where(last, bias_ref[...], mask)
    s = jnp.concatenate([_mm(qbd, kr[...].astype(BF16), NT) + (mask if g < npg - 1 else final_page)
                         for g, kr in enumerate(k_refs)], axis=1)
    update(s, [vr[...].astype(BF16) for vr in v_refs])

    @pl.when(last)
    def _():
        update(_mm(qbd, kn_ref[...], NT) + biasn_ref[...], [vn_ref[...]])
        lam = sc_ref[0]
        rows = 2 * tnew
        for h in range(A_HEADS):
            a = acc_ref[rows * h:rows * (h + 1), :] / l_ref[rows * h:rows * (h + 1), :]
            o = a - lam * pltpu.roll(a, tnew, axis=0)
            o_ref[:, A_DV * h:A_DV * (h + 1)] = _rmsnorm(o, sg_ref[...]) * sc_ref[1]


def _attn_sample(page_table, sc, qbd, bias_s, knew, vnew, sg, cache_k, cache_v, li, tnew):
    bsz, nrow, _ = qbd.shape
    n_pages = page_table.shape[1]
    npg = PAGES_PER_STEP
    assert n_pages % npg == 0
    prow = cache_k.shape[2]
    assert bias_s.shape == (nrow, prow + LANES)

    def kvspec(g):
        return pl.BlockSpec((None, None, prow, 2 * A_DK), lambda b, pg, pt: (li, pt[b, pg * npg + g], 0, 0))

    grid_spec = pltpu.PrefetchScalarGridSpec(
        num_scalar_prefetch=1,
        grid=(bsz, n_pages // npg),
        in_specs=[
            pl.BlockSpec(memory_space=pltpu.SMEM),
            pl.BlockSpec((None, nrow, 2 * A_DK), lambda b, pg, pt: (b, 0, 0)),
            pl.BlockSpec((nrow, prow), lambda b, pg, pt: (0, 0)),
            pl.BlockSpec((nrow, LANES), lambda b, pg, pt: (0, prow // LANES)),
            pl.BlockSpec((None, LANES, 2 * A_DK), lambda b, pg, pt: (b, 0, 0)),
            pl.BlockSpec((None, LANES, A_DV), lambda b, pg, pt: (b, 0, 0)),
            pl.BlockSpec((None, 1, A_DV), lambda b, pg, pt: (li, 0, 0)),
        ] + [kvspec(g) for g in range(npg)] + [kvspec(g) for g in range(npg)],
        out_specs=pl.BlockSpec((None, 2 * tnew, A_V), lambda b, pg, pt: (b, 0, 0)),
        scratch_shapes=[pltpu.VMEM((nrow, 1), F32), pltpu.VMEM((nrow, 1), F32), pltpu.VMEM((nrow, A_DV), F32),
                        pltpu.VMEM((nrow, prow), F32)],
    )
    return pl.pallas_call(
        functools.partial(_attn_sample_kernel, npg=npg, tnew=tnew),
        grid_spec=grid_spec,
        out_shape=jax.ShapeDtypeStruct((bsz, 2 * tnew, A_V), F32),
        compiler_params=_cp("parallel", "arbitrary"),
        name="attn_sample",
    )(page_table, sc, qbd, bias_s, bias_s, knew, vnew, sg, *([cache_k] * npg), *([cache_v] * npg))


def _l2norm(x):
    return x * lax.rsqrt(jnp.sum(x * x, axis=-1, keepdims=True) + EPS)


def _unit_lower_inverses(mats, c):
    n = mats[0].shape[0]
    nblk = n // c
    shift = c.bit_length() - 1
    same = (_iota((n, n), 0) >> shift) == (_iota((n, n), 1) >> shift)
    eye_w = ((_iota((c, n), 1) & (c - 1)) == _iota((c, n), 0)).astype(F32)

    def wide(sq):
        out = sq[0:c]
        for r in range(1, nblk):
            out = out + sq[r * c:(r + 1) * c]
        return out

    def square(w):
        return jnp.where(same, jnp.concatenate([w] * nblk, axis=0), jnp.zeros((), w.dtype))

    def mm3_split(lhs_split, rhs_split):
        (lh, ll), (rh, rl) = lhs_split, rhs_split
        return _mm(lh, rh) + (_mm(lh, rl) + _mm(ll, rh))

    qw = [_split2(wide(a)) for a in mats]
    qsq = [(square(h), square(l)) for h, l in qw]
    ps = [eye_w - wide(a) for a in mats]
    span = 2
    while span < c:
        qw = [_split2(mm3_split(q, s)) for q, s in zip(qw, qsq)]
        qsq = [(square(h), square(l)) for h, l in qw]
        ps = [p + mm3_split(_split2(p), s) for p, s in zip(ps, qsq)]
        span *= 2
    return [square(p) for p in ps]


def _cumsum_rows(x, n):
    row = _iota(x.shape, 0)
    k = 1
    while k < n:
        x = x + jnp.where(row >= k, pltpu.roll(x, k, axis=0), 0.0)
        k *= 2
    return x


def _delta_prep_kernel(x_ref, halo_ref, c0_ref, gb_ref, cw_ref, u_ref, wk_ref, qe_ref, kd_ref, qk_ref, xbuf,
                       *, c, nck):
    ci = pl.program_id(1)
    ch = B_HEADS * c
    rows = nck * c
    xbuf[0:SUBLANES, :] = jnp.where(ci == 0, c0_ref[...], halo_ref[...])
    xbuf[SUBLANES:SUBLANES + rows, :] = x_ref[...]
    first = SUBLANES - (CONV_W - 1)
    conv = xbuf[pl.ds(first, rows), :] * cw_ref[0:1, :]
    for j in range(1, CONV_W):
        conv = conv + xbuf[pl.ds(first + j, rows), :] * cw_ref[j:j + 1, :]
    conv = _silu(conv)

    row = _iota((ch, ch), 0)
    col = _iota((ch, ch), 1)
    shift = c.bit_length() - 1
    same = (row >> shift) == (col >> shift)
    incl = same & (row >= col)
    strict = same & (row > col)

    def stack_lane(a, off):
        return jnp.concatenate([a[:, off + h:off + h + 1] for h in range(B_HEADS)], axis=0)

    parts = []
    for n in range(nck):
        rs = slice(n * c, (n + 1) * c)

        def stack(off, width):
            return jnp.concatenate([conv[rs, off + width * h:off + width * (h + 1)] for h in range(B_HEADS)], axis=0)

        q = _l2norm(stack(0, B_DK)) * (B_DK ** -0.5)
        k = _l2norm(stack(B_QK, B_DK))
        v = stack(2 * B_QK, B_DV)
        gb = gb_ref[rs, :]
        gcum = _cumsum_rows(gb, c)
        gcol = stack_lane(gcum, 0)
        beta = stack_lane(gb, B_HEADS)
        glast = stack_lane(jnp.broadcast_to(gcum[c - 1:c, :], gcum.shape), 0)
        grow = jnp.broadcast_to(gcol, (ch, LANES)).T[0:1, :]
        dec = jnp.where(incl, jnp.exp(jnp.where(incl, gcol - grow, 0.0)), 0.0)
        kb = k.astype(BF16)
        a = jnp.where(strict, _mm(kb, kb, NT) * dec * beta, 0.0)
        egc = jnp.exp(gcol)
        qe_ref[n] = (q * egc).astype(BF16)
        kd_ref[n] = (k * jnp.exp(glast - gcol)).astype(BF16)
        qk_ref[n] = (_mm(q.astype(BF16), kb, NT) * dec).astype(BF16)
        parts.append((a, jnp.concatenate([v * beta, k * (beta * egc)], axis=1)))

    t_invs = _unit_lower_inverses([a for a, _ in parts], c)
    for n in range(nck):
        uw = _mm3(t_invs[n], parts[n][1])
        u_ref[n] = uw[:, 0:B_DV]
        wk_ref[n] = uw[:, B_DV:B_DV + B_DK].astype(BF16)


def _delta_scan_kernel(u_ref, wk_ref, qe_ref, kd_ref, qk_ref, gb_ref, z_ref, s0_ref, bn_ref, ob_ref, sout_ref,
                       s_scr, *, c, nb, cps):
    ci = pl.program_id(1)

    @pl.when(ci == 0)
    def _():
        s_scr[...] = s0_ref[...]

    hs = [slice(c * h, c * (h + 1)) for h in range(B_HEADS)]
    for n in range(cps):
        rows = slice(c * n, c * (n + 1))
        for b in range(nb):
            decay = jnp.exp(jnp.sum(gb_ref[b, rows, :], axis=0, keepdims=True))
            s = [s_scr[b, h] for h in range(B_HEADS)]
            s_bf = [x.astype(BF16) for x in s]
            v_new = u_ref[b, n] - jnp.concatenate(
                [_mm(wk_ref[b, n, hs[h], :], s_bf[h]) for h in range(B_HEADS)], axis=0)
            vn_bf = v_new.astype(BF16)
            o = (jnp.concatenate([_mm(qe_ref[b, n, hs[h], :], s_bf[h]) for h in range(B_HEADS)], axis=0)
                 + _mm(qk_ref[b, n], vn_bf))
            for h in range(B_HEADS):
                s_scr[b, h] = s[h] * decay[:, h:h + 1] + _mm(kd_ref[b, n, hs[h], :], vn_bf[hs[h]], TN)
                sl = slice(B_DV * h, B_DV * (h + 1))
                ob_ref[b, rows, sl] = (_rmsnorm(o[hs[h]], bn_ref[...]) * _silu(z_ref[b, rows, sl])).astype(BF16)

    @pl.when(ci == pl.num_programs(1) - 1)
    def _():
        sout_ref[...] = s_scr[...]


def _delta(bqkv, bz, gb, conv0, s0, cw, bn, li, c):
    bsz, t, _ = bqkv.shape
    assert t % c == 0 and c % SUBLANES == 0 and c & (c - 1) == 0
    nch = t // c
    ch = B_HEADS * c
    nck = min(DELTA_PREP_CHUNKS, nch)
    assert nch % nck == 0
    hb = nck * c // SUBLANES
    tok = lambda n: pl.BlockSpec((None, c, n), lambda b, ci: (b, ci, 0))
    step_tok = lambda n: pl.BlockSpec((None, nck * c, n), lambda b, ci: (b, ci, 0))
    step_chunks = lambda n: pl.BlockSpec((None, nck, ch, n), lambda b, ci: (b, ci, 0, 0))
    chunk_shape = lambda n, dt: jax.ShapeDtypeStruct((bsz, nch, ch, n), dt)
    u, wk, qe, kd, qk = pl.pallas_call(
        functools.partial(_delta_prep_kernel, c=c, nck=nck),
        grid=(bsz, nch // nck),
        in_specs=[
            step_tok(B_CONV_DIM),
            pl.BlockSpec((None, SUBLANES, B_CONV_DIM), lambda b, ci: (b, jnp.maximum(ci * hb - 1, 0), 0)),
            pl.BlockSpec((None, SUBLANES, B_CONV_DIM), lambda b, ci: (b, 0, 0)),
            step_tok(LANES),
            pl.BlockSpec((None, CONV_W, B_CONV_DIM), lambda b, ci: (li, 0, 0)),
        ],
        out_specs=[step_chunks(B_DV), step_chunks(B_DK), step_chunks(B_DK), step_chunks(B_DK), step_chunks(ch)],
        out_shape=[chunk_shape(B_DV, F32), chunk_shape(B_DK, BF16), chunk_shape(B_DK, BF16),
                   chunk_shape(B_DK, BF16), chunk_shape(ch, BF16)],
        scratch_shapes=[pltpu.VMEM((SUBLANES + nck * c, B_CONV_DIM), F32)],
        compiler_params=_cp("parallel", "parallel"),
        name="delta_prep",
    )(bqkv, bqkv, conv0, gb, cw)

    nb = 2
    cps = DELTA_SCAN_CHUNKS if nch % DELTA_SCAN_CHUNKS == 0 else 1
    assert bsz % nb == 0
    grp_chunk = lambda n: pl.BlockSpec((nb, cps, ch, n), lambda b, ci: (b, ci, 0, 0))
    grp_tok = lambda n: pl.BlockSpec((nb, cps * c, n), lambda b, ci: (b, ci, 0))
    state = pl.BlockSpec((nb, B_HEADS, B_DK, B_DV), lambda b, ci: (b, 0, 0, 0))
    return pl.pallas_call(
        functools.partial(_delta_scan_kernel, c=c, nb=nb, cps=cps),
        grid=(bsz // nb, nch // cps),
        in_specs=[grp_chunk(B_DV), grp_chunk(B_DK), grp_chunk(B_DK), grp_chunk(B_DK), grp_chunk(ch),
                  grp_tok(LANES), grp_tok(B_V), state,
                  pl.BlockSpec((None, 1, B_DV), lambda b, ci: (li, 0, 0))],
        out_specs=[grp_tok(B_V), state],
        out_shape=[jax.ShapeDtypeStruct((bsz, t, B_V), BF16),
                   jax.ShapeDtypeStruct((bsz, B_HEADS, B_DK, B_DV), F32)],
        scratch_shapes=[pltpu.VMEM((nb, B_HEADS, B_DK, B_DV), F32)],
        compiler_params=_cp("parallel", "arbitrary"),
        name="delta_scan",
    )(u, wk, qe, kd, qk, gb, bz, s0, bn)


def _inproj_odd_kernel(x_ref, g_ref, w_ref, ws_ref, w2_ref, gbias_ref, q_ref, k_ref, v_ref, r_ref, gk_ref):
    h = _rmsnorm(x_ref[...], g_ref[...]).astype(BF16)
    q_ref[...] = _mm(h, w_ref[:, 0:C_QK])
    k_ref[...] = _mm(h, w_ref[:, C_QK:2 * C_QK])
    v_ref[...] = _mm(h, w_ref[:, 2 * C_QK:2 * C_QK + C_V]).astype(BF16)
    r_ref[...] = _mm(h, w_ref[:, 2 * C_QK + C_V:2 * C_QK + 2 * C_V])
    glr = _mm(h, ws_ref[...]).astype(BF16)
    y = _mm(glr, w2_ref[...]) + gbias_ref[...]
    gk_ref[...] = -_softplus(-y) * (1.0 / GLA_TAU)


def _inproj_odd(x, g, w, ws, w2, gbias, layer, li, tm):
    m, d = x.shape
    wmain = 2 * C_QK + 2 * C_V
    row = lambda n: pl.BlockSpec((tm, n), lambda i: (i, 0))
    out_shapes = [
        jax.ShapeDtypeStruct((m, C_QK), F32),
        jax.ShapeDtypeStruct((m, C_QK), F32),
        jax.ShapeDtypeStruct((m, C_V), BF16),
        jax.ShapeDtypeStruct((m, C_V), F32),
        jax.ShapeDtypeStruct((m, C_QK), F32),
    ]
    return pl.pallas_call(
        _inproj_odd_kernel,
        grid=(m // tm,),
        in_specs=[
            row(d),
            pl.BlockSpec((None, 1, d), lambda i: (layer, 0, 0)),
            pl.BlockSpec((None, d, wmain), lambda i: (li, 0, 0)),
            pl.BlockSpec((None, d, LANES), lambda i: (li, 0, 0)),
            pl.BlockSpec((None, LANES, C_QK), lambda i: (li, 0, 0)),
            pl.BlockSpec((None, 1, C_QK), lambda i: (li, 0, 0)),
        ],
        out_specs=[row(s.shape[1]) for s in out_shapes],
        out_shape=out_shapes,
        compiler_params=_cp("parallel"),
        name="inproj_odd",
    )(x, g, w, ws, w2, gbias)


def _gla_kernel(q_ref, k_ref, v_ref, r_ref, gk_ref, s0_ref, cn_ref, o_ref, sout_ref, s_scr, *, c, sub, nb):
    ci = pl.program_id(1)

    @pl.when(ci == 0)
    def _():
        s_scr[...] = s0_ref[...]

    for b in range(nb):
        _gla_chunk(q_ref.at[b], k_ref.at[b], v_ref.at[b], r_ref.at[b], gk_ref.at[b], cn_ref, o_ref.at[b],
                   s_scr.at[b], c, sub)

    @pl.when(ci == pl.num_programs(1) - 1)
    def _():
        sout_ref[...] = s_scr[...]


def _gla_chunk(q_ref, k_ref, v_ref, r_ref, gk_ref, cn_ref, o_ref, s_scr, c, sub):
    row = _iota((c, c), 0)
    col = _iota((c, c), 1)
    lower_bf = (row >= col).astype(BF16)
    in_sub = row & (sub - 1)
    sub_start = row - in_sub
    b_all = _mm_exact_lhs(lower_bf, gk_ref[...])

    for h in range(C_HEADS):
        sl = slice(C_DK * h, C_DK * (h + 1))
        vl = slice(C_DV * h, C_DV * (h + 1))
        q = q_ref[:, sl] * (C_DK ** -0.5)
        k = k_ref[:, sl]
        b = b_all[:, sl]
        v = v_ref[:, vl]
        s = s_scr[h]

        o = _mm((q * jnp.exp(b)).astype(BF16), s.astype(BF16))

        blocks = [jnp.zeros((sub, c), F32)]
        for r in range(1, c // sub):
            ref = b[sub * r:sub * r + 1, :]
            qt = q[sub * r:sub * (r + 1)] * jnp.exp(b[sub * r:sub * (r + 1)] - ref)
            kt = k * jnp.exp(jnp.minimum(ref - b, 0.0))
            blocks.append(_mm(qt.astype(BF16), kt.astype(BF16), NT))
        a = jnp.where(col < sub_start, jnp.concatenate(blocks, axis=0), 0.0)
        b2 = b * LOG2E
        for d in range(sub):
            kd = k if d == 0 else pltpu.roll(k, d, axis=0)
            bd = b2 if d == 0 else pltpu.roll(b2, d, axis=0)
            ad = jnp.sum(q * kd * jnp.exp2(b2 - bd), axis=1, keepdims=True)
            a = jnp.where((col == row - d) & (in_sub >= d), ad, a)
        o = o + _mm(a.astype(BF16), v)

        blast = b[c - 1:c, :]
        kdec = (k * jnp.exp(blast - b)).astype(BF16)
        s_scr[h] = s * _row_to_col(jnp.exp(blast), C_DK) + _mm(kdec, v, TN)

        o_ref[:, vl] = (_rmsnorm(o, cn_ref[...]) * _silu(r_ref[:, vl])).astype(BF16)


def _gla(q, k, v, r, gk, s0, cn, li, c):
    bsz, t, _ = q.shape
    nb = 2
    assert t % c == 0 and bsz % nb == 0
    sub = min(GLA_SUB, c)
    tok = lambda n: pl.BlockSpec((nb, c, n), lambda b, ci: (b, ci, 0))
    state = pl.BlockSpec((nb, C_HEADS, C_DK, C_DV), lambda b, ci: (b, 0, 0, 0))
    return pl.pallas_call(
        functools.partial(_gla_kernel, c=c, sub=sub, nb=nb),
        grid=(bsz // nb, t // c),
        in_specs=[tok(C_QK), tok(C_QK), tok(C_V), tok(C_V), tok(C_QK), state,
                  pl.BlockSpec((None, 1, C_DV), lambda b, ci: (li, 0, 0))],
        out_specs=[tok(C_V), state],
        out_shape=[jax.ShapeDtypeStruct((bsz, t, C_V), BF16),
                   jax.ShapeDtypeStruct((bsz, C_HEADS, C_DK, C_DV), F32)],
        scratch_shapes=[pltpu.VMEM((nb, C_HEADS, C_DK, C_DV), F32)],
        compiler_params=_cp("parallel", "arbitrary"),
        name="gla",
    )(q, k, v, r, gk, s0, cn)


def _pad_rows(a, rows):
    return jnp.pad(a, ((0, 0), (0, rows - a.shape[1]), (0, 0)))


def _pad_lanes(a):
    return jnp.pad(a, [(0, 0)] * (a.ndim - 1) + [(0, LANES - a.shape[-1])])


def kernel(x_prompt, x_sample, cache_k, cache_v, state_delta, state_conv, state_gla, page_table, rel_bias, norm_ffn1, norm_mix, norm_ffn2, ffn_in, ffn_out, w_in_even, a_qk_norm, a_lambda, a_subln, b_conv, b_a_log, b_dt_bias, b_norm, w_out_even, w_in_odd, c_gate_w2, c_gate_b, c_norm, w_out_odd):
    bp, tp, d = x_prompt.shape
    bs, ts, _ = x_sample.shape
    depth = norm_ffn1.shape[0]
    n_att = w_in_even.shape[0]
    n_pool, page = cache_k.shape[1], cache_k.shape[2]
    past = page_table.shape[1] * page
    tq = ATT_Q_TILE
    cs = SUBLANES
    assert page == LANES and tp % tq == 0 and ts <= cs and ts >= CONV_W - 1
    assert _far_bucket_saturated(ATT_K_TILE + 1) and _far_bucket_saturated(page + 1)

    ffn_in_bf = ffn_in.astype(BF16)
    ffn_out_bf = ffn_out.astype(BF16)
    w_even_bf = w_in_even.astype(BF16)
    w_even_small = _pad_lanes(w_in_even[:, :, IN_EVEN_MAIN:]).astype(BF16)
    w_out_even_bf = w_out_even.astype(BF16)
    w_odd_bf = w_in_odd.astype(BF16)
    w_odd_small = _pad_lanes(w_in_odd[:, :, IN_ODD_MAIN:]).astype(BF16)
    w_out_odd_bf = w_out_odd.astype(BF16)
    gate_w2 = jnp.pad(c_gate_w2, ((0, 0), (0, LANES - GLA_LOWRANK), (0, 0))).astype(BF16)
    gate_b = c_gate_b[:, None, :]
    n1 = norm_ffn1[:, None, :]
    nm = norm_mix[:, None, :]
    n2 = norm_ffn2[:, None, :]
    grp = jnp.kron(jnp.eye(2 * A_HEADS, dtype=F32), jnp.ones((A_DK, A_DK), F32)).astype(BF16)
    qgain = jnp.tile(a_qk_norm[:, 0:1, :], (1, 1, 2 * A_HEADS))
    kgain = jnp.tile(a_qk_norm[:, 1:2, :], (1, 1, 2 * A_HEADS))
    gparams = jnp.stack([_pad_lanes(-jnp.exp(b_a_log)), _pad_lanes(b_dt_bias)], axis=1)
    subln = a_subln[:, None, :]
    bnorm = b_norm[:, None, :]
    cnorm = c_norm[:, None, :]
    cache_k2 = cache_k.reshape(n_att, n_pool, page * A_HEADS, 2 * A_DK)
    cache_v2 = cache_v.reshape(n_att, n_pool, page * A_HEADS, A_DV)

    bias_l2 = (rel_bias - rel_bias[N_BUCKETS - 1:N_BUCKETS]) * LOG2E
    bias_p, bias_s = _bias_tiles(rel_bias, tq, ATT_K_TILE, past, page, ts)
    bias_s = bias_s.reshape(A_HEADS * 2 * ts, page * A_HEADS + LANES)

    xp = x_prompt.reshape(bp * tp, d)
    xs = x_sample.reshape(bs * ts, d)
    tm_p = ROWS_DENSE
    tm_s = bs * ts
    k_rows = [[], []]
    v_rows = [[], []]
    deltas = [[], []]
    convs = [[], []]
    glas = [[], []]

    for i in range(depth):
        li = i // 2
        xp = _ffn(xp, n1, ffn_in_bf, ffn_out_bf, i, 0, tm_p)
        xs = _ffn(xs, n1, ffn_in_bf, ffn_out_bf, i, 0, tm_s)
        if i % 2 == 0:
            lam_init = 0.8 - 0.6 * math.exp(-0.3 * i)
            lq1, lk1, lq2, lk2 = a_lambda[li]
            lam = jnp.exp(jnp.sum(lq1 * lk1)) - jnp.exp(jnp.sum(lq2 * lk2)) + lam_init
            smax = (1.02 * A_DK * A_DK ** -0.5 * LOG2E) * jnp.max(jnp.abs(a_qk_norm[li, 0])) * jnp.max(
                jnp.abs(a_qk_norm[li, 1]))
            shift = smax + jnp.max(bias_l2, axis=0)
            fast = jnp.all(2.0 * smax + jnp.max(bias_l2, axis=0) - jnp.min(bias_l2, axis=0) <= SAFE_SPAN)
            sc = jnp.concatenate([jnp.stack([lam, jnp.asarray(1.0 - lam_init, F32), fast.astype(F32)]),
                                  shift]).astype(F32)

            q, k32, kbf, v32, vt, bqkv, bz, gb = _inproj_even(
                xp, nm, w_even_bf, w_even_small, grp, qgain, kgain, gparams, i, li, ROWS_INPROJ)
            oa = _attn_prompt(sc, q.reshape(bp, tp, A_QK), kbf.reshape(bp, tp, A_QK), vt, bias_p, subln, li, tq)
            bqkv3 = bqkv.reshape(bp, tp, B_CONV_DIM)
            ob, s_new = _delta(bqkv3, bz.reshape(bp, tp, B_V), gb.reshape(bp, tp, LANES),
                               jnp.zeros((bp, SUBLANES, B_CONV_DIM), F32),
                               jnp.zeros((bp, B_HEADS, B_DK, B_DV), F32), b_conv, bnorm, li, SEQ_CHUNK)
            mix_p, w_mix = (oa.reshape(bp * tp, A_V), ob.reshape(bp * tp, B_V)), w_out_even_bf
            k_rows[0].append(k32.reshape(bp, tp, A_HEADS, 2 * A_DK))
            v_rows[0].append(v32.reshape(bp, tp, A_HEADS, A_DV))
            deltas[0].append(s_new)
            convs[0].append(bqkv3[:, tp - (CONV_W - 1):, :])

            q, k32, kbf, v32, _, bqkv, bz, gb = _inproj_even(
                xs, nm, w_even_bf, w_even_small, grp, qgain, kgain, gparams, i, li, tm_s)
            q4 = q.reshape(bs, ts, A_HEADS, 2, A_DK)
            qbd = jnp.einsum('bthcd,ce->bhcted', q4, jnp.eye(2, dtype=BF16)).reshape(bs, A_HEADS * 2 * ts, 2 * A_DK)
            knew = _pad_rows(kbf.reshape(bs, ts * A_HEADS, 2 * A_DK), LANES)
            vnew = _pad_rows(v32.reshape(bs, ts * A_HEADS, A_DV).astype(BF16), LANES)
            oa = _attn_sample(page_table, sc, qbd, bias_s, knew, vnew, subln, cache_k2, cache_v2, li, ts)
            oa = oa[:, :ts, :].astype(BF16)
            bqkv3 = bqkv.reshape(bs, ts, B_CONV_DIM)
            conv0 = jnp.pad(state_conv[li], ((0, 0), (SUBLANES - (CONV_W - 1), 0), (0, 0)))
            ob, s_new = _delta(_pad_rows(bqkv3, cs), _pad_rows(bz.reshape(bs, ts, B_V), cs),
                               _pad_rows(gb.reshape(bs, ts, LANES), cs), conv0, state_delta[li],
                               b_conv, bnorm, li, cs)
            mix_s = (oa.reshape(bs * ts, A_V), ob[:, :ts, :].reshape(bs * ts, B_V))
            k_rows[1].append(k32.reshape(bs, ts, A_HEADS, 2 * A_DK))
            v_rows[1].append(v32.reshape(bs, ts, A_HEADS, A_DV))
            deltas[1].append(s_new)
            convs[1].append(jnp.concatenate([state_conv[li], bqkv3], axis=1)[:, ts:, :])
        else:
            q, k, v, r, gk = _inproj_odd(xp, nm, w_odd_bf, w_odd_small, gate_w2, gate_b, i, li, ROWS_INPROJ)
            o, s_new = _gla(q.reshape(bp, tp, C_QK), k.reshape(bp, tp, C_QK), v.reshape(bp, tp, C_V),
                            r.reshape(bp, tp, C_V), gk.reshape(bp, tp, C_QK),
                            jnp.zeros((bp, C_HEADS, C_DK, C_DV), F32), cnorm, li, SEQ_CHUNK)
            mix_p, w_mix = (o.reshape(bp * tp, C_V),), w_out_odd_bf
            glas[0].append(s_new)

            q, k, v, r, gk = _inproj_odd(xs, nm, w_odd_bf, w_odd_small, gate_w2, gate_b, i, li, tm_s)
            pad = lambda a, n: _pad_rows(a.reshape(bs, ts, n), cs)
            o, s_new = _gla(pad(q, C_QK), pad(k, C_QK), pad(v, C_V), pad(r, C_V), pad(gk, C_QK),
                            state_gla[li], cnorm, li, cs)
            mix_s = (o[:, :ts, :].reshape(bs * ts, C_V),)
            glas[1].append(s_new)
        xp = _ffn(xp, n2, ffn_in_bf, ffn_out_bf, i, 1, tm_p, mix_p, w_mix, li)
        xs = _ffn(xs, n2, ffn_in_bf, ffn_out_bf, i, 1, tm_s, mix_s, w_mix, li)

    return (xp.reshape(bp, tp, d), xs.reshape(bs, ts, d),
            jnp.stack(k_rows[0]), jnp.stack(v_rows[0]), jnp.stack(deltas[0]), jnp.stack(convs[0]),
            jnp.stack(glas[0]),
            jnp.stack(k_rows[1]), jnp.stack(v_rows[1]), jnp.stack(deltas[1]), jnp.stack(convs[1]),
            jnp.stack(glas[1]))
```
